```python
import math
import jax, jax.numpy as jnp
from jax import lax
import numpy as np

D_MODEL = 1024
BATCH = 16
SEQ = 2048
DEPTH = 1
DEC_BATCH = 32
DEC_SEQ = 32
PAST_LEN = 2048

CHUNK = 64
CONV_WIDTH = 31
CONV_DIM = 512
ATT_HEADS = 4
ATT_HEAD_DIM = 64
ATT_V_DIM = 2 * ATT_HEAD_DIM
ATT_DIM = ATT_HEADS * ATT_V_DIM
QK_DIM = ATT_HEADS * 2 * ATT_HEAD_DIM
MIX_DIM = CONV_DIM + ATT_DIM
IN_DIM = 2 * CONV_DIM + 2 * QK_DIM + ATT_DIM
ROPE_THETA = 10000.0
Q_BLOCK = 128
PEER_HEADS = 8
PEER_KEYS = 128
PEER_N = PEER_KEYS * PEER_KEYS
PEER_QDIM = 256
PEER_HALF = PEER_QDIM // 2
PEER_TOPK = 16
PEER_TOK_BLOCK = 256
LN_EPS = 1e-5
DEEPNORM_ALPHA = (2 * DEPTH) ** 0.25
DEEPNORM_BETA = (8 * DEPTH) ** -0.25

kernel_name = 'hymba_conformer_diffattn_peer_stream_step'


def layer_norm(x, g, b):
    x32 = x.astype(jnp.float32)
    mu = jnp.mean(x32, -1, keepdims=True)
    var = jnp.mean(jnp.square(x32 - mu), -1, keepdims=True)
    return ((x32 - mu) * lax.rsqrt(var + LN_EPS) * g.astype(jnp.float32) + b.astype(jnp.float32)).astype(x.dtype)


def rope(x, pos):
    half = ATT_HEAD_DIM // 2
    inv = 1.0 / (ROPE_THETA ** (jnp.arange(half, dtype=jnp.float32) / half))
    ang = pos.astype(jnp.float32)[:, None] * inv[None, :]
    cos = jnp.cos(ang)[:, None, None, :]
    sin = jnp.sin(ang)[:, None, None, :]
    x32 = x.astype(jnp.float32)
    x1, x2 = x32[..., :half], x32[..., half:]
    return jnp.concatenate([x1 * cos - x2 * sin, x2 * cos + x1 * sin], -1).astype(x.dtype)


def diff_attend(q, k, v, q_pos, k_pos, lam, subln_g, lam_init):
    s = jnp.einsum('bqhpd,bkhpd->bhpqk', q, k).astype(jnp.float32) * (ATT_HEAD_DIM ** -0.5)
    allowed = (k_pos[None, :] // CHUNK) <= (q_pos[:, None] // CHUNK)
    s = jnp.where(allowed, s, -jnp.inf)
    p = jax.nn.softmax(s, axis=-1)
    a = p[:, :, 0] - lam * p[:, :, 1]
    o = jnp.einsum('bhqk,bkhe->bqhe', a.astype(v.dtype), v).astype(jnp.float32)
    o = o * lax.rsqrt(jnp.mean(jnp.square(o), -1, keepdims=True) + LN_EPS) * subln_g.astype(jnp.float32)
    return (o * (1.0 - lam_init)).astype(v.dtype)


def prompt_attention(q, k, v, lam, subln_g, lam_init):
    B, S = q.shape[0], q.shape[1]
    nb = S // Q_BLOCK
    qb = q.reshape(B, nb, Q_BLOCK, ATT_HEADS, 2, ATT_HEAD_DIM).swapaxes(0, 1)
    k_pos = jnp.arange(S)

    def one(args):
        qblk, i = args
        q_pos = i * Q_BLOCK + jnp.arange(Q_BLOCK)
        return diff_attend(qblk, k, v, q_pos, k_pos, lam, subln_g, lam_init)

    ob = lax.map(one, (qb, jnp.arange(nb)))
    return ob.swapaxes(0, 1).reshape(B, S, ATT_HEADS, ATT_V_DIM)


def causal_dwconv(u, hist, w, b):
    xp = jnp.concatenate([hist, u], axis=1)
    y = lax.conv_general_dilated(xp, w[:, None, :], window_strides=(1,), padding='VALID',
                                 dimension_numbers=('NWC', 'WIO', 'NWC'), feature_group_count=CONV_DIM)
    return y + b, xp[:, -(CONV_WIDTH - 1):]


def peer(h, w_query, sub_keys, u_tab, v_tab):
    B, L, D = h.shape
    T = B * L
    nb = -(-T // PEER_TOK_BLOCK)
    t = jnp.pad(h.reshape(T, D), ((0, nb * PEER_TOK_BLOCK - T), (0, 0))).reshape(nb, PEER_TOK_BLOCK, D)

    def one(tb):
        q = (tb @ w_query).reshape(PEER_TOK_BLOCK, PEER_HEADS, 2, PEER_HALF)
        s = jnp.einsum('thpd,hpkd->thpk', q, sub_keys).astype(jnp.float32)
        sv, si = lax.top_k(s, PEER_TOPK)
        cand = (sv[:, :, 0, :, None] + sv[:, :, 1, None, :]).reshape(PEER_TOK_BLOCK, PEER_HEADS, PEER_TOPK * PEER_TOPK)
        cidx = (si[:, :, 0, :, None] * PEER_KEYS + si[:, :, 1, None, :]).reshape(PEER_TOK_BLOCK, PEER_HEADS, PEER_TOPK * PEER_TOPK)
        fv, fp = lax.top_k(cand, PEER_TOPK)
        e = jnp.take_along_axis(cidx, fp, axis=-1)
        g = jax.nn.softmax(fv, axis=-1)
        a = jnp.einsum('td,thkd->thk', tb, u_tab[e]).astype(jnp.float32)
        a = jax.nn.gelu(a, approximate=False) * g
        return jnp.einsum('thk,thkd->td', a.astype(tb.dtype), v_tab[e])

    out = lax.map(one, t).reshape(nb * PEER_TOK_BLOCK, D)[:T]
    return out.reshape(B, L, D)


def layer_forward(x, c, conv_hist, k_past, v_past, lw, lam_init):
    B, L, _ = x.shape
    mod = jax.nn.silu(c) @ lw['w_ada'] + lw['b_ada']
    sh1, sc1, g1, sh2, sc2, g2 = jnp.split(mod[:, None, :], 6, axis=-1)
    h = x * (1 + sc1) + sh1
    z = h @ lw['w_in'] + lw['b_in']
    conv_a, conv_b, q, k, v = jnp.split(z, [CONV_DIM, 2 * CONV_DIM, 2 * CONV_DIM + QK_DIM, 2 * CONV_DIM + 2 * QK_DIM], axis=-1)
    u = conv_a * jax.nn.sigmoid(conv_b)
    dw, conv_state = causal_dwconv(u, conv_hist, lw['w_dw'], lw['b_dw'])
    conv_out = jax.nn.silu(layer_norm(dw, lw['conv_ln_g'], lw['conv_ln_b']))
    past = 0 if k_past is None else k_past.shape[1]
    q_pos = past + jnp.arange(L)
    q = rope(q.reshape(B, L, ATT_HEADS, 2, ATT_HEAD_DIM), q_pos)
    k = rope(k.reshape(B, L, ATT_HEADS, 2, ATT_HEAD_DIM), q_pos)
    v = v.reshape(B, L, ATT_HEADS, ATT_V_DIM)
    f32 = jnp.float32
    lam = (jnp.exp(jnp.sum(lw['lam_q1'].astype(f32) * lw['lam_k1'].astype(f32)))
           - jnp.exp(jnp.sum(lw['lam_q2'].astype(f32) * lw['lam_k2'].astype(f32))) + lam_init)
    if k_past is None:
        att = prompt_attention(q, k, v, lam, lw['subln_g'], lam_init)
    else:
        k_all = jnp.concatenate([k_past, k], axis=1)
        v_all = jnp.concatenate([v_past, v], axis=1)
        att = diff_attend(q, k_all, v_all, q_pos, jnp.arange(past + L), lam, lw['subln_g'], lam_init)
    mix = jnp.concatenate([conv_out, att.reshape(B, L, ATT_DIM)], axis=-1)
    x = layer_norm(DEEPNORM_ALPHA * x + g1 * (mix @ lw['w_out'] + lw['b_out']), lw['ln1_g'], lw['ln1_b'])
    h2 = x * (1 + sc2) + sh2
    ff = peer(h2, lw['w_query'], lw['sub_keys'], lw['u_tab'], lw['v_tab'])
    x = layer_norm(DEEPNORM_ALPHA * x + g2 * ff, lw['ln2_g'], lw['ln2_b'])
    return x, conv_state, k, v


def setup_inputs(seed: int = 0) -> dict:
    key = jax.random.key(seed)
    ks = jax.random.split(key, 40)
    n = lambda i, shape, s: jax.random.normal(ks[i], shape, jnp.float32) * s
    D = D_MODEL
    return {
        'x_prompt': n(0, (BATCH, SEQ, D), 1.0),
        'x_sample': n(1, (DEC_BATCH, DEC_SEQ, D), 1.0),
        'cache_k': n(2, (DEPTH, DEC_BATCH, PAST_LEN, ATT_HEADS, 2, ATT_HEAD_DIM), 1.0),
        'cache_v': n(3, (DEPTH, DEC_BATCH, PAST_LEN, ATT_HEADS, ATT_V_DIM), 1.0),
        'cache_conv': n(4, (DEPTH, DEC_BATCH, CONV_WIDTH - 1, CONV_DIM), 0.5),
        'c_prompt': n(5, (BATCH, D), 1.0),
        'c_sample': n(6, (DEC_BATCH, D), 1.0),
        'w_ada': n(7, (DEPTH, D, 6 * D), 0.5 * D ** -0.5),
        'b_ada': n(8, (DEPTH, 6 * D), 0.01),
        'w_in': n(9, (DEPTH, D, IN_DIM), D ** -0.5),
        'b_in': n(10, (DEPTH, IN_DIM), 0.01),
        'w_dw': n(11, (DEPTH, CONV_WIDTH, CONV_DIM), CONV_WIDTH ** -0.5),
        'b_dw': n(12, (DEPTH, CONV_DIM), 0.01),
        'conv_ln_g': 1.0 + n(13, (DEPTH, CONV_DIM), 0.05),
        'conv_ln_b': n(14, (DEPTH, CONV_DIM), 0.01),
        'lam_q1': n(15, (DEPTH, ATT_HEAD_DIM), 0.1),
        'lam_k1': n(16, (DEPTH, ATT_HEAD_DIM), 0.1),
        'lam_q2': n(17, (DEPTH, ATT_HEAD_DIM), 0.1),
        'lam_k2': n(18, (DEPTH, ATT_HEAD_DIM), 0.1),
        'subln_g': 1.0 + n(19, (DEPTH, ATT_V_DIM), 0.05),
        'w_out': n(20, (DEPTH, MIX_DIM, D), DEEPNORM_BETA * MIX_DIM ** -0.5),
        'b_out': n(21, (DEPTH, D), 0.01),
        'ln1_g': 1.0 + n(22, (DEPTH, D), 0.05),
        'ln1_b': n(23, (DEPTH, D), 0.01),
        'w_query': n(24, (DEPTH, D, PEER_HEADS * PEER_QDIM), D ** -0.5),
        'sub_keys': n(25, (DEPTH, PEER_HEADS, 2, PEER_KEYS, PEER_HALF), PEER_HALF ** -0.5),
        'u_tab': n(26, (DEPTH, PEER_N, D), D ** -0.5),
        'v_tab': n(27, (DEPTH, PEER_N, D), DEEPNORM_BETA * D ** -0.5),
        'ln2_g': 1.0 + n(28, (DEPTH, D), 0.05),
        'ln2_b': n(29, (DEPTH, D), 0.01),
    }


def reference(x_prompt, x_sample, cache_k, cache_v, cache_conv, c_prompt, c_sample,
              w_ada, b_ada, w_in, b_in, w_dw, b_dw, conv_ln_g, conv_ln_b,
              lam_q1, lam_k1, lam_q2, lam_k2, subln_g, w_out, b_out, ln1_g, ln1_b,
              w_query, sub_keys, u_tab, v_tab, ln2_g, ln2_b):
    xp, xs = x_prompt, x_sample
    kp_l, vp_l, cp_l, ks_l, vs_l, cs_l = [], [], [], [], [], []
    for l in range(DEPTH):
        lam_init = 0.8 - 0.6 * math.exp(-0.3 * l)
        lw = {'w_ada': w_ada[l], 'b_ada': b_ada[l], 'w_in': w_in[l], 'b_in': b_in[l],
              'w_dw': w_dw[l], 'b_dw': b_dw[l], 'conv_ln_g': conv_ln_g[l], 'conv_ln_b': conv_ln_b[l],
              'lam_q1': lam_q1[l], 'lam_k1': lam_k1[l], 'lam_q2': lam_q2[l], 'lam_k2': lam_k2[l],
              'subln_g': subln_g[l], 'w_out': w_out[l], 'b_out': b_out[l],
              'ln1_g': ln1_g[l], 'ln1_b': ln1_b[l], 'w_query': w_query[l], 'sub_keys': sub_keys[l],
              'u_tab': u_tab[l], 'v_tab': v_tab[l], 'ln2_g': ln2_g[l], 'ln2_b': ln2_b[l]}
        hist0 = jnp.zeros((xp.shape[0], CONV_WIDTH - 1, CONV_DIM), xp.dtype)
        xp, cp, kp, vp = layer_forward(xp, c_prompt, hist0, None, None, lw, lam_init)
        xs, cs, kn, vn = layer_forward(xs, c_sample, cache_conv[l], cache_k[l], cache_v[l], lw, lam_init)
        kp_l.append(kp); vp_l.append(vp); cp_l.append(cp)
        ks_l.append(kn); vs_l.append(vn); cs_l.append(cs)
    k_prompt = jnp.stack(kp_l)
    v_prompt = jnp.stack(vp_l)
    conv_prompt = jnp.stack(cp_l)
    k_sample = jnp.stack(ks_l)
    v_sample = jnp.stack(vs_l)
    conv_sample = jnp.stack(cs_l)
    return (xp, xs, k_prompt, v_prompt, conv_prompt, k_sample, v_sample, conv_sample)
```

```python
import functools
import math

import jax
import jax.numpy as jnp
import numpy as np
from jax import lax
from jax.experimental import pallas as pl
from jax.experimental.pallas import tpu as pltpu

F32 = jnp.float32
BF16 = jnp.bfloat16

CHUNK = 64
ATT_HEADS = 4
ATT_HEAD_DIM = 64
ATT_V_DIM = 2 * ATT_HEAD_DIM
ROPE_THETA = 10000.0
PEER_TOPK = 16
LN_EPS = 1e-5

LANES = 128
SUBLANES = 8
VMEM_LIMIT_DEFAULT = 48 * 1024 * 1024
VMEM_LIMIT_TABLE = 56 * 1024 * 1024

_NT = (((1,), (1,)), ((), ()))


def _cparams(n_axes, vmem=VMEM_LIMIT_DEFAULT):
    return pltpu.CompilerParams(dimension_semantics=("arbitrary",) * n_axes, vmem_limit_bytes=vmem)


def _ln(y, g, b):
    mu = jnp.mean(y, axis=-1, keepdims=True)
    d = y - mu
    var = jnp.mean(d * d, axis=-1, keepdims=True)
    return d * lax.rsqrt(var + LN_EPS) * g + b


def _ada_kernel(c_ref, w_ref, b_ref, o_ref):
    c = c_ref[...]
    s = (c * jax.nn.sigmoid(c)).astype(BF16)
    o_ref[...] = jnp.dot(s, w_ref[...], preferred_element_type=F32) + b_ref[...]


def ada_mod(c, w_bf, b):
    rows, d = c.shape
    n = w_bf.shape[1]
    tn = min(n, 1024)
    return pl.pallas_call(
        _ada_kernel,
        grid=(n // tn,),
        in_specs=[pl.BlockSpec((rows, d), lambda j: (0, 0)),
                  pl.BlockSpec((d, tn), lambda j: (0, j)),
                  pl.BlockSpec((1, tn), lambda j: (0, j))],
        out_specs=pl.BlockSpec((rows, tn), lambda j: (0, j)),
        out_shape=jax.ShapeDtypeStruct((rows, n), F32),
        compiler_params=_cparams(1),
        name="ada_mod",
    )(c, w_bf, b.reshape(1, n))


def _inproj_kernel(x_ref, sh_ref, sc_ref, w_ref, b_ref, cos_ref, sin_ref,
                   u_ref, k_ref, v_ref, qb_ref, kb_ref, vb_ref, *, conv_dim, qk_dim):
    bb, tl, d = x_ref.shape
    h = x_ref[...] * (1.0 + sc_ref[...]) + sh_ref[...]
    z = jnp.dot(h.reshape(bb * tl, d).astype(BF16), w_ref[...], preferred_element_type=F32) + b_ref[...]
    c0, c1, c2, c3 = conv_dim, 2 * conv_dim, 2 * conv_dim + qk_dim, 2 * conv_dim + 2 * qk_dim
    u = z[:, :c0] * jax.nn.sigmoid(z[:, c0:c1])
    u_ref[...] = u.reshape(bb, tl, conv_dim)

    lane = lax.broadcasted_iota(jnp.int32, (bb * tl, qk_dim), 1)
    first_half = (lane % ATT_HEAD_DIM) < (ATT_HEAD_DIM // 2)
    cos = jnp.broadcast_to(cos_ref[...][None], (bb, tl, qk_dim)).reshape(bb * tl, qk_dim)
    sin = jnp.broadcast_to(sin_ref[...][None], (bb, tl, qk_dim)).reshape(bb * tl, qk_dim)

    def rope(t):
        swapped = jnp.where(first_half,
                            pltpu.roll(t, qk_dim - ATT_HEAD_DIM // 2, axis=1),
                            pltpu.roll(t, ATT_HEAD_DIM // 2, axis=1))
        return t * cos + swapped * sin

    q = rope(z[:, c1:c2]) * (ATT_HEAD_DIM ** -0.5)
    k = rope(z[:, c2:c3])
    v = z[:, c3:]
    k_ref[...] = k.reshape(bb, tl, qk_dim)
    v_ref[...] = v.reshape(bb, tl, v.shape[1])
    qb_ref[...] = q.astype(BF16).reshape(bb, tl, qk_dim)
    kb_ref[...] = k.astype(BF16).reshape(bb, tl, qk_dim)
    vb_ref[...] = v.astype(BF16).reshape(bb, tl, v.shape[1])


def in_proj(x, sh1, sc1, w_bf, b_in, cos, sin, bb, tl, conv_dim, qk_dim, att_dim):
    B, L, D = x.shape
    n = w_bf.shape[1]
    tok = lambda b, l: (b, l, 0)
    mod = lambda b, l: (b, 0, 0)
    const = lambda b, l: (0, 0)
    out_shapes = [jax.ShapeDtypeStruct((B, L, conv_dim), F32),
                  jax.ShapeDtypeStruct((B, L, qk_dim), F32),
                  jax.ShapeDtypeStruct((B, L, att_dim), F32),
                  jax.ShapeDtypeStruct((B, L, qk_dim), BF16),
                  jax.ShapeDtypeStruct((B, L, qk_dim), BF16),
                  jax.ShapeDtypeStruct((B, L, att_dim), BF16)]
    out_specs = [pl.BlockSpec((bb, tl, s.shape[2]), tok) for s in out_shapes]
    return pl.pallas_call(
        functools.partial(_inproj_kernel, conv_dim=conv_dim, qk_dim=qk_dim),
        grid=(B // bb, L // tl),
        in_specs=[pl.BlockSpec((bb, tl, D), tok),
                  pl.BlockSpec((bb, 1, D), mod),
                  pl.BlockSpec((bb, 1, D), mod),
                  pl.BlockSpec((D, n), const),
                  pl.BlockSpec((1, n), const),
                  pl.BlockSpec((tl, qk_dim), lambda b, l: (l, 0)),
                  pl.BlockSpec((tl, qk_dim), lambda b, l: (l, 0))],
        out_specs=out_specs,
        out_shape=out_shapes,
        compiler_params=_cparams(2),
        name="in_proj",
    )(x, sh1, sc1, w_bf, b_in.reshape(1, n), cos, sin)


HIST_PAD = 32


def _conv_kernel(u_ref, prev_ref, hist_ref, w_ref, b_ref, g_ref, beta_ref, o_ref, win_ref, *, width):
    i = pl.program_id(1)
    tl = u_ref.shape[1]

    @pl.when(i == 0)
    def _():
        win_ref[0:HIST_PAD, :] = hist_ref[0]

    @pl.when(i != 0)
    def _():
        win_ref[0:HIST_PAD, :] = prev_ref[0]

    win_ref[HIST_PAD:HIST_PAD + tl, :] = u_ref[0]
    off = HIST_PAD - (width - 1)
    acc = w_ref[0:1, :] * win_ref[off:off + tl, :]
    for j in range(1, width):
        acc = acc + w_ref[j:j + 1, :] * win_ref[off + j:off + j + tl, :]
    y = _ln(acc + b_ref[...], g_ref[...], beta_ref[...])
    o_ref[0] = (y * jax.nn.sigmoid(y)).astype(BF16)


def conv_ln(u, hist_pad, w_dw, b_dw, ln_g, ln_b, tl):
    B, L, C = u.shape
    width = w_dw.shape[0]
    steps = tl // HIST_PAD
    row = lambda b, i: (0, 0)
    return pl.pallas_call(
        functools.partial(_conv_kernel, width=width),
        grid=(B, L // tl),
        in_specs=[pl.BlockSpec((1, tl, C), lambda b, i: (b, i, 0)),
                  pl.BlockSpec((1, HIST_PAD, C), lambda b, i: (b, jnp.maximum(i * steps - 1, 0), 0)),
                  pl.BlockSpec((1, HIST_PAD, C), lambda b, i: (b, 0, 0)),
                  pl.BlockSpec((width, C), row),
                  pl.BlockSpec((1, C), row), pl.BlockSpec((1, C), row), pl.BlockSpec((1, C), row)],
        out_specs=pl.BlockSpec((1, tl, C), lambda b, i: (b, i, 0)),
        out_shape=jax.ShapeDtypeStruct((B, L, C), BF16),
        scratch_shapes=[pltpu.VMEM((HIST_PAD + tl, C), F32)],
        compiler_params=_cparams(2),
        name="conv_ln",
    )(u, u, hist_pad, w_dw, b_dw.reshape(1, C), ln_g.reshape(1, C), ln_b.reshape(1, C))


def _lambda_value(lam_ref, lam_init):
    s1 = jnp.sum(lam_ref[0:1, :] * lam_ref[1:2, :], axis=1, keepdims=True)
    s2 = jnp.sum(lam_ref[2:3, :] * lam_ref[3:4, :], axis=1, keepdims=True)
    return jnp.exp(s1) - jnp.exp(s2) + lam_init


def _split_maps(q):
    lane = lax.broadcasted_iota(jnp.int32, q.shape, 1)
    zero = jnp.zeros_like(q)
    return jnp.where(lane < ATT_HEAD_DIM, q, zero), jnp.where(lane >= ATT_HEAD_DIM, q, zero)


def _finish_heads(o0, o1, lam, g, lam_init):
    o = o0 - lam * o1
    o = o * lax.rsqrt(jnp.mean(o * o, axis=-1, keepdims=True) + LN_EPS) * g
    return o * (1.0 - lam_init)


def _attn_prompt_kernel(lam_ref, g_ref, q_ref, k_ref, v_ref, o_ref, m_sc, l_sc, acc_sc, *, lam_init):
    i = pl.program_id(2)
    tq = q_ref.shape[1]
    qs = _split_maps(q_ref[0])
    m_sc[...] = jnp.full(m_sc.shape, -jnp.inf, F32)
    l_sc[...] = jnp.zeros(l_sc.shape, F32)
    acc_sc[...] = jnp.zeros(acc_sc.shape, F32)

    qpos = lax.broadcasted_iota(jnp.int32, (tq, tq), 0)
    kpos = lax.broadcasted_iota(jnp.int32, (tq, tq), 1)
    allowed = (kpos // CHUNK) <= (qpos // CHUNK)

    def step(j, masked):
        start = pl.multiple_of(j * tq, tq)
        kb = k_ref[0, pl.ds(start, tq), :]
        vb = v_ref[0, pl.ds(start, tq), :]
        for p in range(2):
            s = lax.dot_general(qs[p], kb, _NT, preferred_element_type=F32)
            if masked:
                s = jnp.where(allowed, s, -jnp.inf)
            m_prev = m_sc[p]
            m_new = jnp.maximum(m_prev, jnp.max(s, axis=1, keepdims=True))
            alpha = jnp.exp(m_prev - m_new)
            pe = jnp.exp(s - m_new)
            l_sc[p] = alpha * l_sc[p] + jnp.sum(pe, axis=1, keepdims=True)
            acc_sc[p] = alpha * acc_sc[p] + jnp.dot(pe.astype(BF16), vb, preferred_element_type=F32)
            m_sc[p] = m_new

    def body(j, c):
        step(j, False)
        return c

    lax.fori_loop(0, i, body, 0)
    step(i, True)
    lam = _lambda_value(lam_ref, lam_init)
    o = _finish_heads(acc_sc[0] / l_sc[0], acc_sc[1] / l_sc[1], lam, g_ref[...], lam_init)
    o_ref[0] = o.astype(BF16)


def attn_prompt(qb, kb, vb, lam_rows, subln_g, lam_init, tq):
    B, L, _ = qb.shape
    return pl.pallas_call(
        functools.partial(_attn_prompt_kernel, lam_init=lam_init),
        grid=(B, ATT_HEADS, L // tq),
        in_specs=[pl.BlockSpec((4, ATT_HEAD_DIM), lambda b, h, i: (0, 0)),
                  pl.BlockSpec((1, ATT_V_DIM), lambda b, h, i: (0, 0)),
                  pl.BlockSpec((1, tq, LANES), lambda b, h, i: (b, i, h)),
                  pl.BlockSpec((1, L, LANES), lambda b, h, i: (b, 0, h)),
                  pl.BlockSpec((1, L, LANES), lambda b, h, i: (b, 0, h))],
        out_specs=pl.BlockSpec((1, tq, LANES), lambda b, h, i: (b, i, h)),
        out_shape=jax.ShapeDtypeStruct((B, L, ATT_HEADS * ATT_V_DIM), BF16),
        scratch_shapes=[pltpu.VMEM((2, tq, 1), F32), pltpu.VMEM((2, tq, 1), F32),
                        pltpu.VMEM((2, tq, ATT_V_DIM), F32)],
        compiler_params=_cparams(3),
        name="attn_prompt",
    )(lam_rows, subln_g.reshape(1, ATT_V_DIM), qb, kb, vb)


def _attn_sample_kernel(lam_ref, g_ref, q_ref, ck_ref, cv_ref, kn_ref, vn_ref, o_ref, *, lam_init, past):
    ls = q_ref.shape[1]
    qs = _split_maps(q_ref[0])
    kp = ck_ref[0].astype(BF16)
    vp = cv_ref[0].astype(BF16)
    kn = kn_ref[0]
    vn = vn_ref[0]
    qpos = past + lax.broadcasted_iota(jnp.int32, (ls, ls), 0)
    kpos = past + lax.broadcasted_iota(jnp.int32, (ls, ls), 1)
    allowed = (kpos // CHUNK) <= (qpos // CHUNK)
    outs = []
    for p in range(2):
        sp = lax.dot_general(qs[p], kp, _NT, preferred_element_type=F32)
        sn = lax.dot_general(qs[p], kn, _NT, preferred_element_type=F32)
        sn = jnp.where(allowed, sn, -jnp.inf)
        m = jnp.maximum(jnp.max(sp, axis=1, keepdims=True), jnp.max(sn, axis=1, keepdims=True))
        pp = jnp.exp(sp - m)
        pn = jnp.exp(sn - m)
        denom = jnp.sum(pp, axis=1, keepdims=True) + jnp.sum(pn, axis=1, keepdims=True)
        acc = (jnp.dot(pp.astype(BF16), vp, preferred_element_type=F32)
               + jnp.dot(pn.astype(BF16), vn, preferred_element_type=F32))
        outs.append(acc / denom)
    lam = _lambda_value(lam_ref, lam_init)
    o_ref[0] = _finish_heads(outs[0], outs[1], lam, g_ref[...], lam_init).astype(BF16)


def attn_sample(qb, cache_k, cache_v, kb, vb, lam_rows, subln_g, lam_init):
    B, Ls, _ = qb.shape
    P = cache_k.shape[1]
    blk = lambda n: pl.BlockSpec((1, n, LANES), lambda b, h: (b, 0, h))
    return pl.pallas_call(
        functools.partial(_attn_sample_kernel, lam_init=lam_init, past=P),
        grid=(B, ATT_HEADS),
        in_specs=[pl.BlockSpec((4, ATT_HEAD_DIM), lambda b, h: (0, 0)),
                  pl.BlockSpec((1, ATT_V_DIM), lambda b, h: (0, 0)),
                  blk(Ls), blk(P), blk(P), blk(Ls), blk(Ls)],
        out_specs=blk(Ls),
        out_shape=jax.ShapeDtypeStruct((B, Ls, ATT_HEADS * ATT_V_DIM), BF16),
        compiler_params=_cparams(2),
        name="attn_sample",
    )(lam_rows, subln_g.reshape(1, ATT_V_DIM), qb, cache_k, cache_v, kb, vb)


def _outproj_kernel(conv_ref, att_ref, x_ref, g1_ref, sh2_ref, sc2_ref, wc_ref, wa_ref, b_ref,
                    lg_ref, lb_ref, x1_ref, h2_ref, h2b_ref, *, alpha):
    bb, tl, d = x_ref.shape
    conv = conv_ref[...].reshape(bb * tl, conv_ref.shape[2])
    att = att_ref[...].reshape(bb * tl, att_ref.shape[2])
    mm = (jnp.dot(conv, wc_ref[...], preferred_element_type=F32)
          + jnp.dot(att, wa_ref[...], preferred_element_type=F32) + b_ref[...])
    y = alpha * x_ref[...] + g1_ref[...] * mm.reshape(bb, tl, d)
    x1 = _ln(y, lg_ref[...], lb_ref[...])
    x1_ref[...] = x1
    h2 = x1 * (1.0 + sc2_ref[...]) + sh2_ref[...]
    h2_ref[...] = h2
    h2b_ref[...] = h2.astype(BF16)


def out_proj(conv_out, att, x, g1, sh2, sc2, wc_bf, wa_bf, b_out, ln_g, ln_b, alpha, bb, tl):
    B, L, D = x.shape
    tok = lambda b, l: (b, l, 0)
    mod = lambda b, l: (b, 0, 0)
    const = lambda b, l: (0, 0)
    return pl.pallas_call(
        functools.partial(_outproj_kernel, alpha=alpha),
        grid=(B // bb, L // tl),
        in_specs=[pl.BlockSpec((bb, tl, conv_out.shape[2]), tok),
                  pl.BlockSpec((bb, tl, att.shape[2]), tok),
                  pl.BlockSpec((bb, tl, D), tok),
                  pl.BlockSpec((bb, 1, D), mod), pl.BlockSpec((bb, 1, D), mod), pl.BlockSpec((bb, 1, D), mod),
                  pl.BlockSpec(wc_bf.shape, const), pl.BlockSpec(wa_bf.shape, const),
                  pl.BlockSpec((1, D), const), pl.BlockSpec((1, D), const), pl.BlockSpec((1, D), const)],
        out_specs=[pl.BlockSpec((bb, tl, D), tok)] * 3,
        out_shape=[jax.ShapeDtypeStruct((B, L, D), F32), jax.ShapeDtypeStruct((B, L, D), F32),
                   jax.ShapeDtypeStruct((B, L, D), BF16)],
        compiler_params=_cparams(2),
        name="out_proj",
    )(conv_out, att, x, g1, sh2, sc2, wc_bf, wa_bf, b_out.reshape(1, D), ln_g.reshape(1, D), ln_b.reshape(1, D))


def _extract_top(vals, order, payload, count):
    big = jnp.int32(1 << 30)
    out_v, out_p = [], []
    for _ in range(count):
        m = jnp.max(vals, axis=0, keepdims=True)
        pos = jnp.min(jnp.where(vals == m, order, big), axis=0, keepdims=True)
        sel = order == pos
        out_v.append(m)
        out_p.append(jnp.max(jnp.where(sel, payload, -1), axis=0, keepdims=True))
        vals = jnp.where(sel, -jnp.inf, vals)
    return jnp.concatenate(out_v, axis=0), jnp.concatenate(out_p, axis=0)


def _topk_kernel(h_ref, wq_ref, keys_ref, e_ref, g_ref, q_sc, e_sc, g_sc, *, n_heads, n_keys):
    tl = h_ref.shape[0]
    K = PEER_TOPK
    qp = jnp.dot(h_ref[...], wq_ref[...], preferred_element_type=F32)
    for hp in range(2 * n_heads):
        q_sc[hp] = qp[:, hp * LANES:(hp + 1) * LANES].astype(BF16)

    key_iota = lax.broadcasted_iota(jnp.int32, (n_keys, tl), 0)
    n_cand = K + (K - 1) * SUBLANES
    row = lax.broadcasted_iota(jnp.int32, (n_cand, tl), 0)
    ci = jnp.where(row < K, 0, ((row - K) >> 3) + 1)
    cj = jnp.where(row < K, row, (row - K) & 7)
    limit = jnp.where(ci == 0, 16, jnp.where(ci == 1, 8, jnp.where(ci == 2, 5, jnp.where(
        ci == 3, 4, jnp.where(ci == 4, 3, jnp.where(ci <= 7, 2, 1))))))
    cand_ok = cj < limit
    cand_order = ci * K + cj

    def head(h, carry):
        halves = []
        for p in range(2):
            s = lax.dot_general(keys_ref[2 * h + p], q_sc[2 * h + p], _NT, preferred_element_type=F32)
            halves.append(_extract_top(s, key_iota, key_iota, K))
        (sv0, si0), (sv1, si1) = halves
        cv = [sv0[0:1] + sv1]
        ce = [si0[0:1] * n_keys + si1]
        for i in range(1, K):
            cv.append(sv0[i:i + 1] + sv1[0:SUBLANES])
            ce.append(si0[i:i + 1] * n_keys + si1[0:SUBLANES])
        cand = jnp.where(cand_ok, jnp.concatenate(cv, axis=0), -jnp.inf)
        fv, fe = _extract_top(cand, cand_order, jnp.concatenate(ce, axis=0), K)
        ex = jnp.exp(fv - fv[0:1])
        gate = ex / jnp.sum(ex, axis=0, keepdims=True)
        off = pl.multiple_of(h * K, K)
        e_sc[pl.ds(off, K), :] = fe
        g_sc[pl.ds(off, K), :] = gate
        return carry

    lax.fori_loop(0, n_heads, head, 0)
    e_ref[...] = jnp.transpose(e_sc[...])
    g_ref[...] = jnp.transpose(g_sc[...])


def peer_topk(h2b, wq_bf, keys_bf, tl):
    T, D = h2b.shape
    nq = wq_bf.shape[1]
    n_hp, n_keys, half = keys_bf.shape
    n_heads = n_hp // 2
    slots = n_heads * PEER_TOPK
    assert half == LANES and n_keys == LANES and PEER_TOPK * PEER_TOPK <= 256
    return pl.pallas_call(
        functools.partial(_topk_kernel, n_heads=n_heads, n_keys=n_keys),
        grid=(T // tl,),
        in_specs=[pl.BlockSpec((tl, D), lambda i: (i, 0)),
                  pl.BlockSpec((D, nq), lambda i: (0, 0)),
                  pl.BlockSpec((n_hp, n_keys, half), lambda i: (0, 0, 0))],
        out_specs=[pl.BlockSpec((tl, slots), lambda i: (i, 0))] * 2,
        out_shape=[jax.ShapeDtypeStruct((T, slots), jnp.int32), jax.ShapeDtypeStruct((T, slots), F32)],
        scratch_shapes=[pltpu.VMEM((n_hp, tl, LANES), BF16), pltpu.VMEM((slots, tl), jnp.int32),
                        pltpu.VMEM((slots, tl), F32)],
        compiler_params=_cparams(1),
        name="peer_topk",
    )(h2b, wq_bf, keys_bf)


PEER_TOKEN_BLOCK = 128
PEER_GROUP = 8


def _fold8(ps):
    sub = lax.broadcasted_iota(jnp.int32, (SUBLANES, LANES), 0)
    lo4, lo2, lo1 = sub < 4, (sub & 3) < 2, (sub & 1) < 1
    rev = [0, 4, 2, 6, 1, 5, 3, 7]
    leaves = [ps[r] for r in rev]
    t = []
    for a, b in zip(leaves[0::2], leaves[1::2]):
        t.append(jnp.where(lo4, a, b) + pltpu.roll(jnp.where(lo4, b, a), 4, axis=0))
    u = []
    for a, b in zip(t[0::2], t[1::2]):
        u.append(jnp.where(lo2, a + pltpu.roll(a, 6, axis=0), b + pltpu.roll(b, 2, axis=0)))
    a, b = u
    return jnp.where(lo1, a + pltpu.roll(a, 7, axis=0), b + pltpu.roll(b, 1, axis=0))


def _peer_a_kernel(e_ref, h_ref, g_ref, tab_ref, w_ref, a_sc, *, slots):
    tb = h_ref.shape[0]
    ones = jnp.ones((SUBLANES, LANES), BF16)

    def token(t, carry):
        hvec = h_ref[t]
        base = pl.multiple_of(t * slots, slots)
        folded = []
        for gi in range(slots // PEER_GROUP):
            ps = []
            for j in range(PEER_GROUP):
                idx = e_ref[base + gi * PEER_GROUP + j]
                ps.append(tab_ref[idx].astype(F32) * hvec)
            folded.append(_fold8(ps))
        m = jnp.concatenate(folded, axis=0)
        hi = m.astype(BF16)
        lo = (m - hi.astype(F32)).astype(BF16)
        r = (lax.dot_general(ones, hi, _NT, preferred_element_type=F32)
             + lax.dot_general(ones, lo, _NT, preferred_element_type=F32))
        a_sc[pl.ds(t, 1), :] = r[0:1]
        return carry

    lax.fori_loop(0, tb, token, 0)
    a = a_sc[...]
    gelu = 0.5 * a * (1.0 + lax.erf(a * (2.0 ** -0.5)))
    w_ref[...] = gelu * g_ref[...]


def peer_a(e_flat, h3, gate, tab_bf, slots):
    T = h3.shape[0]
    tb = PEER_TOKEN_BLOCK
    N = tab_bf.shape[0]
    return pl.pallas_call(
        functools.partial(_peer_a_kernel, slots=slots),
        grid=(T // tb,),
        in_specs=[pl.BlockSpec((tb * slots,), lambda i: (i,), memory_space=pltpu.SMEM),
                  pl.BlockSpec((tb, SUBLANES, LANES), lambda i: (i, 0, 0)),
                  pl.BlockSpec((tb, slots), lambda i: (i, 0)),
                  pl.BlockSpec((N, SUBLANES, LANES), lambda i: (0, 0, 0), pipeline_mode=pl.Buffered(1))],
        out_specs=pl.BlockSpec((tb, slots), lambda i: (i, 0)),
        out_shape=jax.ShapeDtypeStruct((T, slots), F32),
        scratch_shapes=[pltpu.VMEM((tb, slots), F32)],
        compiler_params=_cparams(1, VMEM_LIMIT_TABLE),
        name="peer_a",
    )(e_flat, h3, gate, tab_bf)


PEER_B_UNROLL = 32


def _peer_b_kernel(e_ref, w_ref, tab_ref, o_ref, *, slots):
    tb = o_ref.shape[0]
    per_tok = slots // PEER_B_UNROLL

    def group(i, carry):
        base = pl.multiple_of(i * PEER_B_UNROLL, PEER_B_UNROLL)
        accs = [None] * 4
        for j in range(PEER_B_UNROLL):
            term = w_ref[base + j] * tab_ref[e_ref[base + j]].astype(F32)
            accs[j % 4] = term if accs[j % 4] is None else accs[j % 4] + term
        total = (accs[0] + accs[1]) + (accs[2] + accs[3])
        t = i // per_tok
        first = (i % per_tok) == 0

        @pl.when(first)
        def _():
            o_ref[t] = total

        @pl.when(jnp.logical_not(first))
        def _():
            o_ref[t] = o_ref[t] + total

        return carry

    lax.fori_loop(0, tb * per_tok, group, 0)


def peer_b(e_flat, w_flat, tab_bf, slots):
    T = e_flat.shape[0] // slots
    tb = PEER_TOKEN_BLOCK
    N = tab_bf.shape[0]
    return pl.pallas_call(
        functools.partial(_peer_b_kernel, slots=slots),
        grid=(T // tb,),
        in_specs=[pl.BlockSpec((tb * slots,), lambda i: (i,), memory_space=pltpu.SMEM),
                  pl.BlockSpec((tb * slots,), lambda i: (i,), memory_space=pltpu.SMEM),
                  pl.BlockSpec((N, SUBLANES, LANES), lambda i: (0, 0, 0), pipeline_mode=pl.Buffered(1))],
        out_specs=pl.BlockSpec((tb, SUBLANES, LANES), lambda i: (i, 0, 0)),
        out_shape=jax.ShapeDtypeStruct((T, SUBLANES, LANES), F32),
        compiler_params=_cparams(1, VMEM_LIMIT_TABLE),
        name="peer_b",
    )(e_flat, w_flat, tab_bf)


def _final_kernel(x_ref, ff_ref, g2_ref, lg_ref, lb_ref, o_ref, *, alpha):
    y = alpha * x_ref[...] + g2_ref[...] * ff_ref[...]
    o_ref[...] = _ln(y, lg_ref[...], lb_ref[...])


def final_ln(x1, ff, g2, ln_g, ln_b, alpha, bb, tl):
    B, L, D = x1.shape
    tok = lambda b, l: (b, l, 0)
    const = lambda b, l: (0, 0)
    return pl.pallas_call(
        functools.partial(_final_kernel, alpha=alpha),
        grid=(B // bb, L // tl),
        in_specs=[pl.BlockSpec((bb, tl, D), tok), pl.BlockSpec((bb, tl, D), tok),
                  pl.BlockSpec((bb, 1, D), lambda b, l: (b, 0, 0)),
                  pl.BlockSpec((1, D), const), pl.BlockSpec((1, D), const)],
        out_specs=pl.BlockSpec((bb, tl, D), tok),
        out_shape=jax.ShapeDtypeStruct((B, L, D), F32),
        compiler_params=_cparams(2),
        name="final_ln",
    )(x1, ff, g2, ln_g.reshape(1, D), ln_b.reshape(1, D))


def _rope_tables(past, length, n_groups):
    half = ATT_HEAD_DIM // 2
    inv = 1.0 / (ROPE_THETA ** (jnp.arange(half, dtype=F32) / half))
    ang = (past + jnp.arange(length)).astype(F32)[:, None] * inv[None, :]
    cos = jnp.cos(ang)
    sin = jnp.sin(ang)
    cos_t = jnp.tile(jnp.concatenate([cos, cos], axis=-1), (1, n_groups))
    sin_t = jnp.tile(jnp.concatenate([-sin, sin], axis=-1), (1, n_groups))
    return cos_t, sin_t


def _token_tiles(B, L, target):
    if L >= target:
        return 1, target
    bb = max(1, min(B, target // L))
    while B % bb:
        bb -= 1
    return bb, L


def _layer(x, mod, conv_hist, k_past, v_past, wts, lam_init, alpha):
    B, L, D = x.shape
    conv_dim = wts["w_dw"].shape[1]
    qk_dim = ATT_HEADS * 2 * ATT_HEAD_DIM
    att_dim = ATT_HEADS * ATT_V_DIM
    width = wts["w_dw"].shape[0]
    assert L >= width - 1 and width - 1 <= HIST_PAD
    sh1, sc1, g1, sh2, sc2, g2 = [m[:, None, :] for m in jnp.split(mod, 6, axis=-1)]
    past = 0 if k_past is None else k_past.shape[1]
    cos, sin = _rope_tables(past, L, qk_dim // ATT_HEAD_DIM)
    bb, tl = _token_tiles(B, L, 512)

    u, k, v, qb, kb, vb = in_proj(x, sh1, sc1, wts["w_in"], wts["b_in"], cos, sin, bb, tl,
                                  conv_dim, qk_dim, att_dim)
    hist_pad = jnp.pad(conv_hist, ((0, 0), (HIST_PAD - (width - 1), 0), (0, 0)))
    conv_out = conv_ln(u, hist_pad, wts["w_dw"], wts["b_dw"], wts["conv_ln_g"], wts["conv_ln_b"], min(L, 256))
    conv_state = u[:, L - (width - 1):, :]

    if k_past is None:
        att = attn_prompt(qb, kb, vb, wts["lam_rows"], wts["subln_g"], lam_init, min(L, 256))
    else:
        att = attn_sample(qb, k_past.reshape(B, past, qk_dim), v_past.reshape(B, past, att_dim), kb, vb,
                          wts["lam_rows"], wts["subln_g"], lam_init)

    x1, h2, h2b = out_proj(conv_out, att, x, g1, sh2, sc2, wts["w_out_conv"], wts["w_out_att"], wts["b_out"],
                           wts["ln1_g"], wts["ln1_b"], alpha, bb, tl)

    T = B * L
    slots = (wts["sub_keys"].shape[0] // 2) * PEER_TOPK
    e, gate = peer_topk(h2b.reshape(T, D), wts["w_query"], wts["sub_keys"], min(T, 256))
    e_flat = e.reshape(T * slots)
    w = peer_a(e_flat, h2.reshape(T, SUBLANES, LANES), gate, wts["u_tab"], slots)
    ff = peer_b(e_flat, w.reshape(T * slots), wts["v_tab"], slots)
    out = final_ln(x1, ff.reshape(B, L, D), g2, wts["ln2_g"], wts["ln2_b"], alpha, bb, tl)
    return out, conv_state, k, v


def kernel(x_prompt, x_sample, cache_k, cache_v, cache_conv, c_prompt, c_sample, w_ada, b_ada, w_in, b_in, w_dw, b_dw, conv_ln_g, conv_ln_b, lam_q1, lam_k1, lam_q2, lam_k2, subln_g, w_out, b_out, ln1_g, ln1_b, w_query, sub_keys, u_tab, v_tab, ln2_g, ln2_b):
    depth = w_ada.shape[0]
    D = x_prompt.shape[-1]
    assert D == SUBLANES * LANES
    alpha = (2 * depth) ** 0.25
    Bp, Bs = c_prompt.shape[0], c_sample.shape[0]
    xp, xs = x_prompt, x_sample
    outs = [[] for _ in range(6)]
    for l in range(depth):
        lam_init = 0.8 - 0.6 * math.exp(-0.3 * l)
        conv_dim = w_dw.shape[2]
        n_exp = u_tab.shape[1]
        wts = {
            "w_in": w_in[l].astype(BF16), "b_in": b_in[l], "w_dw": w_dw[l], "b_dw": b_dw[l],
            "conv_ln_g": conv_ln_g[l], "conv_ln_b": conv_ln_b[l],
            "lam_rows": jnp.stack([lam_q1[l], lam_k1[l], lam_q2[l], lam_k2[l]]).astype(F32),
            "subln_g": subln_g[l],
            "w_out_conv": w_out[l, :conv_dim].astype(BF16), "w_out_att": w_out[l, conv_dim:].astype(BF16),
            "b_out": b_out[l], "ln1_g": ln1_g[l], "ln1_b": ln1_b[l],
            "w_query": w_query[l].astype(BF16),
            "sub_keys": sub_keys[l].astype(BF16).reshape(-1, sub_keys.shape[3], sub_keys.shape[4]),
            "u_tab": u_tab[l].astype(BF16).reshape(n_exp, SUBLANES, LANES),
            "v_tab": v_tab[l].astype(BF16).reshape(n_exp, SUBLANES, LANES),
            "ln2_g": ln2_g[l], "ln2_b": ln2_b[l],
        }
        c_all = jnp.concatenate([c_prompt, c_sample], axis=0)
        pad = (-c_all.shape[0]) % 16
        c_all = jnp.pad(c_all, ((0, pad), (0, 0)))
        mod = ada_mod(c_all, w_ada[l].astype(BF16), b_ada[l])
        hist0 = jnp.zeros((Bp, w_dw.shape[1] - 1, conv_dim), xp.dtype)
        xp, cp, kp, vp = _layer(xp, mod[:Bp], hist0, None, None, wts, lam_init, alpha)
        xs, cs, kn, vn = _layer(xs, mod[Bp:Bp + Bs], cache_conv[l], cache_k[l], cache_v[l], wts, lam_init, alpha)
        for lst, val in zip(outs, (kp, vp, cp, kn, vn, cs)):
            lst.append(val)
    Lp, Ls = x_prompt.shape[1], x_sample.shape[1]
    k_prompt = jnp.stack(outs[0]).reshape(depth, Bp, Lp, ATT_HEADS, 2, ATT_HEAD_DIM)
    v_prompt = jnp.stack(outs[1]).reshape(depth, Bp, Lp, ATT_HEADS, ATT_V_DIM)
    conv_prompt = jnp.stack(outs[2])
    k_sample = jnp.stack(outs[3]).reshape(depth, Bs, Ls, ATT_HEADS, 2, ATT_HEAD_DIM)
    v_sample = jnp.stack(outs[4]).reshape(depth, Bs, Ls, ATT_HEADS, ATT_V_DIM)
    conv_sample = jnp.stack(outs[5])
    return (xp, xs, k_prompt, v_prompt, conv_prompt, k_sample, v_sample, conv_sample)
```

```python
import functools
import math

import jax
import jax.numpy as jnp
import numpy as np
from jax import lax
from jax.experimental import pallas as pl
from jax.experimental.pallas import tpu as pltpu

F32 = jnp.float32
BF16 = jnp.bfloat16

CHUNK = 64
ATT_HEADS = 4
ATT_HEAD_DIM = 64
ATT_V_DIM = 2 * ATT_HEAD_DIM
ROPE_THETA = 10000.0
PEER_TOPK = 16
LN_EPS = 1e-5

LANES = 128
SUBLANES = 8
VMEM_LIMIT_DEFAULT = 48 * 1024 * 1024
VMEM_LIMIT_TABLE = 56 * 1024 * 1024

_NT = (((1,), (1,)), ((), ()))


def _cparams(n_axes, vmem=VMEM_LIMIT_DEFAULT):
    return pltpu.CompilerParams(dimension_semantics=("arbitrary",) * n_axes, vmem_limit_bytes=vmem)


def _ln(y, g, b):
    mu = jnp.mean(y, axis=-1, keepdims=True)
    d = y - mu
    var = jnp.mean(d * d, axis=-1, keepdims=True)
    return d * lax.rsqrt(var + LN_EPS) * g + b


def _ada_kernel(c_ref, w_ref, b_ref, o_ref):
    c = c_ref[...]
    s = (c * jax.nn.sigmoid(c)).astype(BF16)
    o_ref[...] = jnp.dot(s, w_ref[...], preferred_element_type=F32) + b_ref[...]


def ada_mod(c, w_bf, b):
    rows, d = c.shape
    n = w_bf.shape[1]
    tn = min(n, 1024)
    return pl.pallas_call(
        _ada_kernel,
        grid=(n // tn,),
        in_specs=[pl.BlockSpec((rows, d), lambda j: (0, 0)),
                  pl.BlockSpec((d, tn), lambda j: (0, j)),
                  pl.BlockSpec((1, tn), lambda j: (0, j))],
        out_specs=pl.BlockSpec((rows, tn), lambda j: (0, j)),
        out_shape=jax.ShapeDtypeStruct((rows, n), F32),
        compiler_params=_cparams(1),
        name="ada_mod",
    )(c, w_bf, b.reshape(1, n))


def _inproj_kernel(x_ref, sh_ref, sc_ref, w_ref, b_ref, cos_ref, sin_ref,
                   u_ref, k_ref, v_ref, qb_ref, kb_ref, vb_ref, *, conv_dim, qk_dim):
    bb, tl, d = x_ref.shape
    h = x_ref[...] * (1.0 + sc_ref[...]) + sh_ref[...]
    z = jnp.dot(h.reshape(bb * tl, d).astype(BF16), w_ref[...], preferred_element_type=F32) + b_ref[...]
    c0, c1, c2, c3 = conv_dim, 2 * conv_dim, 2 * conv_dim + qk_dim, 2 * conv_dim + 2 * qk_dim
    u = z[:, :c0] * jax.nn.sigmoid(z[:, c0:c1])
    u_ref[...] = u.reshape(bb, tl, conv_dim)

    lane = lax.broadcasted_iota(jnp.int32, (bb * tl, qk_dim), 1)
    first_half = (lane % ATT_HEAD_DIM) < (ATT_HEAD_DIM // 2)
    cos = jnp.broadcast_to(cos_ref[...][None], (bb, tl, qk_dim)).reshape(bb * tl, qk_dim)
    sin = jnp.broadcast_to(sin_ref[...][None], (bb, tl, qk_dim)).reshape(bb * tl, qk_dim)

    def rope(t):
        swapped = jnp.where(first_half,
                            pltpu.roll(t, qk_dim - ATT_HEAD_DIM // 2, axis=1),
                            pltpu.roll(t, ATT_HEAD_DIM // 2, axis=1))
        return t * cos + swapped * sin

    q = rope(z[:, c1:c2]) * (ATT_HEAD_DIM ** -0.5)
    k = rope(z[:, c2:c3])
    v = z[:, c3:]
    k_ref[...] = k.reshape(bb, tl, qk_dim)
    v_ref[...] = v.reshape(bb, tl, v.shape[1])
    qb_ref[...] = q.astype(BF16).reshape(bb, tl, qk_dim)
    kb_ref[...] = k.astype(BF16).reshape(bb, tl, qk_dim)
    vb_ref[...] = v.astype(BF16).reshape(bb, tl, v.shape[1])


def in_proj(x, sh1, sc1, w_bf, b_in, cos, sin, bb, tl, conv_dim, qk_dim, att_dim):
    B, L, D = x.shape
    n = w_bf.shape[1]
    tok = lambda b, l: (b, l, 0)
    mod = lambda b, l: (b, 0, 0)
    const = lambda b, l: (0, 0)
    out_shapes = [jax.ShapeDtypeStruct((B, L, conv_dim), F32),
                  jax.ShapeDtypeStruct((B, L, qk_dim), F32),
                  jax.ShapeDtypeStruct((B, L, att_dim), F32),
                  jax.ShapeDtypeStruct((B, L, qk_dim), BF16),
                  jax.ShapeDtypeStruct((B, L, qk_dim), BF16),
                  jax.ShapeDtypeStruct((B, L, att_dim), BF16)]
    out_specs = [pl.BlockSpec((bb, tl, s.shape[2]), tok) for s in out_shapes]
    return pl.pallas_call(
        functools.partial(_inproj_kernel, conv_dim=conv_dim, qk_dim=qk_dim),
        grid=(B // bb, L // tl),
        in_specs=[pl.BlockSpec((bb, tl, D), tok),
                  pl.BlockSpec((bb, 1, D), mod),
                  pl.BlockSpec((bb, 1, D), mod),
                  pl.BlockSpec((D, n), const),
                  pl.BlockSpec((1, n), const),
                  pl.BlockSpec((tl, qk_dim), lambda b, l: (l, 0)),
                  pl.BlockSpec((tl, qk_dim), lambda b, l: (l, 0))],
        out_specs=out_specs,
        out_shape=out_shapes,
        compiler_params=_cparams(2),
        name="in_proj",
    )(x, sh1, sc1, w_bf, b_in.reshape(1, n), cos, sin)


HIST_PAD = 32


def _conv_kernel(u_ref, prev_ref, hist_ref, w_ref, b_ref, g_ref, beta_ref, o_ref, win_ref, *, width):
    i = pl.program_id(1)
    tl = u_ref.shape[1]

    @pl.when(i == 0)
    def _():
        win_ref[0:HIST_PAD, :] = hist_ref[0]

    @pl.when(i != 0)
    def _():
        win_ref[0:HIST_PAD, :] = prev_ref[0]

    win_ref[HIST_PAD:HIST_PAD + tl, :] = u_ref[0]
    off = HIST_PAD - (width - 1)
    acc = w_ref[0:1, :] * win_ref[off:off + tl, :]
    for j in range(1, width):
        acc = acc + w_ref[j:j + 1, :] * win_ref[off + j:off + j + tl, :]
    y = _ln(acc + b_ref[...], g_ref[...], beta_ref[...])
    o_ref[0] = (y * jax.nn.sigmoid(y)).astype(BF16)


def conv_ln(u, hist_pad, w_dw, b_dw, ln_g, ln_b, tl):
    B, L, C = u.shape
    width = w_dw.shape[0]
    steps = tl // HIST_PAD
    row = lambda b, i: (0, 0)
    return pl.pallas_call(
        functools.partial(_conv_kernel, width=width),
        grid=(B, L // tl),
        in_specs=[pl.BlockSpec((1, tl, C), lambda b, i: (b, i, 0)),
                  pl.BlockSpec((1, HIST_PAD, C), lambda b, i: (b, jnp.maximum(i * steps - 1, 0), 0)),
                  pl.BlockSpec((1, HIST_PAD, C), lambda b, i: (b, 0, 0)),
                  pl.BlockSpec((width, C), row),
                  pl.BlockSpec((1, C), row), pl.BlockSpec((1, C), row), pl.BlockSpec((1, C), row)],
        out_specs=pl.BlockSpec((1, tl, C), lambda b, i: (b, i, 0)),
        out_shape=jax.ShapeDtypeStruct((B, L, C), BF16),
        scratch_shapes=[pltpu.VMEM((HIST_PAD + tl, C), F32)],
        compiler_params=_cparams(2),
        name="conv_ln",
    )(u, u, hist_pad, w_dw, b_dw.reshape(1, C), ln_g.reshape(1, C), ln_b.reshape(1, C))


def _lambda_value(lam_ref, lam_init):
    s1 = jnp.sum(lam_ref[0:1, :] * lam_ref[1:2, :], axis=1, keepdims=True)
    s2 = jnp.sum(lam_ref[2:3, :] * lam_ref[3:4, :], axis=1, keepdims=True)
    return jnp.exp(s1) - jnp.exp(s2) + lam_init


def _split_maps(q):
    lane = lax.broadcasted_iota(jnp.int32, q.shape, 1)
    zero = jnp.zeros_like(q)
    return jnp.where(lane < ATT_HEAD_DIM, q, zero), jnp.where(lane >= ATT_HEAD_DIM, q, zero)


def _finish_heads(o0, o1, lam, g, lam_init):
    o = o0 - lam * o1
    o = o * lax.rsqrt(jnp.mean(o * o, axis=-1, keepdims=True) + LN_EPS) * g
    return o * (1.0 - lam_init)


def _attn_prompt_kernel(lam_ref, g_ref, q_ref, k_ref, v_ref, o_ref, m_sc, l_sc, acc_sc, *, lam_init):
    i = pl.program_id(2)
    tq = q_ref.shape[1]
    qs = _split_maps(q_ref[0])
    m_sc[...] = jnp.full(m_sc.shape, -jnp.inf, F32)
    l_sc[...] = jnp.zeros(l_sc.shape, F32)
    acc_sc[...] = jnp.zeros(acc_sc.shape, F32)

    qpos = lax.broadcasted_iota(jnp.int32, (tq, tq), 0)
    kpos = lax.broadcasted_iota(jnp.int32, (tq, tq), 1)
    allowed = (kpos // CHUNK) <= (qpos // CHUNK)

    def step(j, masked):
        start = pl.multiple_of(j * tq, tq)
        kb = k_ref[0, pl.ds(start, tq), :]
        vb = v_ref[0, pl.ds(start, tq), :]
        for p in range(2):
            s = lax.dot_general(qs[p], kb, _NT, preferred_element_type=F32)
            if masked:
                s = jnp.where(allowed, s, -jnp.inf)
            m_prev = m_sc[p]
            m_new = jnp.maximum(m_prev, jnp.max(s, axis=1, keepdims=True))
            alpha = jnp.exp(m_prev - m_new)
            pe = jnp.exp(s - m_new)
            l_sc[p] = alpha * l_sc[p] + jnp.sum(pe, axis=1, keepdims=True)
            acc_sc[p] = alpha * acc_sc[p] + jnp.dot(pe.astype(BF16), vb, preferred_element_type=F32)
            m_sc[p] = m_new

    def body(j, c):
        step(j, False)
        return c

    lax.fori_loop(0, i, body, 0)
    step(i, True)
    lam = _lambda_value(lam_ref, lam_init)
    o = _finish_heads(acc_sc[0] / l_sc[0], acc_sc[1] / l_sc[1], lam, g_ref[...], lam_init)
    o_ref[0] = o.astype(BF16)


def attn_prompt(qb, kb, vb, lam_rows, subln_g, lam_init, tq):
    B, L, _ = qb.shape
    return pl.pallas_call(
        functools.partial(_attn_prompt_kernel, lam_init=lam_init),
        grid=(B, ATT_HEADS, L // tq),
        in_specs=[pl.BlockSpec((4, ATT_HEAD_DIM), lambda b, h, i: (0, 0)),
                  pl.BlockSpec((1, ATT_V_DIM), lambda b, h, i: (0, 0)),
                  pl.BlockSpec((1, tq, LANES), lambda b, h, i: (b, i, h)),
                  pl.BlockSpec((1, L, LANES), lambda b, h, i: (b, 0, h)),
                  pl.BlockSpec((1, L, LANES), lambda b, h, i: (b, 0, h))],
        out_specs=pl.BlockSpec((1, tq, LANES), lambda b, h, i: (b, i, h)),
        out_shape=jax.ShapeDtypeStruct((B, L, ATT_HEADS * ATT_V_DIM), BF16),
        scratch_shapes=[pltpu.VMEM((2, tq, 1), F32), pltpu.VMEM((2, tq, 1), F32),
                        pltpu.VMEM((2, tq, ATT_V_DIM), F32)],
        compiler_params=_cparams(3),
        name="attn_prompt",
    )(lam_rows, subln_g.reshape(1, ATT_V_DIM), qb, kb, vb)


def _attn_sample_kernel(lam_ref, g_ref, q_ref, ck_ref, cv_ref, kn_ref, vn_ref, o_ref, *, lam_init, past):
    ls = q_ref.shape[1]
    qs = _split_maps(q_ref[0])
    kp = ck_ref[0].astype(BF16)
    vp = cv_ref[0].astype(BF16)
    kn = kn_ref[0]
    vn = vn_ref[0]
    qpos = past + lax.broadcasted_iota(jnp.int32, (ls, ls), 0)
    kpos = past + lax.broadcasted_iota(jnp.int32, (ls, ls), 1)
    allowed = (kpos // CHUNK) <= (qpos // CHUNK)
    outs = []
    for p in range(2):
        sp = lax.dot_general(qs[p], kp, _NT, preferred_element_type=F32)
        sn = lax.dot_general(qs[p], kn, _NT, preferred_element_type=F32)
        sn = jnp.where(allowed, sn, -jnp.inf)
        m = jnp.maximum(jnp.max(sp, axis=1, keepdims=True), jnp.max(sn, axis=1, keepdims=True))
        pp = jnp.exp(sp - m)
        pn = jnp.exp(sn - m)
        denom = jnp.sum(pp, axis=1, keepdims=True) + jnp.sum(pn, axis=1, keepdims=True)
        acc = (jnp.dot(pp.astype(BF16), vp, preferred_element_type=F32)
               + jnp.dot(pn.astype(BF16), vn, preferred_element_type=F32))
        outs.append(acc / denom)
    lam = _lambda_value(lam_ref, lam_init)
    o_ref[0] = _finish_heads(outs[0], outs[1], lam, g_ref[...], lam_init).astype(BF16)


def attn_sample(qb, cache_k, cache_v, kb, vb, lam_rows, subln_g, lam_init):
    B, Ls, _ = qb.shape
    P = cache_k.shape[1]
    blk = lambda n: pl.BlockSpec((1, n, LANES), lambda b, h: (b, 0, h))
    return pl.pallas_call(
        functools.partial(_attn_sample_kernel, lam_init=lam_init, past=P),
        grid=(B, ATT_HEADS),
        in_specs=[pl.BlockSpec((4, ATT_HEAD_DIM), lambda b, h: (0, 0)),
                  pl.BlockSpec((1, ATT_V_DIM), lambda b, h: (0, 0)),
                  blk(Ls), blk(P), blk(P), blk(Ls), blk(Ls)],
        out_specs=blk(Ls),
        out_shape=jax.ShapeDtypeStruct((B, Ls, ATT_HEADS * ATT_V_DIM), BF16),
        compiler_params=_cparams(2),
        name="attn_sample",
    )(lam_rows, subln_g.reshape(1, ATT_V_DIM), qb, cache_k, cache_v, kb, vb)


def _outproj_kernel(conv_ref, att_ref, x_ref, g1_ref, sh2_ref, sc2_ref, wc_ref, wa_ref, b_ref,
                    lg_ref, lb_ref, x1_ref, h2_ref, h2b_ref, *, alpha):
    bb, tl, d = x_ref.shape
    conv = conv_ref[...].reshape(bb * tl, conv_ref.shape[2])
    att = att_ref[...].reshape(bb * tl, att_ref.shape[2])
    mm = (jnp.dot(conv, wc_ref[...], preferred_element_type=F32)
          + jnp.dot(att, wa_ref[...], preferred_element_type=F32) + b_ref[...])
    y = alpha * x_ref[...] + g1_ref[...] * mm.reshape(bb, tl, d)
    x1 = _ln(y, lg_ref[...], lb_ref[...])
    x1_ref[...] = x1
    h2 = x1 * (1.0 + sc2_ref[...]) + sh2_ref[...]
    h2_ref[...] = h2
    h2b_ref[...] = h2.astype(BF16)


def out_proj(conv_out, att, x, g1, sh2, sc2, wc_bf, wa_bf, b_out, ln_g, ln_b, alpha, bb, tl):
    B, L, D = x.shape
    tok = lambda b, l: (b, l, 0)
    mod = lambda b, l: (b, 0, 0)
    const = lambda b, l: (0, 0)
    return pl.pallas_call(
        functools.partial(_outproj_kernel, alpha=alpha),
        grid=(B // bb, L // tl),
        in_specs=[pl.BlockSpec((bb, tl, conv_out.shape[2]), tok),
                  pl.BlockSpec((bb, tl, att.shape[2]), tok),
                  pl.BlockSpec((bb, tl, D), tok),
                  pl.BlockSpec((bb, 1, D), mod), pl.BlockSpec((bb, 1, D), mod), pl.BlockSpec((bb, 1, D), mod),
                  pl.BlockSpec(wc_bf.shape, const), pl.BlockSpec(wa_bf.shape, const),
                  pl.BlockSpec((1, D), const), pl.BlockSpec((1, D), const), pl.BlockSpec((1, D), const)],
        out_specs=[pl.BlockSpec((bb, tl, D), tok)] * 3,
        out_shape=[jax.ShapeDtypeStruct((B, L, D), F32), jax.ShapeDtypeStruct((B, L, D), F32),
                   jax.ShapeDtypeStruct((B, L, D), BF16)],
        compiler_params=_cparams(2),
        name="out_proj",
    )(conv_out, att, x, g1, sh2, sc2, wc_bf, wa_bf, b_out.reshape(1, D), ln_g.reshape(1, D), ln_b.reshape(1, D))


def _extract_top(vals, order, payload, count):
    big = jnp.int32(1 << 30)
    out_v, out_p = [], []
    for _ in range(count):
        m = jnp.max(vals, axis=0, keepdims=True)
        pos = jnp.min(jnp.where(vals == m, order, big), axis=0, keepdims=True)
        sel = order == pos
        out_v.append(m)
        out_p.append(jnp.max(jnp.where(sel, payload, -1), axis=0, keepdims=True))
        vals = jnp.where(sel, -jnp.inf, vals)
    return jnp.concatenate(out_v, axis=0), jnp.concatenate(out_p, axis=0)


def _topk_kernel(h_ref, wq_ref, keys_ref, e_ref, g_ref, q_sc, e_sc, g_sc, *, n_heads, n_keys):
    tl = h_ref.shape[0]
    K = PEER_TOPK
    qp = jnp.dot(h_ref[...], wq_ref[...], preferred_element_type=F32)
    for hp in range(2 * n_heads):
        q_sc[hp] = qp[:, hp * LANES:(hp + 1) * LANES].astype(BF16)

    key_iota = lax.broadcasted_iota(jnp.int32, (n_keys, tl), 0)
    n_cand = K + (K - 1) * SUBLANES
    row = lax.broadcasted_iota(jnp.int32, (n_cand, tl), 0)
    ci = jnp.where(row < K, 0, ((row - K) >> 3) + 1)
    cj = jnp.where(row < K, row, (row - K) & 7)
    limit = jnp.where(ci == 0, 16, jnp.where(ci == 1, 8, jnp.where(ci == 2, 5, jnp.where(
        ci == 3, 4, jnp.where(ci == 4, 3, jnp.where(ci <= 7, 2, 1))))))
    cand_ok = cj < limit
    cand_order = ci * K + cj

    def head(h, carry):
        halves = []
        for p in range(2):
            s = lax.dot_general(keys_ref[2 * h + p], q_sc[2 * h + p], _NT, preferred_element_type=F32)
            halves.append(_extract_top(s, key_iota, key_iota, K))
        (sv0, si0), (sv1, si1) = halves
        cv = [sv0[0:1] + sv1]
        ce = [si0[0:1] * n_keys + si1]
        for i in range(1, K):
            cv.append(sv0[i:i + 1] + sv1[0:SUBLANES])
            ce.append(si0[i:i + 1] * n_keys + si1[0:SUBLANES])
        cand = jnp.where(cand_ok, jnp.concatenate(cv, axis=0), -jnp.inf)
        fv, fe = _extract_top(cand, cand_order, jnp.concatenate(ce, axis=0), K)
        ex = jnp.exp(fv - fv[0:1])
        gate = ex / jnp.sum(ex, axis=0, keepdims=True)
        off = pl.multiple_of(h * K, K)
        e_sc[pl.ds(off, K), :] = fe * TABLE_ROWS_PER_EXPERT
        g_sc[pl.ds(off, K), :] = gate
        return carry

    lax.fori_loop(0, n_heads, head, 0)
    e_ref[...] = jnp.transpose(e_sc[...])
    g_ref[...] = jnp.transpose(g_sc[...])


def peer_topk(h2b, wq_bf, keys_bf, tl):
    T, D = h2b.shape
    nq = wq_bf.shape[1]
    n_hp, n_keys, half = keys_bf.shape
    n_heads = n_hp // 2
    slots = n_heads * PEER_TOPK
    assert half == LANES and n_keys == LANES and PEER_TOPK * PEER_TOPK <= 256
    return pl.pallas_call(
        functools.partial(_topk_kernel, n_heads=n_heads, n_keys=n_keys),
        grid=(T // tl,),
        in_specs=[pl.BlockSpec((tl, D), lambda i: (i, 0)),
                  pl.BlockSpec((D, nq), lambda i: (0, 0)),
                  pl.BlockSpec((n_hp, n_keys, half), lambda i: (0, 0, 0))],
        out_specs=[pl.BlockSpec((tl, slots), lambda i: (i, 0))] * 2,
        out_shape=[jax.ShapeDtypeStruct((T, slots), jnp.int32), jax.ShapeDtypeStruct((T, slots), F32)],
        scratch_shapes=[pltpu.VMEM((n_hp, tl, LANES), BF16), pltpu.VMEM((slots, tl), jnp.int32),
                        pltpu.VMEM((slots, tl), F32)],
        compiler_params=_cparams(1),
        name="peer_topk",
    )(h2b, wq_bf, keys_bf)


TABLE_ROWS_PER_EXPERT = 4
PEER_TOKEN_BLOCK = 128
PEER_TOKEN_UNROLL = 4


def pack_table(tab):
    n, d = tab.shape
    t = tab.astype(BF16).reshape(n, d // (2 * LANES), 2, LANES)
    t = jnp.swapaxes(t, -1, -2)
    return lax.bitcast_convert_type(t, jnp.uint32).reshape(n * TABLE_ROWS_PER_EXPERT, LANES)


def _gather_rows(e_ref, tab_ref, base, slots):
    rows = []
    for j in range(slots):
        start = pl.multiple_of(e_ref[base + j], TABLE_ROWS_PER_EXPERT)
        rows.append(pltpu.bitcast(tab_ref[pl.ds(start, TABLE_ROWS_PER_EXPERT), :], BF16))
    return jnp.concatenate(rows, axis=0)


def _split_bf16(x):
    hi = x.astype(BF16)
    return hi, (x - hi.astype(F32)).astype(BF16)


def _chunk_mask(slots):
    lane = lax.broadcasted_iota(jnp.int32, (SUBLANES, slots * SUBLANES), 1)
    sub = lax.broadcasted_iota(jnp.int32, (SUBLANES, slots * SUBLANES), 0)
    return (lane & (SUBLANES - 1)) == sub


def _slot_spread(slots):
    r = lax.broadcasted_iota(jnp.int32, (slots, slots * SUBLANES), 0)
    c = lax.broadcasted_iota(jnp.int32, (slots, slots * SUBLANES), 1)
    return jnp.where((c >> 3) == r, 1.0, 0.0).astype(BF16)


def _peer_a_kernel(e_ref, h_ref, g_ref, tab_ref, w_ref, hh_sc, z_sc, *, slots):
    tb = h_ref.shape[0]
    hi, lo = _split_bf16(h_ref[...])
    hh_sc[:, 0:SUBLANES, :] = hi
    hh_sc[:, SUBLANES:2 * SUBLANES, :] = lo
    mask = _chunk_mask(slots)

    def tokens(i, carry):
        for u in range(PEER_TOKEN_UNROLL):
            t = i * PEER_TOKEN_UNROLL + u
            g = _gather_rows(e_ref, tab_ref, pl.multiple_of(t * slots, slots), slots)
            z = lax.dot_general(hh_sc[t], g, _NT, preferred_element_type=F32)
            z8 = z[0:SUBLANES] + z[SUBLANES:2 * SUBLANES]
            z_sc[pl.ds(pl.multiple_of(t * SUBLANES, SUBLANES), SUBLANES), :] = jnp.where(mask, z8, 0.0)
        return carry

    lax.fori_loop(0, tb // PEER_TOKEN_UNROLL, tokens, 0)
    zhi, zlo = _split_bf16(z_sc[...])
    spread = _slot_spread(slots)
    a8 = (lax.dot_general(zhi, spread, _NT, preferred_element_type=F32)
          + lax.dot_general(zlo, spread, _NT, preferred_element_type=F32))
    a = jnp.sum(a8.reshape(tb, SUBLANES, slots), axis=1)
    gelu = 0.5 * a * (1.0 + lax.erf(a * (2.0 ** -0.5)))
    w_ref[...] = gelu * g_ref[...]


def peer_a(e_flat, h3, gate, tab_u32, slots):
    T = h3.shape[0]
    tb = PEER_TOKEN_BLOCK
    return pl.pallas_call(
        functools.partial(_peer_a_kernel, slots=slots),
        grid=(T // tb,),
        in_specs=[pl.BlockSpec((tb * slots,), lambda i: (i,), memory_space=pltpu.SMEM),
                  pl.BlockSpec((tb, SUBLANES, LANES), lambda i: (i, 0, 0)),
                  pl.BlockSpec((tb, slots), lambda i: (i, 0)),
                  pl.BlockSpec(tab_u32.shape, lambda i: (0, 0), pipeline_mode=pl.Buffered(1))],
        out_specs=pl.BlockSpec((tb, slots), lambda i: (i, 0)),
        out_shape=jax.ShapeDtypeStruct((T, slots), F32),
        scratch_shapes=[pltpu.VMEM((tb, 2 * SUBLANES, LANES), BF16),
                        pltpu.VMEM((tb * SUBLANES, slots * SUBLANES), F32)],
        compiler_params=_cparams(1, VMEM_LIMIT_TABLE),
        name="peer_a",
    )(e_flat, h3, gate, tab_u32)


def _peer_b_kernel(e_ref, w_ref, tab_ref, o_ref, rep_sc, *, slots):
    tb = o_ref.shape[0]
    spread = _slot_spread(slots)
    whi, wlo = _split_bf16(w_ref[...])
    rep_sc[0] = jnp.dot(whi, spread, preferred_element_type=F32)
    rep_sc[1] = jnp.dot(wlo, spread, preferred_element_type=F32)
    mask = _chunk_mask(slots)
    wide = (SUBLANES, slots * SUBLANES)

    def tokens(i, carry):
        for u in range(PEER_TOKEN_UNROLL):
            t = i * PEER_TOKEN_UNROLL + u
            g = _gather_rows(e_ref, tab_ref, pl.multiple_of(t * slots, slots), slots)
            parts = [jnp.where(mask, jnp.broadcast_to(rep_sc[p, pl.ds(t, 1), :], wide), 0.0).astype(BF16)
                     for p in range(2)]
            r = jnp.dot(jnp.concatenate(parts, axis=0), g, preferred_element_type=F32)
            o_ref[t] = r[0:SUBLANES] + r[SUBLANES:2 * SUBLANES]
        return carry

    lax.fori_loop(0, tb // PEER_TOKEN_UNROLL, tokens, 0)


def peer_b(e_flat, w, tab_u32, slots):
    T = w.shape[0]
    tb = PEER_TOKEN_BLOCK
    return pl.pallas_call(
        functools.partial(_peer_b_kernel, slots=slots),
        grid=(T // tb,),
        in_specs=[pl.BlockSpec((tb * slots,), lambda i: (i,), memory_space=pltpu.SMEM),
                  pl.BlockSpec((tb, slots), lambda i: (i, 0)),
                  pl.BlockSpec(tab_u32.shape, lambda i: (0, 0), pipeline_mode=pl.Buffered(1))],
        out_specs=pl.BlockSpec((tb, SUBLANES, LANES), lambda i: (i, 0, 0)),
        out_shape=jax.ShapeDtypeStruct((T, SUBLANES, LANES), F32),
        scratch_shapes=[pltpu.VMEM((2, tb, slots * SUBLANES), F32)],
        compiler_params=_cparams(1, VMEM_LIMIT_TABLE),
        name="peer_b",
    )(e_flat, w, tab_u32)


def _final_kernel(x_ref, ff_ref, g2_ref, lg_ref, lb_ref, o_ref, *, alpha):
    y = alpha * x_ref[...] + g2_ref[...] * ff_ref[...]
    o_ref[...] = _ln(y, lg_ref[...], lb_ref[...])


def final_ln(x1, ff, g2, ln_g, ln_b, alpha, bb, tl):
    B, L, D = x1.shape
    tok = lambda b, l: (b, l, 0)
    const = lambda b, l: (0, 0)
    return pl.pallas_call(
        functools.partial(_final_kernel, alpha=alpha),
        grid=(B // bb, L // tl),
        in_specs=[pl.BlockSpec((bb, tl, D), tok), pl.BlockSpec((bb, tl, D), tok),
                  pl.BlockSpec((bb, 1, D), lambda b, l: (b, 0, 0)),
                  pl.BlockSpec((1, D), const), pl.BlockSpec((1, D), const)],
        out_specs=pl.BlockSpec((bb, tl, D), tok),
        out_shape=jax.ShapeDtypeStruct((B, L, D), F32),
        compiler_params=_cparams(2),
        name="final_ln",
    )(x1, ff, g2, ln_g.reshape(1, D), ln_b.reshape(1, D))


def _rope_tables(past, length, n_groups):
    half = ATT_HEAD_DIM // 2
    inv = 1.0 / (ROPE_THETA ** (jnp.arange(half, dtype=F32) / half))
    ang = (past + jnp.arange(length)).astype(F32)[:, None] * inv[None, :]
    cos = jnp.cos(ang)
    sin = jnp.sin(ang)
    cos_t = jnp.tile(jnp.concatenate([cos, cos], axis=-1), (1, n_groups))
    sin_t = jnp.tile(jnp.concatenate([-sin, sin], axis=-1), (1, n_groups))
    return cos_t, sin_t


def _token_tiles(B, L, target):
    if L >= target:
        return 1, target
    bb = max(1, min(B, target // L))
    while B % bb:
        bb -= 1
    return bb, L


def _layer(x, mod, conv_hist, k_past, v_past, wts, lam_init, alpha):
    B, L, D = x.shape
    conv_dim = wts["w_dw"].shape[1]
    qk_dim = ATT_HEADS * 2 * ATT_HEAD_DIM
    att_dim = ATT_HEADS * ATT_V_DIM
    width = wts["w_dw"].shape[0]
    assert L >= width - 1 and width - 1 <= HIST_PAD
    sh1, sc1, g1, sh2, sc2, g2 = [m[:, None, :] for m in jnp.split(mod, 6, axis=-1)]
    past = 0 if k_past is None else k_past.shape[1]
    cos, sin = _rope_tables(past, L, qk_dim // ATT_HEAD_DIM)
    bb, tl = _token_tiles(B, L, 512)

    u, k, v, qb, kb, vb = in_proj(x, sh1, sc1, wts["w_in"], wts["b_in"], cos, sin, bb, tl,
                                  conv_dim, qk_dim, att_dim)
    hist_pad = jnp.pad(conv_hist, ((0, 0), (HIST_PAD - (width - 1), 0), (0, 0)))
    conv_out = conv_ln(u, hist_pad, wts["w_dw"], wts["b_dw"], wts["conv_ln_g"], wts["conv_ln_b"], min(L, 256))
    conv_state = u[:, L - (width - 1):, :]

    if k_past is None:
        att = attn_prompt(qb, kb, vb, wts["lam_rows"], wts["subln_g"], lam_init, min(L, 256))
    else:
        att = attn_sample(qb, k_past.reshape(B, past, qk_dim), v_past.reshape(B, past, att_dim), kb, vb,
                          wts["lam_rows"], wts["subln_g"], lam_init)

    x1, h2, h2b = out_proj(conv_out, att, x, g1, sh2, sc2, wts["w_out_conv"], wts["w_out_att"], wts["b_out"],
                           wts["ln1_g"], wts["ln1_b"], alpha, bb, tl)

    T = B * L
    slots = (wts["sub_keys"].shape[0] // 2) * PEER_TOPK
    e, gate = peer_topk(h2b.reshape(T, D), wts["w_query"], wts["sub_keys"], min(T, 256))
    e_flat = e.reshape(T * slots)
    w = peer_a(e_flat, h2.reshape(T, SUBLANES, LANES), gate, wts["u_tab"], slots)
    ff = peer_b(e_flat, w, wts["v_tab"], slots)
    out = final_ln(x1, ff.reshape(B, L, D), g2, wts["ln2_g"], wts["ln2_b"], alpha, bb, tl)
    return out, conv_state, k, v


def kernel(x_prompt, x_sample, cache_k, cache_v, cache_conv, c_prompt, c_sample, w_ada, b_ada, w_in, b_in, w_dw, b_dw, conv_ln_g, conv_ln_b, lam_q1, lam_k1, lam_q2, lam_k2, subln_g, w_out, b_out, ln1_g, ln1_b, w_query, sub_keys, u_tab, v_tab, ln2_g, ln2_b):
    depth = w_ada.shape[0]
    D = x_prompt.shape[-1]
    assert D == SUBLANES * LANES
    alpha = (2 * depth) ** 0.25
    Bp, Bs = c_prompt.shape[0], c_sample.shape[0]
    xp, xs = x_prompt, x_sample
    outs = [[] for _ in range(6)]
    for l in range(depth):
        lam_init = 0.8 - 0.6 * math.exp(-0.3 * l)
        conv_dim = w_dw.shape[2]
        n_exp = u_tab.shape[1]
        wts = {
            "w_in": w_in[l].astype(BF16), "b_in": b_in[l], "w_dw": w_dw[l], "b_dw": b_dw[l],
            "conv_ln_g": conv_ln_g[l], "conv_ln_b": conv_ln_b[l],
            "lam_rows": jnp.stack([lam_q1[l], lam_k1[l], lam_q2[l], lam_k2[l]]).astype(F32),
            "subln_g": subln_g[l],
            "w_out_conv": w_out[l, :conv_dim].astype(BF16), "w_out_att": w_out[l, conv_dim:].astype(BF16),
            "b_out": b_out[l], "ln1_g": ln1_g[l], "ln1_b": ln1_b[l],
            "w_query": w_query[l].astype(BF16),
            "sub_keys": sub_keys[l].astype(BF16).reshape(-1, sub_keys.shape[3], sub_keys.shape[4]),
            "u_tab": pack_table(u_tab[l]), "v_tab": pack_table(v_tab[l]),
            "ln2_g": ln2_g[l], "ln2_b": ln2_b[l],
        }
        c_all = jnp.concatenate([c_prompt, c_sample], axis=0)
        pad = (-c_all.shape[0]) % 16
        c_all = jnp.pad(c_all, ((0, pad), (0, 0)))
        mod = ada_mod(c_all, w_ada[l].astype(BF16), b_ada[l])
        hist0 = jnp.zeros((Bp, w_dw.shape[1] - 1, conv_dim), xp.dtype)
        xp, cp, kp, vp = _layer(xp, mod[:Bp], hist0, None, None, wts, lam_init, alpha)
        xs, cs, kn, vn = _layer(xs, mod[Bp:Bp + Bs], cache_conv[l], cache_k[l], cache_v[l], wts, lam_init, alpha)
        for lst, val in zip(outs, (kp, vp, cp, kn, vn, cs)):
            lst.append(val)
    Lp, Ls = x_prompt.shape[1], x_sample.shape[1]
    k_prompt = jnp.stack(outs[0]).reshape(depth, Bp, Lp, ATT_HEADS, 2, ATT_HEAD_DIM)
    v_prompt = jnp.stack(outs[1]).reshape(depth, Bp, Lp, ATT_HEADS, ATT_V_DIM)
    conv_prompt = jnp.stack(outs[2])
    k_sample = jnp.stack(outs[3]).reshape(depth, Bs, Ls, ATT_HEADS, 2, ATT_HEAD_DIM)
    v_sample = jnp.stack(outs[4]).reshape(depth, Bs, Ls, ATT_HEADS, ATT_V_DIM)
    conv_sample = jnp.stack(outs[5])
    return (xp, xs, k_prompt, v_prompt, conv_prompt, k_sample, v_sample, conv_sample)
```

```python
import functools
import math

import jax
import jax.numpy as jnp
import numpy as np
from jax import lax
from jax.experimental import pallas as pl
from jax.experimental.pallas import tpu as pltpu

F32 = jnp.float32
BF16 = jnp.bfloat16

CHUNK = 64
ATT_HEADS = 4
ATT_HEAD_DIM = 64
ATT_V_DIM = 2 * ATT_HEAD_DIM
ROPE_THETA = 10000.0
PEER_TOPK = 16
LN_EPS = 1e-5

LANES = 128
SUBLANES = 8
VMEM_LIMIT_DEFAULT = 48 * 1024 * 1024
VMEM_LIMIT_TABLE = 56 * 1024 * 1024

_NT = (((1,), (1,)), ((), ()))


def _cparams(n_axes, vmem=VMEM_LIMIT_DEFAULT):
    return pltpu.CompilerParams(dimension_semantics=("arbitrary",) * n_axes, vmem_limit_bytes=vmem)


def _ln(y, g, b):
    mu = jnp.mean(y, axis=-1, keepdims=True)
    d = y - mu
    var = jnp.mean(d * d, axis=-1, keepdims=True)
    return d * lax.rsqrt(var + LN_EPS) * g + b


def _ada_kernel(c_ref, w_ref, b_ref, o_ref):
    c = c_ref[...]
    s = (c * jax.nn.sigmoid(c)).astype(BF16)
    o_ref[...] = jnp.dot(s, w_ref[...], preferred_element_type=F32) + b_ref[...]


def ada_mod(c, w_bf, b):
    rows, d = c.shape
    n = w_bf.shape[1]
    tn = min(n, 1024)
    return pl.pallas_call(
        _ada_kernel,
        grid=(n // tn,),
        in_specs=[pl.BlockSpec((rows, d), lambda j: (0, 0)),
                  pl.BlockSpec((d, tn), lambda j: (0, j)),
                  pl.BlockSpec((1, tn), lambda j: (0, j))],
        out_specs=pl.BlockSpec((rows, tn), lambda j: (0, j)),
        out_shape=jax.ShapeDtypeStruct((rows, n), F32),
        compiler_params=_cparams(1),
        name="ada_mod",
    )(c, w_bf, b.reshape(1, n))


def _inproj_kernel(x_ref, sh_ref, sc_ref, w_ref, b_ref, cos_ref, sin_ref,
                   wvt_ref, bvt_ref, u_ref, k_ref, v_ref, qb_ref, kb_ref, vb_ref, *, conv_dim, qk_dim,
                   transposed_v):
    bb, tl, d = x_ref.shape
    h = x_ref[...] * (1.0 + sc_ref[...]) + sh_ref[...]
    hb = h.reshape(bb * tl, d).astype(BF16)
    z = jnp.dot(hb, w_ref[...], preferred_element_type=F32) + b_ref[...]
    c0, c1, c2, c3 = conv_dim, 2 * conv_dim, 2 * conv_dim + qk_dim, 2 * conv_dim + 2 * qk_dim
    u = z[:, :c0] * jax.nn.sigmoid(z[:, c0:c1])
    u_ref[...] = u.reshape(bb, tl, conv_dim)

    lane = lax.broadcasted_iota(jnp.int32, (bb * tl, qk_dim), 1)
    first_half = (lane % ATT_HEAD_DIM) < (ATT_HEAD_DIM // 2)
    cos = jnp.broadcast_to(cos_ref[...][None], (bb, tl, qk_dim)).reshape(bb * tl, qk_dim)
    sin = jnp.broadcast_to(sin_ref[...][None], (bb, tl, qk_dim)).reshape(bb * tl, qk_dim)

    def rope(t):
        swapped = jnp.where(first_half,
                            pltpu.roll(t, qk_dim - ATT_HEAD_DIM // 2, axis=1),
                            pltpu.roll(t, ATT_HEAD_DIM // 2, axis=1))
        return t * cos + swapped * sin

    q = rope(z[:, c1:c2]) * (ATT_HEAD_DIM ** -0.5)
    k = rope(z[:, c2:c3])
    v = z[:, c3:]
    k_ref[...] = k.reshape(bb, tl, qk_dim)
    v_ref[...] = v.reshape(bb, tl, v.shape[1])
    qb_ref[...] = q.astype(BF16).reshape(bb, tl, qk_dim)
    kb_ref[...] = k.astype(BF16).reshape(bb, tl, qk_dim)
    if transposed_v:
        vt = lax.dot_general(wvt_ref[...], hb, _NT, preferred_element_type=F32) + bvt_ref[...]
        vb_ref[0] = vt.astype(BF16)
    else:
        vb_ref[...] = v.astype(BF16).reshape(bb, tl, v.shape[1])


def in_proj(x, sh1, sc1, w_bf, b_in, cos, sin, bb, tl, conv_dim, qk_dim, att_dim, transposed_v):
    B, L, D = x.shape
    n = w_bf.shape[1]
    assert not transposed_v or bb == 1
    tok = lambda b, l: (b, l, 0)
    mod = lambda b, l: (b, 0, 0)
    const = lambda b, l: (0, 0)
    out_shapes = [jax.ShapeDtypeStruct((B, L, conv_dim), F32),
                  jax.ShapeDtypeStruct((B, L, qk_dim), F32),
                  jax.ShapeDtypeStruct((B, L, att_dim), F32),
                  jax.ShapeDtypeStruct((B, L, qk_dim), BF16),
                  jax.ShapeDtypeStruct((B, L, qk_dim), BF16),
                  jax.ShapeDtypeStruct((B, L, att_dim), BF16)]
    out_specs = [pl.BlockSpec((bb, tl, s.shape[2]), tok) for s in out_shapes]
    if transposed_v:
        out_shapes[5] = jax.ShapeDtypeStruct((B, att_dim, L), BF16)
        out_specs[5] = pl.BlockSpec((1, att_dim, tl), lambda b, l: (b, 0, l))
    wvt = jnp.transpose(w_bf[:, n - att_dim:])
    bvt = b_in[n - att_dim:].reshape(att_dim, 1)
    return pl.pallas_call(
        functools.partial(_inproj_kernel, conv_dim=conv_dim, qk_dim=qk_dim, transposed_v=transposed_v),
        grid=(B // bb, L // tl),
        in_specs=[pl.BlockSpec((bb, tl, D), tok),
                  pl.BlockSpec((bb, 1, D), mod),
                  pl.BlockSpec((bb, 1, D), mod),
                  pl.BlockSpec((D, n), const),
                  pl.BlockSpec((1, n), const),
                  pl.BlockSpec((tl, qk_dim), lambda b, l: (l, 0)),
                  pl.BlockSpec((tl, qk_dim), lambda b, l: (l, 0)),
                  pl.BlockSpec((att_dim, D), const),
                  pl.BlockSpec((att_dim, 1), const)],
        out_specs=out_specs,
        out_shape=out_shapes,
        compiler_params=_cparams(2),
        name="in_proj",
    )(x, sh1, sc1, w_bf, b_in.reshape(1, n), cos, sin, wvt, bvt)


HIST_PAD = 32


def _conv_kernel(u_ref, prev_ref, hist_ref, w_ref, b_ref, g_ref, beta_ref, o_ref, win_ref, *, width):
    i = pl.program_id(1)
    tl = u_ref.shape[1]

    @pl.when(i == 0)
    def _():
        win_ref[0:HIST_PAD, :] = hist_ref[0]

    @pl.when(i != 0)
    def _():
        win_ref[0:HIST_PAD, :] = prev_ref[0]

    win_ref[HIST_PAD:HIST_PAD + tl, :] = u_ref[0]
    off = HIST_PAD - (width - 1)
    acc = w_ref[0:1, :] * win_ref[off:off + tl, :]
    for j in range(1, width):
        acc = acc + w_ref[j:j + 1, :] * win_ref[off + j:off + j + tl, :]
    y = _ln(acc + b_ref[...], g_ref[...], beta_ref[...])
    o_ref[0] = (y * jax.nn.sigmoid(y)).astype(BF16)


def conv_ln(u, hist_pad, w_dw, b_dw, ln_g, ln_b, tl):
    B, L, C = u.shape
    width = w_dw.shape[0]
    steps = tl // HIST_PAD
    row = lambda b, i: (0, 0)
    return pl.pallas_call(
        functools.partial(_conv_kernel, width=width),
        grid=(B, L // tl),
        in_specs=[pl.BlockSpec((1, tl, C), lambda b, i: (b, i, 0)),
                  pl.BlockSpec((1, HIST_PAD, C), lambda b, i: (b, jnp.maximum(i * steps - 1, 0), 0)),
                  pl.BlockSpec((1, HIST_PAD, C), lambda b, i: (b, 0, 0)),
                  pl.BlockSpec((width, C), row),
                  pl.BlockSpec((1, C), row), pl.BlockSpec((1, C), row), pl.BlockSpec((1, C), row)],
        out_specs=pl.BlockSpec((1, tl, C), lambda b, i: (b, i, 0)),
        out_shape=jax.ShapeDtypeStruct((B, L, C), BF16),
        scratch_shapes=[pltpu.VMEM((HIST_PAD + tl, C), F32)],
        compiler_params=_cparams(2),
        name="conv_ln",
    )(u, u, hist_pad, w_dw, b_dw.reshape(1, C), ln_g.reshape(1, C), ln_b.reshape(1, C))


def _lambda_value(lam_ref, lam_init):
    s1 = jnp.sum(lam_ref[0:1, :] * lam_ref[1:2, :], axis=1, keepdims=True)
    s2 = jnp.sum(lam_ref[2:3, :] * lam_ref[3:4, :], axis=1, keepdims=True)
    return jnp.exp(s1) - jnp.exp(s2) + lam_init


def _split_maps(q):
    lane = lax.broadcasted_iota(jnp.int32, q.shape, 1)
    zero = jnp.zeros_like(q)
    return jnp.where(lane < ATT_HEAD_DIM, q, zero), jnp.where(lane >= ATT_HEAD_DIM, q, zero)


def _finish_heads(o0, o1, lam, g, lam_init):
    o = o0 - lam * o1
    o = o * lax.rsqrt(jnp.mean(o * o, axis=-1, keepdims=True) + LN_EPS) * g
    return o * (1.0 - lam_init)


def _attn_prompt_kernel(lam_ref, g_ref, q_ref, k_ref, vt_ref, o_ref, m_sc, l_sc, acc_sc, *, lam_init):
    i = pl.program_id(2)
    tq = q_ref.shape[1]
    q2 = jnp.concatenate(_split_maps(q_ref[0]), axis=0)
    m_sc[...] = jnp.full(m_sc.shape, -jnp.inf, F32)
    l_sc[...] = jnp.zeros(l_sc.shape, F32)
    acc_sc[...] = jnp.zeros(acc_sc.shape, F32)

    kpos = lax.broadcasted_iota(jnp.int32, (tq, 2 * tq), 0)
    qpos = lax.broadcasted_iota(jnp.int32, (tq, 2 * tq), 1) % tq
    allowed = (kpos // CHUNK) <= (qpos // CHUNK)

    def scores(j):
        kb = k_ref[0, pl.ds(pl.multiple_of(j * tq, tq), tq), :]
        return lax.dot_general(kb, q2, _NT, preferred_element_type=F32)

    def update(s, j):
        vt = vt_ref[0, :, pl.ds(pl.multiple_of(j * tq, tq), tq)]
        m_prev = m_sc[...]
        m_new = jnp.maximum(m_prev, jnp.max(s, axis=0, keepdims=True))
        alpha = jnp.exp(m_prev - m_new)
        pe = jnp.exp(s - m_new)
        l_sc[...] = alpha * l_sc[...] + jnp.sum(pe, axis=0, keepdims=True)
        acc_sc[...] = alpha * acc_sc[...] + jnp.dot(vt, pe.astype(BF16), preferred_element_type=F32)
        m_sc[...] = m_new

    def pair(jj, c):
        s_a, s_b = scores(2 * jj), scores(2 * jj + 1)
        update(s_a, 2 * jj)
        update(s_b, 2 * jj + 1)
        return c

    lax.fori_loop(0, i // 2, pair, 0)

    @pl.when(i % 2 == 1)
    def _():
        s_a, s_b = scores(i - 1), jnp.where(allowed, scores(i), -jnp.inf)
        update(s_a, i - 1)
        update(s_b, i)

    @pl.when(i % 2 == 0)
    def _():
        update(jnp.where(allowed, scores(i), -jnp.inf), i)

    lam = _lambda_value(lam_ref, lam_init)
    o = acc_sc[...] / l_sc[...]
    o = o[:, :tq] - lam * o[:, tq:]
    o = o * lax.rsqrt(jnp.mean(o * o, axis=0, keepdims=True) + LN_EPS) * g_ref[...]
    o_ref[0] = jnp.transpose(o * (1.0 - lam_init)).astype(BF16)


def attn_prompt(qb, kb, vt, lam_rows, subln_g, lam_init, tq):
    B, L, _ = qb.shape
    return pl.pallas_call(
        functools.partial(_attn_prompt_kernel, lam_init=lam_init),
        grid=(B, ATT_HEADS, L // tq),
        in_specs=[pl.BlockSpec((4, ATT_HEAD_DIM), lambda b, h, i: (0, 0)),
                  pl.BlockSpec((ATT_V_DIM, 1), lambda b, h, i: (0, 0)),
                  pl.BlockSpec((1, tq, LANES), lambda b, h, i: (b, i, h)),
                  pl.BlockSpec((1, L, LANES), lambda b, h, i: (b, 0, h)),
                  pl.BlockSpec((1, ATT_V_DIM, L), lambda b, h, i: (b, h, 0))],
        out_specs=pl.BlockSpec((1, tq, LANES), lambda b, h, i: (b, i, h)),
        out_shape=jax.ShapeDtypeStruct((B, L, ATT_HEADS * ATT_V_DIM), BF16),
        scratch_shapes=[pltpu.VMEM((1, 2 * tq), F32), pltpu.VMEM((1, 2 * tq), F32),
                        pltpu.VMEM((ATT_V_DIM, 2 * tq), F32)],
        compiler_params=_cparams(3),
        name="attn_prompt",
    )(lam_rows, subln_g.reshape(ATT_V_DIM, 1), qb, kb, vt)


def _attn_sample_kernel(lam_ref, g_ref, q_ref, ck_ref, cv_ref, kn_ref, vn_ref, o_ref, *, lam_init, past):
    ls = q_ref.shape[1]
    qs = _split_maps(q_ref[0])
    kp = ck_ref[0].astype(BF16)
    vp = cv_ref[0].astype(BF16)
    kn = kn_ref[0]
    vn = vn_ref[0]
    qpos = past + lax.broadcasted_iota(jnp.int32, (ls, ls), 0)
    kpos = past + lax.broadcasted_iota(jnp.int32, (ls, ls), 1)
    allowed = (kpos // CHUNK) <= (qpos // CHUNK)
    outs = []
    for p in range(2):
        sp = lax.dot_general(qs[p], kp, _NT, preferred_element_type=F32)
        sn = lax.dot_general(qs[p], kn, _NT, preferred_element_type=F32)
        sn = jnp.where(allowed, sn, -jnp.inf)
        m = jnp.maximum(jnp.max(sp, axis=1, keepdims=True), jnp.max(sn, axis=1, keepdims=True))
        pp = jnp.exp(sp - m)
        pn = jnp.exp(sn - m)
        denom = jnp.sum(pp, axis=1, keepdims=True) + jnp.sum(pn, axis=1, keepdims=True)
        acc = (jnp.dot(pp.astype(BF16), vp, preferred_element_type=F32)
               + jnp.dot(pn.astype(BF16), vn, preferred_element_type=F32))
        outs.append(acc / denom)
    lam = _lambda_value(lam_ref, lam_init)
    o_ref[0] = _finish_heads(outs[0], outs[1], lam, g_ref[...], lam_init).astype(BF16)


def attn_sample(qb, cache_k, cache_v, kb, vb, lam_rows, subln_g, lam_init):
    B, Ls, _ = qb.shape
    P = cache_k.shape[1]
    blk = lambda n: pl.BlockSpec((1, n, LANES), lambda b, h: (b, 0, h))
    return pl.pallas_call(
        functools.partial(_attn_sample_kernel, lam_init=lam_init, past=P),
        grid=(B, ATT_HEADS),
        in_specs=[pl.BlockSpec((4, ATT_HEAD_DIM), lambda b, h: (0, 0)),
                  pl.BlockSpec((1, ATT_V_DIM), lambda b, h: (0, 0)),
                  blk(Ls), blk(P), blk(P), blk(Ls), blk(Ls)],
        out_specs=blk(Ls),
        out_shape=jax.ShapeDtypeStruct((B, Ls, ATT_HEADS * ATT_V_DIM), BF16),
        compiler_params=_cparams(2),
        name="attn_sample",
    )(lam_rows, subln_g.reshape(1, ATT_V_DIM), qb, cache_k, cache_v, kb, vb)


def _outproj_kernel(conv_ref, att_ref, x_ref, g1_ref, sh2_ref, sc2_ref, wc_ref, wa_ref, b_ref,
                    lg_ref, lb_ref, x1_ref, h2_ref, h2b_ref, *, alpha):
    bb, tl, d = x_ref.shape
    conv = conv_ref[...].reshape(bb * tl, conv_ref.shape[2])
    att = att_ref[...].reshape(bb * tl, att_ref.shape[2])
    mm = (jnp.dot(conv, wc_ref[...], preferred_element_type=F32)
          + jnp.dot(att, wa_ref[...], preferred_element_type=F32) + b_ref[...])
    y = alpha * x_ref[...] + g1_ref[...] * mm.reshape(bb, tl, d)
    x1 = _ln(y, lg_ref[...], lb_ref[...])
    x1_ref[...] = x1
    h2 = x1 * (1.0 + sc2_ref[...]) + sh2_ref[...]
    h2_ref[...] = h2
    h2b_ref[...] = h2.astype(BF16)


def out_proj(conv_out, att, x, g1, sh2, sc2, wc_bf, wa_bf, b_out, ln_g, ln_b, alpha, bb, tl):
    B, L, D = x.shape
    tok = lambda b, l: (b, l, 0)
    mod = lambda b, l: (b, 0, 0)
    const = lambda b, l: (0, 0)
    return pl.pallas_call(
        functools.partial(_outproj_kernel, alpha=alpha),
        grid=(B // bb, L // tl),
        in_specs=[pl.BlockSpec((bb, tl, conv_out.shape[2]), tok),
                  pl.BlockSpec((bb, tl, att.shape[2]), tok),
                  pl.BlockSpec((bb, tl, D), tok),
                  pl.BlockSpec((bb, 1, D), mod), pl.BlockSpec((bb, 1, D), mod), pl.BlockSpec((bb, 1, D), mod),
                  pl.BlockSpec(wc_bf.shape, const), pl.BlockSpec(wa_bf.shape, const),
                  pl.BlockSpec((1, D), const), pl.BlockSpec((1, D), const), pl.BlockSpec((1, D), const)],
        out_specs=[pl.BlockSpec((bb, tl, D), tok)] * 3,
        out_shape=[jax.ShapeDtypeStruct((B, L, D), F32), jax.ShapeDtypeStruct((B, L, D), F32),
                   jax.ShapeDtypeStruct((B, L, D), BF16)],
        compiler_params=_cparams(2),
        name="out_proj",
    )(conv_out, att, x, g1, sh2, sc2, wc_bf, wa_bf, b_out.reshape(1, D), ln_g.reshape(1, D), ln_b.reshape(1, D))


TOPK_STREAMS = 2


def _extract_top(vals, order, count, payload=None):
    big = jnp.float32(1e9)
    out_v, out_p = [], []
    for _ in range(count):
        m = jnp.max(vals, axis=0, keepdims=True)
        pos = jnp.min(jnp.where(vals == m, order, big), axis=0, keepdims=True)
        sel = order == pos
        out_v.append(m)
        out_p.append(pos if payload is None else jnp.max(jnp.where(sel, payload, -1.0), axis=0, keepdims=True))
        vals = jnp.where(sel, -jnp.inf, vals)
    return jnp.concatenate(out_v, axis=0), jnp.concatenate(out_p, axis=0)


def _topk_kernel(h_ref, wq_ref, keys_ref, e_ref, g_ref, q_sc, e_sc, g_sc, *, n_heads, n_keys):
    tl = h_ref.shape[0]
    n_chunks = tl // LANES
    K = PEER_TOPK
    qp = jnp.dot(h_ref[...], wq_ref[...], preferred_element_type=F32)
    for hp in range(2 * n_heads):
        q_sc[hp] = qp[:, hp * LANES:(hp + 1) * LANES].astype(BF16)

    key_iota = lax.broadcasted_iota(jnp.int32, (n_keys, LANES), 0).astype(F32)
    n_cand = K + (K - 1) * SUBLANES
    row = lax.broadcasted_iota(jnp.int32, (n_cand, LANES), 0)
    ci = jnp.where(row < K, 0, ((row - K) >> 3) + 1)
    cj = jnp.where(row < K, row, (row - K) & 7)
    limit = jnp.where(ci == 0, 16, jnp.where(ci == 1, 8, jnp.where(ci == 2, 5, jnp.where(
        ci == 3, 4, jnp.where(ci == 4, 3, jnp.where(ci <= 7, 2, 1))))))
    cand_ok = cj < limit
    cand_order = (ci * K + cj).astype(F32)

    def head_chunk(h, col):
        halves = []
        for p in range(2):
            qc = q_sc[2 * h + p, pl.ds(col, LANES), :]
            s = lax.dot_general(keys_ref[2 * h + p], qc, _NT, preferred_element_type=F32)
            halves.append(_extract_top(s, key_iota, K))
        (sv0, si0), (sv1, si1) = halves
        cv = [sv0[0:1] + sv1]
        ce = [si0[0:1] * n_keys + si1]
        for r in range(1, K):
            cv.append(sv0[r:r + 1] + sv1[0:SUBLANES])
            ce.append(si0[r:r + 1] * n_keys + si1[0:SUBLANES])
        cand = jnp.where(cand_ok, jnp.concatenate(cv, axis=0), -jnp.inf)
        fv, fe = _extract_top(cand, cand_order, K, payload=jnp.concatenate(ce, axis=0))
        ex = jnp.exp(fv - fv[0:1])
        gate = ex / jnp.sum(ex, axis=0, keepdims=True)
        off = pl.multiple_of(h * K, K)
        e_sc[pl.ds(off, K), pl.ds(col, LANES)] = fe
        g_sc[pl.ds(off, K), pl.ds(col, LANES)] = gate

    streams = TOPK_STREAMS if n_chunks % TOPK_STREAMS == 0 else 1
    groups = n_chunks // streams

    def step(i, carry):
        h = i // groups
        for c in range(streams):
            head_chunk(h, pl.multiple_of(((i % groups) * streams + c) * LANES, LANES))
        return carry

    lax.fori_loop(0, n_heads * groups, step, 0)
    e_ref[...] = jnp.transpose(e_sc[...]).astype(jnp.int32) * TABLE_ROWS_PER_EXPERT
    g_ref[...] = jnp.transpose(g_sc[...])


def peer_topk(h2b, wq_bf, keys_bf, tl):
    T, D = h2b.shape
    nq = wq_bf.shape[1]
    n_hp, n_keys, half = keys_bf.shape
    n_heads = n_hp // 2
    slots = n_heads * PEER_TOPK
    assert half == LANES and n_keys == LANES and PEER_TOPK * PEER_TOPK <= 256
    return pl.pallas_call(
        functools.partial(_topk_kernel, n_heads=n_heads, n_keys=n_keys),
        grid=(T // tl,),
        in_specs=[pl.BlockSpec((tl, D), lambda i: (i, 0)),
                  pl.BlockSpec((D, nq), lambda i: (0, 0)),
                  pl.BlockSpec((n_hp, n_keys, half), lambda i: (0, 0, 0))],
        out_specs=[pl.BlockSpec((tl, slots), lambda i: (i, 0))] * 2,
        out_shape=[jax.ShapeDtypeStruct((T, slots), jnp.int32), jax.ShapeDtypeStruct((T, slots), F32)],
        scratch_shapes=[pltpu.VMEM((n_hp, tl, LANES), BF16), pltpu.VMEM((slots, tl), F32),
                        pltpu.VMEM((slots, tl), F32)],
        compiler_params=_cparams(1),
        name="peer_topk",
    )(h2b, wq_bf, keys_bf)


TABLE_ROWS_PER_EXPERT = 4
PEER_TOKEN_BLOCK = 128
PEER_TOKEN_UNROLL = 8


def pack_table(tab):
    n, d = tab.shape
    t = tab.astype(BF16).reshape(n, d // (2 * LANES), 2, LANES)
    t = jnp.swapaxes(t, -1, -2)
    return lax.bitcast_convert_type(t, jnp.uint32).reshape(n * TABLE_ROWS_PER_EXPERT, LANES)


def _gather_rows(e_ref, tab_ref, base, slots):
    rows = []
    for j in range(slots):
        start = pl.multiple_of(e_ref[base + j], TABLE_ROWS_PER_EXPERT)
        rows.append(pltpu.bitcast(tab_ref[pl.ds(start, TABLE_ROWS_PER_EXPERT), :], BF16))
    return jnp.concatenate(rows, axis=0)


def _split_bf16(x):
    hi = x.astype(BF16)
    return hi, (x - hi.astype(F32)).astype(BF16)


def _chunk_mask(slots):
    lane = lax.broadcasted_iota(jnp.int32, (SUBLANES, slots * SUBLANES), 1)
    sub = lax.broadcasted_iota(jnp.int32, (SUBLANES, slots * SUBLANES), 0)
    return (lane & (SUBLANES - 1)) == sub


def _slot_spread(slots):
    r = lax.broadcasted_iota(jnp.int32, (slots, slots * SUBLANES), 0)
    c = lax.broadcasted_iota(jnp.int32, (slots, slots * SUBLANES), 1)
    return jnp.where((c >> 3) == r, 1.0, 0.0).astype(BF16)


def _peer_a_kernel(e_ref, h_ref, g_ref, tab_ref, w_ref, hh_sc, z_sc, *, slots):
    tb = h_ref.shape[0]
    hi, lo = _split_bf16(h_ref[...])
    hh_sc[:, 0:SUBLANES, :] = hi
    hh_sc[:, SUBLANES:2 * SUBLANES, :] = lo
    mask = _chunk_mask(slots)

    def tokens(i, carry):
        for u in range(PEER_TOKEN_UNROLL):
            t = i * PEER_TOKEN_UNROLL + u
            g = _gather_rows(e_ref, tab_ref, pl.multiple_of(t * slots, slots), slots)
            z = lax.dot_general(hh_sc[t], g, _NT, preferred_element_type=F32)
            z8 = z[0:SUBLANES] + z[SUBLANES:2 * SUBLANES]
            z_sc[pl.ds(pl.multiple_of(t * SUBLANES, SUBLANES), SUBLANES), :] = jnp.where(mask, z8, 0.0)
        return carry

    lax.fori_loop(0, tb // PEER_TOKEN_UNROLL, tokens, 0)
    zhi, zlo = _split_bf16(z_sc[...])
    spread = _slot_spread(slots)
    a8 = (lax.dot_general(zhi, spread, _NT, preferred_element_type=F32)
          + lax.dot_general(zlo, spread, _NT, preferred_element_type=F32))
    a = jnp.sum(a8.reshape(tb, SUBLANES, slots), axis=1)
    gelu = 0.5 * a * (1.0 + lax.erf(a * (2.0 ** -0.5)))
    w_ref[...] = gelu * g_ref[...]


def peer_a(e_flat, h3, gate, tab_u32, slots):
    T = h3.shape[0]
    tb = PEER_TOKEN_BLOCK
    return pl.pallas_call(
        functools.partial(_peer_a_kernel, slots=slots),
        grid=(T // tb,),
        in_specs=[pl.BlockSpec((tb * slots,), lambda i: (i,), memory_space=pltpu.SMEM),
                  pl.BlockSpec((tb, SUBLANES, LANES), lambda i: (i, 0, 0)),
                  pl.BlockSpec((tb, slots), lambda i: (i, 0)),
                  pl.BlockSpec(tab_u32.shape, lambda i: (0, 0), pipeline_mode=pl.Buffered(1))],
        out_specs=pl.BlockSpec((tb, slots), lambda i: (i, 0)),
        out_shape=jax.ShapeDtypeStruct((T, slots), F32),
        scratch_shapes=[pltpu.VMEM((tb, 2 * SUBLANES, LANES), BF16),
                        pltpu.VMEM((tb * SUBLANES, slots * SUBLANES), F32)],
        compiler_params=_cparams(1, VMEM_LIMIT_TABLE),
        name="peer_a",
    )(e_flat, h3, gate, tab_u32)


def _peer_b_kernel(e_ref, w_ref, tab_ref, o_ref, rep_sc, *, slots):
    tb = o_ref.shape[0]
    spread = _slot_spread(slots)
    whi, wlo = _split_bf16(w_ref[...])
    rep_sc[0] = jnp.dot(whi, spread, preferred_element_type=F32)
    rep_sc[1] = jnp.dot(wlo, spread, preferred_element_type=F32)
    mask = _chunk_mask(slots)
    wide = (SUBLANES, slots * SUBLANES)

    def tokens(i, carry):
        for u in range(PEER_TOKEN_UNROLL):
            t = i * PEER_TOKEN_UNROLL + u
            g = _gather_rows(e_ref, tab_ref, pl.multiple_of(t * slots, slots), slots)
            parts = [jnp.where(mask, jnp.broadcast_to(rep_sc[p, pl.ds(t, 1), :], wide), 0.0).astype(BF16)
                     for p in range(2)]
            r = jnp.dot(jnp.concatenate(parts, axis=0), g, preferred_element_type=F32)
            o_ref[t] = r[0:SUBLANES] + r[SUBLANES:2 * SUBLANES]
        return carry

    lax.fori_loop(0, tb // PEER_TOKEN_UNROLL, tokens, 0)


def peer_b(e_flat, w, tab_u32, slots):
    T = w.shape[0]
    tb = PEER_TOKEN_BLOCK
    return pl.pallas_call(
        functools.partial(_peer_b_kernel, slots=slots),
        grid=(T // tb,),
        in_specs=[pl.BlockSpec((tb * slots,), lambda i: (i,), memory_space=pltpu.SMEM),
                  pl.BlockSpec((tb, slots), lambda i: (i, 0)),
                  pl.BlockSpec(tab_u32.shape, lambda i: (0, 0), pipeline_mode=pl.Buffered(1))],
        out_specs=pl.BlockSpec((tb, SUBLANES, LANES), lambda i: (i, 0, 0)),
        out_shape=jax.ShapeDtypeStruct((T, SUBLANES, LANES), F32),
        scratch_shapes=[pltpu.VMEM((2, tb, slots * SUBLANES), F32)],
        compiler_params=_cparams(1, VMEM_LIMIT_TABLE),
        name="peer_b",
    )(e_flat, w, tab_u32)


def _final_kernel(x_ref, ff_ref, g2_ref, lg_ref, lb_ref, o_ref, *, alpha):
    y = alpha * x_ref[...] + g2_ref[...] * ff_ref[...]
    o_ref[...] = _ln(y, lg_ref[...], lb_ref[...])


def final_ln(x1, ff, g2, ln_g, ln_b, alpha, bb, tl):
    B, L, D = x1.shape
    tok = lambda b, l: (b, l, 0)
    const = lambda b, l: (0, 0)
    return pl.pallas_call(
        functools.partial(_final_kernel, alpha=alpha),
        grid=(B // bb, L // tl),
        in_specs=[pl.BlockSpec((bb, tl, D), tok), pl.BlockSpec((bb, tl, D), tok),
                  pl.BlockSpec((bb, 1, D), lambda b, l: (b, 0, 0)),
                  pl.BlockSpec((1, D), const), pl.BlockSpec((1, D), const)],
        out_specs=pl.BlockSpec((bb, tl, D), tok),
        out_shape=jax.ShapeDtypeStruct((B, L, D), F32),
        compiler_params=_cparams(2),
        name="final_ln",
    )(x1, ff, g2, ln_g.reshape(1, D), ln_b.reshape(1, D))


def _rope_tables(past, length, n_groups):
    half = ATT_HEAD_DIM // 2
    inv = 1.0 / (ROPE_THETA ** (jnp.arange(half, dtype=F32) / half))
    ang = (past + jnp.arange(length)).astype(F32)[:, None] * inv[None, :]
    cos = jnp.cos(ang)
    sin = jnp.sin(ang)
    cos_t = jnp.tile(jnp.concatenate([cos, cos], axis=-1), (1, n_groups))
    sin_t = jnp.tile(jnp.concatenate([-sin, sin], axis=-1), (1, n_groups))
    return cos_t, sin_t


def _token_tiles(B, L, target):
    if L >= target:
        return 1, target
    bb = max(1, min(B, target // L))
    while B % bb:
        bb -= 1
    return bb, L


def _layer(x, mod, conv_hist, k_past, v_past, wts, lam_init, alpha):
    B, L, D = x.shape
    conv_dim = wts["w_dw"].shape[1]
    qk_dim = ATT_HEADS * 2 * ATT_HEAD_DIM
    att_dim = ATT_HEADS * ATT_V_DIM
    width = wts["w_dw"].shape[0]
    assert L >= width - 1 and width - 1 <= HIST_PAD
    sh1, sc1, g1, sh2, sc2, g2 = [m[:, None, :] for m in jnp.split(mod, 6, axis=-1)]
    past = 0 if k_past is None else k_past.shape[1]
    cos, sin = _rope_tables(past, L, qk_dim // ATT_HEAD_DIM)
    bb, tl = _token_tiles(B, L, 512)

    u, k, v, qb, kb, vb = in_proj(x, sh1, sc1, wts["w_in"], wts["b_in"], cos, sin, bb, tl,
                                  conv_dim, qk_dim, att_dim, transposed_v=k_past is None)
    hist_pad = jnp.pad(conv_hist, ((0, 0), (HIST_PAD - (width - 1), 0), (0, 0)))
    conv_out = conv_ln(u, hist_pad, wts["w_dw"], wts["b_dw"], wts["conv_ln_g"], wts["conv_ln_b"], min(L, 256))
    conv_state = u[:, L - (width - 1):, :]

    if k_past is None:
        att = attn_prompt(qb, kb, vb, wts["lam_rows"], wts["subln_g"], lam_init, min(L, 256))
    else:
        att = attn_sample(qb, k_past.reshape(B, past, qk_dim), v_past.reshape(B, past, att_dim), kb, vb,
                          wts["lam_rows"], wts["subln_g"], lam_init)

    x1, h2, h2b = out_proj(conv_out, att, x, g1, sh2, sc2, wts["w_out_conv"], wts["w_out_att"], wts["b_out"],
                           wts["ln1_g"], wts["ln1_b"], alpha, bb, tl)

    T = B * L
    slots = (wts["sub_keys"].shape[0] // 2) * PEER_TOPK
    e, gate = peer_topk(h2b.reshape(T, D), wts["w_query"], wts["sub_keys"], min(T, 512))
    e_flat = e.reshape(T * slots)
    w = peer_a(e_flat, h2.reshape(T, SUBLANES, LANES), gate, wts["u_tab"], slots)
    ff = peer_b(e_flat, w, wts["v_tab"], slots)
    out = final_ln(x1, ff.reshape(B, L, D), g2, wts["ln2_g"], wts["ln2_b"], alpha, bb, tl)
    return out, conv_state, k, v


def kernel(x_prompt, x_sample, cache_k, cache_v, cache_conv, c_prompt, c_sample, w_ada, b_ada, w_in, b_in, w_dw, b_dw, conv_ln_g, conv_ln_b, lam_q1, lam_k1, lam_q2, lam_k2, subln_g, w_out, b_out, ln1_g, ln1_b, w_query, sub_keys, u_tab, v_tab, ln2_g, ln2_b):
    depth = w_ada.shape[0]
    D = x_prompt.shape[-1]
    assert D == SUBLANES * LANES
    alpha = (2 * depth) ** 0.25
    Bp, Bs = c_prompt.shape[0], c_sample.shape[0]
    xp, xs = x_prompt, x_sample
    outs = [[] for _ in range(6)]
    for l in range(depth):
        lam_init = 0.8 - 0.6 * math.exp(-0.3 * l)
        conv_dim = w_dw.shape[2]
        n_exp = u_tab.shape[1]
        wts = {
            "w_in": w_in[l].astype(BF16), "b_in": b_in[l], "w_dw": w_dw[l], "b_dw": b_dw[l],
            "conv_ln_g": conv_ln_g[l], "conv_ln_b": conv_ln_b[l],
            "lam_rows": jnp.stack([lam_q1[l], lam_k1[l], lam_q2[l], lam_k2[l]]).astype(F32),
            "subln_g": subln_g[l],
            "w_out_conv": w_out[l, :conv_dim].astype(BF16), "w_out_att": w_out[l, conv_dim:].astype(BF16),
            "b_out": b_out[l], "ln1_g": ln1_g[l], "ln1_b": ln1_b[l],
            "w_query": w_query[l].astype(BF16),
            "sub_keys": sub_keys[l].astype(BF16).reshape(-1, sub_keys.shape[3], sub_keys.shape[4]),
            "u_tab": pack_table(u_tab[l]), "v_tab": pack_table(v_tab[l]),
            "ln2_g": ln2_g[l], "ln2_b": ln2_b[l],
        }
        c_all = jnp.concatenate([c_prompt, c_sample], axis=0)
        pad = (-c_all.shape[0]) % 16
        c_all = jnp.pad(c_all, ((0, pad), (0, 0)))
        mod = ada_mod(c_all, w_ada[l].astype(BF16), b_ada[l])
        hist0 = jnp.zeros((Bp, w_dw.shape[1] - 1, conv_dim), xp.dtype)
        xp, cp, kp, vp = _layer(xp, mod[:Bp], hist0, None, None, wts, lam_init, alpha)
        xs, cs, kn, vn = _layer(xs, mod[Bp:Bp + Bs], cache_conv[l], cache_k[l], cache_v[l], wts, lam_init, alpha)
        for lst, val in zip(outs, (kp, vp, cp, kn, vn, cs)):
            lst.append(val)
    Lp, Ls = x_prompt.shape[1], x_sample.shape[1]
    k_prompt = jnp.stack(outs[0]).reshape(depth, Bp, Lp, ATT_HEADS, 2, ATT_HEAD_DIM)
    v_prompt = jnp.stack(outs[1]).reshape(depth, Bp, Lp, ATT_HEADS, ATT_V_DIM)
    conv_prompt = jnp.stack(outs[2])
    k_sample = jnp.stack(outs[3]).reshape(depth, Bs, Ls, ATT_HEADS, 2, ATT_HEAD_DIM)
    v_sample = jnp.stack(outs[4]).reshape(depth, Bs, Ls, ATT_HEADS, ATT_V_DIM)
    conv_sample = jnp.stack(outs[5])
    return (xp, xs, k_prompt, v_prompt, conv_prompt, k_sample, v_sample, conv_sample)
```

```python
import functools
import math

import jax
import jax.numpy as jnp
import numpy as np
from jax import lax
from jax.experimental import pallas as pl
from jax.experimental.pallas import tpu as pltpu

F32 = jnp.float32
BF16 = jnp.bfloat16

CHUNK = 64
ATT_HEADS = 4
ATT_HEAD_DIM = 64
ATT_V_DIM = 2 * ATT_HEAD_DIM
ROPE_THETA = 10000.0
PEER_TOPK = 16
LN_EPS = 1e-5

LANES = 128
SUBLANES = 8
VMEM_LIMIT_DEFAULT = 48 * 1024 * 1024
VMEM_LIMIT_TABLE = 56 * 1024 * 1024

_NT = (((1,), (1,)), ((), ()))


def _cparams(n_axes, vmem=VMEM_LIMIT_DEFAULT):
    return pltpu.CompilerParams(dimension_semantics=("arbitrary",) * n_axes, vmem_limit_bytes=vmem)


def _ln(y, g, b):
    mu = jnp.mean(y, axis=-1, keepdims=True)
    d = y - mu
    var = jnp.mean(d * d, axis=-1, keepdims=True)
    return d * lax.rsqrt(var + LN_EPS) * g + b


def _ada_kernel(c_ref, w_ref, b_ref, o_ref):
    c = c_ref[...]
    s = (c * jax.nn.sigmoid(c)).astype(BF16)
    o_ref[...] = jnp.dot(s, w_ref[...], preferred_element_type=F32) + b_ref[...]


def ada_mod(c, w_bf, b):
    rows, d = c.shape
    n = w_bf.shape[1]
    tn = min(n, 1024)
    return pl.pallas_call(
        _ada_kernel,
        grid=(n // tn,),
        in_specs=[pl.BlockSpec((rows, d), lambda j: (0, 0)),
                  pl.BlockSpec((d, tn), lambda j: (0, j)),
                  pl.BlockSpec((1, tn), lambda j: (0, j))],
        out_specs=pl.BlockSpec((rows, tn), lambda j: (0, j)),
        out_shape=jax.ShapeDtypeStruct((rows, n), F32),
        compiler_params=_cparams(1),
        name="ada_mod",
    )(c, w_bf, b.reshape(1, n))


def _inproj_kernel(x_ref, sh_ref, sc_ref, w_ref, b_ref, cos_ref, sin_ref,
                   wvt_ref, bvt_ref, u_ref, k_ref, v_ref, qb_ref, kb_ref, vb_ref, *, conv_dim, qk_dim,
                   transposed_v):
    bb, tl, d = x_ref.shape
    h = x_ref[...] * (1.0 + sc_ref[...]) + sh_ref[...]
    hb = h.reshape(bb * tl, d).astype(BF16)
    z = jnp.dot(hb, w_ref[...], preferred_element_type=F32) + b_ref[...]
    c0, c1, c2, c3 = conv_dim, 2 * conv_dim, 2 * conv_dim + qk_dim, 2 * conv_dim + 2 * qk_dim
    u = z[:, :c0] * jax.nn.sigmoid(z[:, c0:c1])
    u_ref[...] = u.reshape(bb, tl, conv_dim)

    lane = lax.broadcasted_iota(jnp.int32, (bb * tl, qk_dim), 1)
    first_half = (lane % ATT_HEAD_DIM) < (ATT_HEAD_DIM // 2)
    cos = jnp.broadcast_to(cos_ref[...][None], (bb, tl, qk_dim)).reshape(bb * tl, qk_dim)
    sin = jnp.broadcast_to(sin_ref[...][None], (bb, tl, qk_dim)).reshape(bb * tl, qk_dim)

    def rope(t):
        swapped = jnp.where(first_half,
                            pltpu.roll(t, qk_dim - ATT_HEAD_DIM // 2, axis=1),
                            pltpu.roll(t, ATT_HEAD_DIM // 2, axis=1))
        return t * cos + swapped * sin

    q = rope(z[:, c1:c2]) * (ATT_HEAD_DIM ** -0.5)
    k = rope(z[:, c2:c3])
    v = z[:, c3:]
    k_ref[...] = k.reshape(bb, tl, qk_dim)
    v_ref[...] = v.reshape(bb, tl, v.shape[1])
    qb_ref[...] = q.astype(BF16).reshape(bb, tl, qk_dim)
    kb_ref[...] = k.astype(BF16).reshape(bb, tl, qk_dim)
    if transposed_v:
        vt = lax.dot_general(wvt_ref[...], hb, _NT, preferred_element_type=F32) + bvt_ref[...]
        vb_ref[0] = vt.astype(BF16)
    else:
        vb_ref[...] = v.astype(BF16).reshape(bb, tl, v.shape[1])


def in_proj(x, sh1, sc1, w_bf, b_in, cos, sin, bb, tl, conv_dim, qk_dim, att_dim, transposed_v):
    B, L, D = x.shape
    n = w_bf.shape[1]
    assert not transposed_v or bb == 1
    tok = lambda b, l: (b, l, 0)
    mod = lambda b, l: (b, 0, 0)
    const = lambda b, l: (0, 0)
    out_shapes = [jax.ShapeDtypeStruct((B, L, conv_dim), F32),
                  jax.ShapeDtypeStruct((B, L, qk_dim), F32),
                  jax.ShapeDtypeStruct((B, L, att_dim), F32),
                  jax.ShapeDtypeStruct((B, L, qk_dim), BF16),
                  jax.ShapeDtypeStruct((B, L, qk_dim), BF16),
                  jax.ShapeDtypeStruct((B, L, att_dim), BF16)]
    out_specs = [pl.BlockSpec((bb, tl, s.shape[2]), tok) for s in out_shapes]
    if transposed_v:
        out_shapes[5] = jax.ShapeDtypeStruct((B, att_dim, L), BF16)
        out_specs[5] = pl.BlockSpec((1, att_dim, tl), lambda b, l: (b, 0, l))
    wvt = jnp.transpose(w_bf[:, n - att_dim:])
    bvt = b_in[n - att_dim:].reshape(att_dim, 1)
    return pl.pallas_call(
        functools.partial(_inproj_kernel, conv_dim=conv_dim, qk_dim=qk_dim, transposed_v=transposed_v),
        grid=(B // bb, L // tl),
        in_specs=[pl.BlockSpec((bb, tl, D), tok),
                  pl.BlockSpec((bb, 1, D), mod),
                  pl.BlockSpec((bb, 1, D), mod),
                  pl.BlockSpec((D, n), const),
                  pl.BlockSpec((1, n), const),
                  pl.BlockSpec((tl, qk_dim), lambda b, l: (l, 0)),
                  pl.BlockSpec((tl, qk_dim), lambda b, l: (l, 0)),
                  pl.BlockSpec((att_dim, D), const),
                  pl.BlockSpec((att_dim, 1), const)],
        out_specs=out_specs,
        out_shape=out_shapes,
        compiler_params=_cparams(2),
        name="in_proj",
    )(x, sh1, sc1, w_bf, b_in.reshape(1, n), cos, sin, wvt, bvt)


HIST_PAD = 32


def _conv_kernel(u_ref, prev_ref, hist_ref, w_ref, b_ref, g_ref, beta_ref, o_ref, win_ref, *, width):
    i = pl.program_id(1)
    tl = u_ref.shape[1]

    @pl.when(i == 0)
    def _():
        win_ref[0:HIST_PAD, :] = hist_ref[0]

    @pl.when(i != 0)
    def _():
        win_ref[0:HIST_PAD, :] = prev_ref[0]

    win_ref[HIST_PAD:HIST_PAD + tl, :] = u_ref[0]
    off = HIST_PAD - (width - 1)
    acc = w_ref[0:1, :] * win_ref[off:off + tl, :]
    for j in range(1, width):
        acc = acc + w_ref[j:j + 1, :] * win_ref[off + j:off + j + tl, :]
    y = _ln(acc + b_ref[...], g_ref[...], beta_ref[...])
    o_ref[0] = (y * jax.nn.sigmoid(y)).astype(BF16)


def conv_ln(u, hist_pad, w_dw, b_dw, ln_g, ln_b, tl):
    B, L, C = u.shape
    width = w_dw.shape[0]
    steps = tl // HIST_PAD
    row = lambda b, i: (0, 0)
    return pl.pallas_call(
        functools.partial(_conv_kernel, width=width),
        grid=(B, L // tl),
        in_specs=[pl.BlockSpec((1, tl, C), lambda b, i: (b, i, 0)),
                  pl.BlockSpec((1, HIST_PAD, C), lambda b, i: (b, jnp.maximum(i * steps - 1, 0), 0)),
                  pl.BlockSpec((1, HIST_PAD, C), lambda b, i: (b, 0, 0)),
                  pl.BlockSpec((width, C), row),
                  pl.BlockSpec((1, C), row), pl.BlockSpec((1, C), row), pl.BlockSpec((1, C), row)],
        out_specs=pl.BlockSpec((1, tl, C), lambda b, i: (b, i, 0)),
        out_shape=jax.ShapeDtypeStruct((B, L, C), BF16),
        scratch_shapes=[pltpu.VMEM((HIST_PAD + tl, C), F32)],
        compiler_params=_cparams(2),
        name="conv_ln",
    )(u, u, hist_pad, w_dw, b_dw.reshape(1, C), ln_g.reshape(1, C), ln_b.reshape(1, C))


def _lambda_value(lam_ref, lam_init):
    s1 = jnp.sum(lam_ref[0:1, :] * lam_ref[1:2, :], axis=1, keepdims=True)
    s2 = jnp.sum(lam_ref[2:3, :] * lam_ref[3:4, :], axis=1, keepdims=True)
    return jnp.exp(s1) - jnp.exp(s2) + lam_init


def _split_maps(q):
    lane = lax.broadcasted_iota(jnp.int32, q.shape, 1)
    zero = jnp.zeros_like(q)
    return jnp.where(lane < ATT_HEAD_DIM, q, zero), jnp.where(lane >= ATT_HEAD_DIM, q, zero)


def _finish_heads(o0, o1, lam, g, lam_init):
    o = o0 - lam * o1
    o = o * lax.rsqrt(jnp.mean(o * o, axis=-1, keepdims=True) + LN_EPS) * g
    return o * (1.0 - lam_init)


def _attn_prompt_kernel(lam_ref, g_ref, q_ref, k_ref, vt_ref, o_ref, m_sc, l_sc, acc_sc, *, lam_init):
    i = pl.program_id(2)
    tq = q_ref.shape[1]
    q2 = jnp.concatenate(_split_maps(q_ref[0]), axis=0)
    m_sc[...] = jnp.full(m_sc.shape, -jnp.inf, F32)
    l_sc[...] = jnp.zeros(l_sc.shape, F32)
    acc_sc[...] = jnp.zeros(acc_sc.shape, F32)

    kpos = lax.broadcasted_iota(jnp.int32, (tq, 2 * tq), 0)
    qpos = lax.broadcasted_iota(jnp.int32, (tq, 2 * tq), 1) % tq
    allowed = (kpos // CHUNK) <= (qpos // CHUNK)

    def scores(j):
        kb = k_ref[0, pl.ds(pl.multiple_of(j * tq, tq), tq), :]
        return lax.dot_general(kb, q2, _NT, preferred_element_type=F32)

    def update(s, j):
        vt = vt_ref[0, :, pl.ds(pl.multiple_of(j * tq, tq), tq)]
        m_prev = m_sc[...]
        m_new = jnp.maximum(m_prev, jnp.max(s, axis=0, keepdims=True))
        alpha = jnp.exp(m_prev - m_new)
        pe = jnp.exp(s - m_new)
        l_sc[...] = alpha * l_sc[...] + jnp.sum(pe, axis=0, keepdims=True)
        acc_sc[...] = alpha * acc_sc[...] + jnp.dot(vt, pe.astype(BF16), preferred_element_type=F32)
        m_sc[...] = m_new

    def pair(jj, c):
        s_a, s_b = scores(2 * jj), scores(2 * jj + 1)
        update(s_a, 2 * jj)
        update(s_b, 2 * jj + 1)
        return c

    lax.fori_loop(0, i // 2, pair, 0)

    @pl.when(i % 2 == 1)
    def _():
        s_a, s_b = scores(i - 1), jnp.where(allowed, scores(i), -jnp.inf)
        update(s_a, i - 1)
        update(s_b, i)

    @pl.when(i % 2 == 0)
    def _():
        update(jnp.where(allowed, scores(i), -jnp.inf), i)

    lam = _lambda_value(lam_ref, lam_init)
    o = acc_sc[...] / l_sc[...]
    o = o[:, :tq] - lam * o[:, tq:]
    o = o * lax.rsqrt(jnp.mean(o * o, axis=0, keepdims=True) + LN_EPS) * g_ref[...]
    o_ref[0] = jnp.transpose(o * (1.0 - lam_init)).astype(BF16)


def attn_prompt(qb, kb, vt, lam_rows, subln_g, lam_init, tq):
    B, L, _ = qb.shape
    return pl.pallas_call(
        functools.partial(_attn_prompt_kernel, lam_init=lam_init),
        grid=(B, ATT_HEADS, L // tq),
        in_specs=[pl.BlockSpec((4, ATT_HEAD_DIM), lambda b, h, i: (0, 0)),
                  pl.BlockSpec((ATT_V_DIM, 1), lambda b, h, i: (0, 0)),
                  pl.BlockSpec((1, tq, LANES), lambda b, h, i: (b, i, h)),
                  pl.BlockSpec((1, L, LANES), lambda b, h, i: (b, 0, h)),
                  pl.BlockSpec((1, ATT_V_DIM, L), lambda b, h, i: (b, h, 0))],
        out_specs=pl.BlockSpec((1, tq, LANES), lambda b, h, i: (b, i, h)),
        out_shape=jax.ShapeDtypeStruct((B, L, ATT_HEADS * ATT_V_DIM), BF16),
        scratch_shapes=[pltpu.VMEM((1, 2 * tq), F32), pltpu.VMEM((1, 2 * tq), F32),
                        pltpu.VMEM((ATT_V_DIM, 2 * tq), F32)],
        compiler_params=_cparams(3),
        name="attn_prompt",
    )(lam_rows, subln_g.reshape(ATT_V_DIM, 1), qb, kb, vt)


def _attn_sample_kernel(lam_ref, g_ref, q_ref, ck_ref, cv_ref, kn_ref, vn_ref, o_ref, *, lam_init, past):
    ls = q_ref.shape[1]
    qs = _split_maps(q_ref[0])
    kp = ck_ref[0].astype(BF16)
    vp = cv_ref[0].astype(BF16)
    kn = kn_ref[0]
    vn = vn_ref[0]
    qpos = past + lax.broadcasted_iota(jnp.int32, (ls, ls), 0)
    kpos = past + lax.broadcasted_iota(jnp.int32, (ls, ls), 1)
    allowed = (kpos // CHUNK) <= (qpos // CHUNK)
    outs = []
    for p in range(2):
        sp = lax.dot_general(qs[p], kp, _NT, preferred_element_type=F32)
        sn = lax.dot_general(qs[p], kn, _NT, preferred_element_type=F32)
        sn = jnp.where(allowed, sn, -jnp.inf)
        m = jnp.maximum(jnp.max(sp, axis=1, keepdims=True), jnp.max(sn, axis=1, keepdims=True))
        pp = jnp.exp(sp - m)
        pn = jnp.exp(sn - m)
        denom = jnp.sum(pp, axis=1, keepdims=True) + jnp.sum(pn, axis=1, keepdims=True)
        acc = (jnp.dot(pp.astype(BF16), vp, preferred_element_type=F32)
               + jnp.dot(pn.astype(BF16), vn, preferred_element_type=F32))
        outs.append(acc / denom)
    lam = _lambda_value(lam_ref, lam_init)
    o_ref[0] = _finish_heads(outs[0], outs[1], lam, g_ref[...], lam_init).astype(BF16)


def attn_sample(qb, cache_k, cache_v, kb, vb, lam_rows, subln_g, lam_init):
    B, Ls, _ = qb.shape
    P = cache_k.shape[1]
    blk = lambda n: pl.BlockSpec((1, n, LANES), lambda b, h: (b, 0, h))
    return pl.pallas_call(
        functools.partial(_attn_sample_kernel, lam_init=lam_init, past=P),
        grid=(B, ATT_HEADS),
        in_specs=[pl.BlockSpec((4, ATT_HEAD_DIM), lambda b, h: (0, 0)),
                  pl.BlockSpec((1, ATT_V_DIM), lambda b, h: (0, 0)),
                  blk(Ls), blk(P), blk(P), blk(Ls), blk(Ls)],
        out_specs=blk(Ls),
        out_shape=jax.ShapeDtypeStruct((B, Ls, ATT_HEADS * ATT_V_DIM), BF16),
        compiler_params=_cparams(2),
        name="attn_sample",
    )(lam_rows, subln_g.reshape(1, ATT_V_DIM), qb, cache_k, cache_v, kb, vb)


def _outproj_kernel(conv_ref, att_ref, x_ref, g1_ref, sh2_ref, sc2_ref, wc_ref, wa_ref, b_ref,
                    lg_ref, lb_ref, x1_ref, hh_ref, h2b_ref, *, alpha):
    bb, tl, d = x_ref.shape
    conv = conv_ref[...].reshape(bb * tl, conv_ref.shape[2])
    att = att_ref[...].reshape(bb * tl, att_ref.shape[2])
    mm = (jnp.dot(conv, wc_ref[...], preferred_element_type=F32)
          + jnp.dot(att, wa_ref[...], preferred_element_type=F32) + b_ref[...])
    y = alpha * x_ref[...] + g1_ref[...] * mm.reshape(bb, tl, d)
    x1 = _ln(y, lg_ref[...], lb_ref[...])
    x1_ref[...] = x1
    h2 = x1 * (1.0 + sc2_ref[...]) + sh2_ref[...]
    h2b_ref[...] = h2.astype(BF16)
    hi, lo = _split_bf16(h2.reshape(bb * tl, d))
    bits = [lax.bitcast_convert_type(part.astype(F32), jnp.uint32) for part in (hi, lo)]
    rows = [b[:, c * LANES:(c + 1) * LANES] for b in bits for c in range(d // LANES)]
    for i in range(len(rows) // 2):
        hh_ref[:, i, :] = (rows[2 * i] >> 16) | rows[2 * i + 1]


def out_proj(conv_out, att, x, g1, sh2, sc2, wc_bf, wa_bf, b_out, ln_g, ln_b, alpha, bb, tl):
    B, L, D = x.shape
    tok = lambda b, l: (b, l, 0)
    mod = lambda b, l: (b, 0, 0)
    const = lambda b, l: (0, 0)
    return pl.pallas_call(
        functools.partial(_outproj_kernel, alpha=alpha),
        grid=(B // bb, L // tl),
        in_specs=[pl.BlockSpec((bb, tl, conv_out.shape[2]), tok),
                  pl.BlockSpec((bb, tl, att.shape[2]), tok),
                  pl.BlockSpec((bb, tl, D), tok),
                  pl.BlockSpec((bb, 1, D), mod), pl.BlockSpec((bb, 1, D), mod), pl.BlockSpec((bb, 1, D), mod),
                  pl.BlockSpec(wc_bf.shape, const), pl.BlockSpec(wa_bf.shape, const),
                  pl.BlockSpec((1, D), const), pl.BlockSpec((1, D), const), pl.BlockSpec((1, D), const)],
        out_specs=[pl.BlockSpec((bb, tl, D), tok),
                   pl.BlockSpec((bb * tl, D // LANES, LANES), lambda b, l: (b * (L // tl) + l, 0, 0)),
                   pl.BlockSpec((bb, tl, D), tok)],
        out_shape=[jax.ShapeDtypeStruct((B, L, D), F32),
                   jax.ShapeDtypeStruct((B * L, D // LANES, LANES), jnp.uint32),
                   jax.ShapeDtypeStruct((B, L, D), BF16)],
        compiler_params=_cparams(2),
        name="out_proj",
    )(conv_out, att, x, g1, sh2, sc2, wc_bf, wa_bf, b_out.reshape(1, D), ln_g.reshape(1, D), ln_b.reshape(1, D))


TOPK_STREAMS = 2


def _extract_top(vals, order, count, payload=None):
    big = jnp.float32(1e9)
    out_v, out_p = [], []
    for _ in range(count):
        m = jnp.max(vals, axis=0, keepdims=True)
        pos = jnp.min(jnp.where(vals == m, order, big), axis=0, keepdims=True)
        sel = order == pos
        out_v.append(m)
        out_p.append(pos if payload is None else jnp.max(jnp.where(sel, payload, -1.0), axis=0, keepdims=True))
        vals = jnp.where(sel, -jnp.inf, vals)
    return jnp.concatenate(out_v, axis=0), jnp.concatenate(out_p, axis=0)


def _topk_kernel(h_ref, wq_ref, keys_ref, e_ref, g_ref, q_sc, e_sc, g_sc, *, n_heads, n_keys):
    tl = h_ref.shape[0]
    n_chunks = tl // LANES
    K = PEER_TOPK
    qp = jnp.dot(h_ref[...], wq_ref[...], preferred_element_type=F32)
    for hp in range(2 * n_heads):
        q_sc[hp] = qp[:, hp * LANES:(hp + 1) * LANES].astype(BF16)

    key_iota = lax.broadcasted_iota(jnp.int32, (n_keys, LANES), 0).astype(F32)
    n_cand = K + (K - 1) * SUBLANES
    row = lax.broadcasted_iota(jnp.int32, (n_cand, LANES), 0)
    ci = jnp.where(row < K, 0, ((row - K) >> 3) + 1)
    cj = jnp.where(row < K, row, (row - K) & 7)
    limit = jnp.where(ci == 0, 16, jnp.where(ci == 1, 8, jnp.where(ci == 2, 5, jnp.where(
        ci == 3, 4, jnp.where(ci == 4, 3, jnp.where(ci <= 7, 2, 1))))))
    cand_ok = cj < limit
    cand_order = (ci * K + cj).astype(F32)

    def head_chunk(h, col):
        halves = []
        for p in range(2):
            qc = q_sc[2 * h + p, pl.ds(col, LANES), :]
            s = lax.dot_general(keys_ref[2 * h + p], qc, _NT, preferred_element_type=F32)
            halves.append(_extract_top(s, key_iota, K))
        (sv0, si0), (sv1, si1) = halves
        cv = [sv0[0:1] + sv1]
        ce = [si0[0:1] * n_keys + si1]
        for r in range(1, K):
            cv.append(sv0[r:r + 1] + sv1[0:SUBLANES])
            ce.append(si0[r:r + 1] * n_keys + si1[0:SUBLANES])
        cand = jnp.where(cand_ok, jnp.concatenate(cv, axis=0), -jnp.inf)
        fv, fe = _extract_top(cand, cand_order, K, payload=jnp.concatenate(ce, axis=0))
        ex = jnp.exp(fv - fv[0:1])
        gate = ex / jnp.sum(ex, axis=0, keepdims=True)
        off = pl.multiple_of(h * K, K)
        e_sc[pl.ds(off, K), pl.ds(col, LANES)] = fe
        g_sc[pl.ds(off, K), pl.ds(col, LANES)] = gate

    streams = TOPK_STREAMS if n_chunks % TOPK_STREAMS == 0 else 1
    groups = n_chunks // streams

    def step(i, carry):
        h = i // groups
        for c in range(streams):
            head_chunk(h, pl.multiple_of(((i % groups) * streams + c) * LANES, LANES))
        return carry

    lax.fori_loop(0, n_heads * groups, step, 0)
    e_ref[...] = jnp.transpose(e_sc[...]).astype(jnp.int32) * TABLE_ROWS_PER_EXPERT
    g_ref[...] = jnp.transpose(g_sc[...])


def peer_topk(h2b, wq_bf, keys_bf, tl):
    T, D = h2b.shape
    nq = wq_bf.shape[1]
    n_hp, n_keys, half = keys_bf.shape
    n_heads = n_hp // 2
    slots = n_heads * PEER_TOPK
    assert half == LANES and n_keys == LANES and PEER_TOPK * PEER_TOPK <= 256
    return pl.pallas_call(
        functools.partial(_topk_kernel, n_heads=n_heads, n_keys=n_keys),
        grid=(T // tl,),
        in_specs=[pl.BlockSpec((tl, D), lambda i: (i, 0)),
                  pl.BlockSpec((D, nq), lambda i: (0, 0)),
                  pl.BlockSpec((n_hp, n_keys, half), lambda i: (0, 0, 0))],
        out_specs=[pl.BlockSpec((tl, slots), lambda i: (i, 0))] * 2,
        out_shape=[jax.ShapeDtypeStruct((T, slots), jnp.int32), jax.ShapeDtypeStruct((T, slots), F32)],
        scratch_shapes=[pltpu.VMEM((n_hp, tl, LANES), BF16), pltpu.VMEM((slots, tl), F32),
                        pltpu.VMEM((slots, tl), F32)],
        compiler_params=_cparams(1),
        name="peer_topk",
    )(h2b, wq_bf, keys_bf)


TABLE_ROWS_PER_EXPERT = 4
PEER_TOKEN_BLOCK = 128
PEER_TOKEN_UNROLL = 8


PACK_EXPERT_BLOCK = 512


def _pack_kernel(t_ref, o_ref):
    n = t_ref.shape[0]
    bits = lax.bitcast_convert_type(t_ref[...].astype(BF16).astype(F32), jnp.uint32)
    for i in range(TABLE_ROWS_PER_EXPERT):
        lo = bits[:, (2 * i) * LANES:(2 * i + 1) * LANES]
        hi = bits[:, (2 * i + 1) * LANES:(2 * i + 2) * LANES]
        o_ref[pl.ds(i, n, stride=TABLE_ROWS_PER_EXPERT), :] = (lo >> 16) | hi


def pack_table(tab):
    n, d = tab.shape
    assert d == 2 * LANES * TABLE_ROWS_PER_EXPERT
    tn = min(n, PACK_EXPERT_BLOCK)
    return pl.pallas_call(
        _pack_kernel,
        grid=(n // tn,),
        in_specs=[pl.BlockSpec((tn, d), lambda i: (i, 0))],
        out_specs=pl.BlockSpec((tn * TABLE_ROWS_PER_EXPERT, LANES), lambda i: (i, 0)),
        out_shape=jax.ShapeDtypeStruct((n * TABLE_ROWS_PER_EXPERT, LANES), jnp.uint32),
        compiler_params=_cparams(1),
        name="pack_table",
    )(tab)


def _gather_rows(e_ref, tab_ref, base, slots):
    rows = []
    for j in range(slots):
        start = pl.multiple_of(e_ref[base + j], TABLE_ROWS_PER_EXPERT)
        rows.append(pltpu.bitcast(tab_ref[pl.ds(start, TABLE_ROWS_PER_EXPERT), :], BF16))
    return jnp.concatenate(rows, axis=0)


def _split_bf16(x):
    hi = x.astype(BF16)
    return hi, (x - hi.astype(F32)).astype(BF16)


def _chunk_mask(slots):
    lane = lax.broadcasted_iota(jnp.int32, (SUBLANES, slots * SUBLANES), 1)
    sub = lax.broadcasted_iota(jnp.int32, (SUBLANES, slots * SUBLANES), 0)
    return (lane & (SUBLANES - 1)) == sub


def _slot_spread(slots):
    r = lax.broadcasted_iota(jnp.int32, (slots, slots * SUBLANES), 0)
    c = lax.broadcasted_iota(jnp.int32, (slots, slots * SUBLANES), 1)
    return jnp.where((c >> 3) == r, 1.0, 0.0).astype(BF16)


def _peer_a_kernel(e_ref, hh_ref, g_ref, tab_ref, w_ref, z_sc, *, slots):
    tb = hh_ref.shape[0]
    mask = _chunk_mask(slots)

    def tokens(i, carry):
        for u in range(PEER_TOKEN_UNROLL):
            t = i * PEER_TOKEN_UNROLL + u
            g = _gather_rows(e_ref, tab_ref, pl.multiple_of(t * slots, slots), slots)
            hh = pltpu.bitcast(hh_ref[t], BF16)
            z = lax.dot_general(hh, g, _NT, preferred_element_type=F32)
            z8 = z[0:SUBLANES] + z[SUBLANES:2 * SUBLANES]
            z_sc[pl.ds(pl.multiple_of(t * SUBLANES, SUBLANES), SUBLANES), :] = jnp.where(mask, z8, 0.0)
        return carry

    lax.fori_loop(0, tb // PEER_TOKEN_UNROLL, tokens, 0)
    zhi, zlo = _split_bf16(z_sc[...])
    spread = _slot_spread(slots)
    a8 = (lax.dot_general(zhi, spread, _NT, preferred_element_type=F32)
          + lax.dot_general(zlo, spread, _NT, preferred_element_type=F32))
    a = jnp.sum(a8.reshape(tb, SUBLANES, slots), axis=1)
    gelu = 0.5 * a * (1.0 + lax.erf(a * (2.0 ** -0.5)))
    w_ref[...] = gelu * g_ref[...]


def peer_a(e_flat, hh, gate, tab_u32, slots):
    T = hh.shape[0]
    tb = PEER_TOKEN_BLOCK
    return pl.pallas_call(
        functools.partial(_peer_a_kernel, slots=slots),
        grid=(T // tb,),
        in_specs=[pl.BlockSpec((tb * slots,), lambda i: (i,), memory_space=pltpu.SMEM),
                  pl.BlockSpec((tb, SUBLANES, LANES), lambda i: (i, 0, 0)),
                  pl.BlockSpec((tb, slots), lambda i: (i, 0)),
                  pl.BlockSpec(tab_u32.shape, lambda i: (0, 0), pipeline_mode=pl.Buffered(1))],
        out_specs=pl.BlockSpec((tb, slots), lambda i: (i, 0)),
        out_shape=jax.ShapeDtypeStruct((T, slots), F32),
        scratch_shapes=[pltpu.VMEM((tb * SUBLANES, slots * SUBLANES), F32)],
        compiler_params=_cparams(1, VMEM_LIMIT_TABLE),
        name="peer_a",
    )(e_flat, hh, gate, tab_u32)


def _peer_b_kernel(e_ref, w_ref, tab_ref, o_ref, rep_sc, *, slots):
    tb = o_ref.shape[0]
    spread = _slot_spread(slots)
    whi, wlo = _split_bf16(w_ref[...])
    rep_sc[0] = jnp.dot(whi, spread, preferred_element_type=F32)
    rep_sc[1] = jnp.dot(wlo, spread, preferred_element_type=F32)
    mask = _chunk_mask(slots)
    wide = (SUBLANES, slots * SUBLANES)

    def tokens(i, carry):
        for u in range(PEER_TOKEN_UNROLL):
            t = i * PEER_TOKEN_UNROLL + u
            g = _gather_rows(e_ref, tab_ref, pl.multiple_of(t * slots, slots), slots)
            parts = [jnp.where(mask, jnp.broadcast_to(rep_sc[p, pl.ds(t, 1), :], wide), 0.0).astype(BF16)
                     for p in range(2)]
            r = jnp.dot(jnp.concatenate(parts, axis=0), g, preferred_element_type=F32)
            o_ref[t] = r[0:SUBLANES] + r[SUBLANES:2 * SUBLANES]
        return carry

    lax.fori_loop(0, tb // PEER_TOKEN_UNROLL, tokens, 0)


def peer_b(e_flat, w, tab_u32, slots):
    T = w.shape[0]
    tb = PEER_TOKEN_BLOCK
    return pl.pallas_call(
        functools.partial(_peer_b_kernel, slots=slots),
        grid=(T // tb,),
        in_specs=[pl.BlockSpec((tb * slots,), lambda i: (i,), memory_space=pltpu.SMEM),
                  pl.BlockSpec((tb, slots), lambda i: (i, 0)),
                  pl.BlockSpec(tab_u32.shape, lambda i: (0, 0), pipeline_mode=pl.Buffered(1))],
        out_specs=pl.BlockSpec((tb, SUBLANES, LANES), lambda i: (i, 0, 0)),
        out_shape=jax.ShapeDtypeStruct((T, SUBLANES, LANES), F32),
        scratch_shapes=[pltpu.VMEM((2, tb, slots * SUBLANES), F32)],
        compiler_params=_cparams(1, VMEM_LIMIT_TABLE),
        name="peer_b",
    )(e_flat, w, tab_u32)


def _final_kernel(x_ref, ff_ref, g2_ref, lg_ref, lb_ref, o_ref, *, alpha):
    bb, tl, d = x_ref.shape
    ff = jnp.concatenate([ff_ref[:, c, :] for c in range(d // LANES)], axis=1)
    y = alpha * x_ref[...] + g2_ref[...] * ff.reshape(bb, tl, d)
    o_ref[...] = _ln(y, lg_ref[...], lb_ref[...])


def final_ln(x1, ff, g2, ln_g, ln_b, alpha, bb, tl):
    B, L, D = x1.shape
    tok = lambda b, l: (b, l, 0)
    const = lambda b, l: (0, 0)
    return pl.pallas_call(
        functools.partial(_final_kernel, alpha=alpha),
        grid=(B // bb, L // tl),
        in_specs=[pl.BlockSpec((bb, tl, D), tok),
                  pl.BlockSpec((bb * tl, D // LANES, LANES), lambda b, l: (b * (L // tl) + l, 0, 0)),
                  pl.BlockSpec((bb, 1, D), lambda b, l: (b, 0, 0)),
                  pl.BlockSpec((1, D), const), pl.BlockSpec((1, D), const)],
        out_specs=pl.BlockSpec((bb, tl, D), tok),
        out_shape=jax.ShapeDtypeStruct((B, L, D), F32),
        compiler_params=_cparams(2),
        name="final_ln",
    )(x1, ff, g2, ln_g.reshape(1, D), ln_b.reshape(1, D))


def _rope_tables(past, length, n_groups):
    half = ATT_HEAD_DIM // 2
    inv = 1.0 / (ROPE_THETA ** (jnp.arange(half, dtype=F32) / half))
    ang = (past + jnp.arange(length)).astype(F32)[:, None] * inv[None, :]
    cos = jnp.cos(ang)
    sin = jnp.sin(ang)
    cos_t = jnp.tile(jnp.concatenate([cos, cos], axis=-1), (1, n_groups))
    sin_t = jnp.tile(jnp.concatenate([-sin, sin], axis=-1), (1, n_groups))
    return cos_t, sin_t


def _token_tiles(B, L, target):
    if L >= target:
        return 1, target
    bb = max(1, min(B, target // L))
    while B % bb:
        bb -= 1
    return bb, L


def _layer(x, mod, conv_hist, k_past, v_past, wts, lam_init, alpha):
    B, L, D = x.shape
    conv_dim = wts["w_dw"].shape[1]
    qk_dim = ATT_HEADS * 2 * ATT_HEAD_DIM
    att_dim = ATT_HEADS * ATT_V_DIM
    width = wts["w_dw"].shape[0]
    assert L >= width - 1 and width - 1 <= HIST_PAD
    sh1, sc1, g1, sh2, sc2, g2 = [m[:, None, :] for m in jnp.split(mod, 6, axis=-1)]
    past = 0 if k_past is None else k_past.shape[1]
    cos, sin = _rope_tables(past, L, qk_dim // ATT_HEAD_DIM)
    bb, tl = _token_tiles(B, L, 512)

    u, k, v, qb, kb, vb = in_proj(x, sh1, sc1, wts["w_in"], wts["b_in"], cos, sin, bb, tl,
                                  conv_dim, qk_dim, att_dim, transposed_v=k_past is None)
    hist_pad = jnp.pad(conv_hist, ((0, 0), (HIST_PAD - (width - 1), 0), (0, 0)))
    conv_out = conv_ln(u, hist_pad, wts["w_dw"], wts["b_dw"], wts["conv_ln_g"], wts["conv_ln_b"], min(L, 256))
    conv_state = u[:, L - (width - 1):, :]

    if k_past is None:
        att = attn_prompt(qb, kb, vb, wts["lam_rows"], wts["subln_g"], lam_init, min(L, 256))
    else:
        att = attn_sample(qb, k_past.reshape(B, past, qk_dim), v_past.reshape(B, past, att_dim), kb, vb,
                          wts["lam_rows"], wts["subln_g"], lam_init)

    x1, hh, h2b = out_proj(conv_out, att, x, g1, sh2, sc2, wts["w_out_conv"], wts["w_out_att"], wts["b_out"],
                           wts["ln1_g"], wts["ln1_b"], alpha, bb, tl)

    T = B * L
    slots = (wts["sub_keys"].shape[0] // 2) * PEER_TOPK
    e, gate = peer_topk(h2b.reshape(T, D), wts["w_query"], wts["sub_keys"], min(T, 512))
    e_flat = e.reshape(T * slots)
    w = peer_a(e_flat, hh, gate, wts["u_tab"], slots)
    ff = peer_b(e_flat, w, wts["v_tab"], slots)
    out = final_ln(x1, ff, g2, wts["ln2_g"], wts["ln2_b"], alpha, bb, tl)
    return out, conv_state, k, v


def kernel(x_prompt, x_sample, cache_k, cache_v, cache_conv, c_prompt, c_sample, w_ada, b_ada, w_in, b_in, w_dw, b_dw, conv_ln_g, conv_ln_b, lam_q1, lam_k1, lam_q2, lam_k2, subln_g, w_out, b_out, ln1_g, ln1_b, w_query, sub_keys, u_tab, v_tab, ln2_g, ln2_b):
    depth = w_ada.shape[0]
    D = x_prompt.shape[-1]
    assert D == SUBLANES * LANES
    alpha = (2 * depth) ** 0.25
    Bp, Bs = c_prompt.shape[0], c_sample.shape[0]
    xp, xs = x_prompt, x_sample
    outs = [[] for _ in range(6)]
    for l in range(depth):
        lam_init = 0.8 - 0.6 * math.exp(-0.3 * l)
        conv_dim = w_dw.shape[2]
        n_exp = u_tab.shape[1]
        wts = {
            "w_in": w_in[l].astype(BF16), "b_in": b_in[l], "w_dw": w_dw[l], "b_dw": b_dw[l],
            "conv_ln_g": conv_ln_g[l], "conv_ln_b": conv_ln_b[l],
            "lam_rows": jnp.stack([lam_q1[l], lam_k1[l], lam_q2[l], lam_k2[l]]).astype(F32),
            "subln_g": subln_g[l],
            "w_out_conv": w_out[l, :conv_dim].astype(BF16), "w_out_att": w_out[l, conv_dim:].astype(BF16),
            "b_out": b_out[l], "ln1_g": ln1_g[l], "ln1_b": ln1_b[l],
            "w_query": w_query[l].astype(BF16),
            "sub_keys": sub_keys[l].astype(BF16).reshape(-1, sub_keys.shape[3], sub_keys.shape[4]),
            "u_tab": pack_table(u_tab[l]), "v_tab": pack_table(v_tab[l]),
            "ln2_g": ln2_g[l], "ln2_b": ln2_b[l],
        }
        c_all = jnp.concatenate([c_prompt, c_sample], axis=0)
        pad = (-c_all.shape[0]) % 16
        c_all = jnp.pad(c_all, ((0, pad), (0, 0)))
        mod = ada_mod(c_all, w_ada[l].astype(BF16), b_ada[l])
        hist0 = jnp.zeros((Bp, w_dw.shape[1] - 1, conv_dim), xp.dtype)
        xp, cp, kp, vp = _layer(xp, mod[:Bp], hist0, None, None, wts, lam_init, alpha)
        xs, cs, kn, vn = _layer(xs, mod[Bp:Bp + Bs], cache_conv[l], cache_k[l], cache_v[l], wts, lam_init, alpha)
        for lst, val in zip(outs, (kp, vp, cp, kn, vn, cs)):
            lst.append(val)
    Lp, Ls = x_prompt.shape[1], x_sample.shape[1]
    k_prompt = jnp.stack(outs[0]).reshape(depth, Bp, Lp, ATT_HEADS, 2, ATT_HEAD_DIM)
    v_prompt = jnp.stack(outs[1]).reshape(depth, Bp, Lp, ATT_HEADS, ATT_V_DIM)
    conv_prompt = jnp.stack(outs[2])
    k_sample = jnp.stack(outs[3]).reshape(depth, Bs, Ls, ATT_HEADS, 2, ATT_HEAD_DIM)
    v_sample = jnp.stack(outs[4]).reshape(depth, Bs, Ls, ATT_HEADS, ATT_V_DIM)
    conv_sample = jnp.stack(outs[5])
    return (xp, xs, k_prompt, v_prompt, conv_prompt, k_sample, v_sample, conv_sample)
```

```python
import functools
import math

import jax
import jax.numpy as jnp
import numpy as np
from jax import lax
from jax.experimental import pallas as pl
from jax.experimental.pallas import tpu as pltpu

F32 = jnp.float32
BF16 = jnp.bfloat16

CHUNK = 64
ATT_HEADS = 4
ATT_HEAD_DIM = 64
ATT_V_DIM = 2 * ATT_HEAD_DIM
ROPE_THETA = 10000.0
PEER_TOPK = 16
LN_EPS = 1e-5

LANES = 128
SUBLANES = 8
VMEM_LIMIT_DEFAULT = 48 * 1024 * 1024
VMEM_LIMIT_TABLE = 56 * 1024 * 1024
VMEM_LIMIT_FUSED = 60 * 1024 * 1024

_NT = (((1,), (1,)), ((), ()))


def _cparams(n_axes, vmem=VMEM_LIMIT_DEFAULT):
    return pltpu.CompilerParams(dimension_semantics=("arbitrary",) * n_axes, vmem_limit_bytes=vmem)


def _ln(y, g, b):
    mu = jnp.mean(y, axis=-1, keepdims=True)
    d = y - mu
    var = jnp.mean(d * d, axis=-1, keepdims=True)
    return d * lax.rsqrt(var + LN_EPS) * g + b


def _ada_kernel(c_ref, w_ref, b_ref, o_ref):
    c = c_ref[...]
    s = (c * jax.nn.sigmoid(c)).astype(BF16)
    o_ref[...] = jnp.dot(s, w_ref[...], preferred_element_type=F32) + b_ref[...]


def ada_mod(c, w_bf, b):
    rows, d = c.shape
    n = w_bf.shape[1]
    tn = min(n, 1024)
    return pl.pallas_call(
        _ada_kernel,
        grid=(n // tn,),
        in_specs=[pl.BlockSpec((rows, d), lambda j: (0, 0)),
                  pl.BlockSpec((d, tn), lambda j: (0, j)),
                  pl.BlockSpec((1, tn), lambda j: (0, j))],
        out_specs=pl.BlockSpec((rows, tn), lambda j: (0, j)),
        out_shape=jax.ShapeDtypeStruct((rows, n), F32),
        compiler_params=_cparams(1),
        name="ada_mod",
    )(c, w_bf, b.reshape(1, n))


def _inproj_kernel(x_ref, sh_ref, sc_ref, w_ref, b_ref, cos_ref, sin_ref,
                   wvt_ref, bvt_ref, u_ref, k_ref, v_ref, qb_ref, kb_ref, vb_ref, *, conv_dim, qk_dim,
                   transposed_v):
    bb, tl, d = x_ref.shape
    h = x_ref[...] * (1.0 + sc_ref[...]) + sh_ref[...]
    hb = h.reshape(bb * tl, d).astype(BF16)
    z = jnp.dot(hb, w_ref[...], preferred_element_type=F32) + b_ref[...]
    c0, c1, c2, c3 = conv_dim, 2 * conv_dim, 2 * conv_dim + qk_dim, 2 * conv_dim + 2 * qk_dim
    u = z[:, :c0] * jax.nn.sigmoid(z[:, c0:c1])
    u_ref[...] = u.reshape(bb, tl, conv_dim)

    lane = lax.broadcasted_iota(jnp.int32, (bb * tl, qk_dim), 1)
    first_half = (lane % ATT_HEAD_DIM) < (ATT_HEAD_DIM // 2)
    cos = jnp.broadcast_to(cos_ref[...][None], (bb, tl, qk_dim)).reshape(bb * tl, qk_dim)
    sin = jnp.broadcast_to(sin_ref[...][None], (bb, tl, qk_dim)).reshape(bb * tl, qk_dim)

    def rope(t):
        swapped = jnp.where(first_half,
                            pltpu.roll(t, qk_dim - ATT_HEAD_DIM // 2, axis=1),
                            pltpu.roll(t, ATT_HEAD_DIM // 2, axis=1))
        return t * cos + swapped * sin

    q = rope(z[:, c1:c2]) * (ATT_HEAD_DIM ** -0.5)
    k = rope(z[:, c2:c3])
    v = z[:, c3:]
    k_ref[...] = k.reshape(bb, tl, qk_dim)
    v_ref[...] = v.reshape(bb, tl, v.shape[1])
    qb_ref[...] = q.astype(BF16).reshape(bb, tl, qk_dim)
    kb_ref[...] = k.astype(BF16).reshape(bb, tl, qk_dim)
    if transposed_v:
        vt = lax.dot_general(wvt_ref[...], hb, _NT, preferred_element_type=F32) + bvt_ref[...]
        vb_ref[0] = vt.astype(BF16)
    else:
        vb_ref[...] = v.astype(BF16).reshape(bb, tl, v.shape[1])


def in_proj(x, sh1, sc1, w_bf, b_in, cos, sin, bb, tl, conv_dim, qk_dim, att_dim, transposed_v):
    B, L, D = x.shape
    n = w_bf.shape[1]
    assert not transposed_v or bb == 1
    tok = lambda b, l: (b, l, 0)
    mod = lambda b, l: (b, 0, 0)
    const = lambda b, l: (0, 0)
    out_shapes = [jax.ShapeDtypeStruct((B, L, conv_dim), F32),
                  jax.ShapeDtypeStruct((B, L, qk_dim), F32),
                  jax.ShapeDtypeStruct((B, L, att_dim), F32),
                  jax.ShapeDtypeStruct((B, L, qk_dim), BF16),
                  jax.ShapeDtypeStruct((B, L, qk_dim), BF16),
                  jax.ShapeDtypeStruct((B, L, att_dim), BF16)]
    out_specs = [pl.BlockSpec((bb, tl, s.shape[2]), tok) for s in out_shapes]
    if transposed_v:
        out_shapes[5] = jax.ShapeDtypeStruct((B, att_dim, L), BF16)
        out_specs[5] = pl.BlockSpec((1, att_dim, tl), lambda b, l: (b, 0, l))
    wvt = jnp.transpose(w_bf[:, n - att_dim:])
    bvt = b_in[n - att_dim:].reshape(att_dim, 1)
    return pl.pallas_call(
        functools.partial(_inproj_kernel, conv_dim=conv_dim, qk_dim=qk_dim, transposed_v=transposed_v),
        grid=(B // bb, L // tl),
        in_specs=[pl.BlockSpec((bb, tl, D), tok),
                  pl.BlockSpec((bb, 1, D), mod),
                  pl.BlockSpec((bb, 1, D), mod),
                  pl.BlockSpec((D, n), const),
                  pl.BlockSpec((1, n), const),
                  pl.BlockSpec((tl, qk_dim), lambda b, l: (l, 0)),
                  pl.BlockSpec((tl, qk_dim), lambda b, l: (l, 0)),
                  pl.BlockSpec((att_dim, D), const),
                  pl.BlockSpec((att_dim, 1), const)],
        out_specs=out_specs,
        out_shape=out_shapes,
        compiler_params=_cparams(2),
        name="in_proj",
    )(x, sh1, sc1, w_bf, b_in.reshape(1, n), cos, sin, wvt, bvt)


HIST_PAD = 32


def _conv_kernel(u_ref, prev_ref, hist_ref, w_ref, b_ref, g_ref, beta_ref, o_ref, win_ref, *, width):
    i = pl.program_id(1)
    tl = u_ref.shape[1]

    @pl.when(i == 0)
    def _():
        win_ref[0:HIST_PAD, :] = hist_ref[0]

    @pl.when(i != 0)
    def _():
        win_ref[0:HIST_PAD, :] = prev_ref[0]

    win_ref[HIST_PAD:HIST_PAD + tl, :] = u_ref[0]
    off = HIST_PAD - (width - 1)
    acc = w_ref[0:1, :] * win_ref[off:off + tl, :]
    for j in range(1, width):
        acc = acc + w_ref[j:j + 1, :] * win_ref[off + j:off + j + tl, :]
    y = _ln(acc + b_ref[...], g_ref[...], beta_ref[...])
    o_ref[0] = (y * jax.nn.sigmoid(y)).astype(BF16)


def conv_ln(u, hist_pad, w_dw, b_dw, ln_g, ln_b, tl):
    B, L, C = u.shape
    width = w_dw.shape[0]
    steps = tl // HIST_PAD
    row = lambda b, i: (0, 0)
    return pl.pallas_call(
        functools.partial(_conv_kernel, width=width),
        grid=(B, L // tl),
        in_specs=[pl.BlockSpec((1, tl, C), lambda b, i: (b, i, 0)),
                  pl.BlockSpec((1, HIST_PAD, C), lambda b, i: (b, jnp.maximum(i * steps - 1, 0), 0)),
                  pl.BlockSpec((1, HIST_PAD, C), lambda b, i: (b, 0, 0)),
                  pl.BlockSpec((width, C), row),
                  pl.BlockSpec((1, C), row), pl.BlockSpec((1, C), row), pl.BlockSpec((1, C), row)],
        out_specs=pl.BlockSpec((1, tl, C), lambda b, i: (b, i, 0)),
        out_shape=jax.ShapeDtypeStruct((B, L, C), BF16),
        scratch_shapes=[pltpu.VMEM((HIST_PAD + tl, C), F32)],
        compiler_params=_cparams(2),
        name="conv_ln",
    )(u, u, hist_pad, w_dw, b_dw.reshape(1, C), ln_g.reshape(1, C), ln_b.reshape(1, C))


def _lambda_value(lam_ref, lam_init):
    s1 = jnp.sum(lam_ref[0:1, :] * lam_ref[1:2, :], axis=1, keepdims=True)
    s2 = jnp.sum(lam_ref[2:3, :] * lam_ref[3:4, :], axis=1, keepdims=True)
    return jnp.exp(s1) - jnp.exp(s2) + lam_init


def _split_maps(q):
    lane = lax.broadcasted_iota(jnp.int32, q.shape, 1)
    zero = jnp.zeros_like(q)
    return jnp.where(lane < ATT_HEAD_DIM, q, zero), jnp.where(lane >= ATT_HEAD_DIM, q, zero)


def _finish_heads(o0, o1, lam, g, lam_init):
    o = o0 - lam * o1
    o = o * lax.rsqrt(jnp.mean(o * o, axis=-1, keepdims=True) + LN_EPS) * g
    return o * (1.0 - lam_init)


def _attn_prompt_kernel(lam_ref, g_ref, q_ref, k_ref, vt_ref, o_ref, m_sc, l_sc, acc_sc, *, lam_init):
    i = pl.program_id(2)
    tq = q_ref.shape[1]
    q2 = jnp.concatenate(_split_maps(q_ref[0]), axis=0)
    m_sc[...] = jnp.full(m_sc.shape, -jnp.inf, F32)
    l_sc[...] = jnp.zeros(l_sc.shape, F32)
    acc_sc[...] = jnp.zeros(acc_sc.shape, F32)

    kpos = lax.broadcasted_iota(jnp.int32, (tq, 2 * tq), 0)
    qpos = lax.broadcasted_iota(jnp.int32, (tq, 2 * tq), 1) % tq
    allowed = (kpos // CHUNK) <= (qpos // CHUNK)

    def scores(j):
        kb = k_ref[0, pl.ds(pl.multiple_of(j * tq, tq), tq), :]
        return lax.dot_general(kb, q2, _NT, preferred_element_type=F32)

    def update(s, j):
        vt = vt_ref[0, :, pl.ds(pl.multiple_of(j * tq, tq), tq)]
        m_prev = m_sc[...]
        m_new = jnp.maximum(m_prev, jnp.max(s, axis=0, keepdims=True))
        alpha = jnp.exp(m_prev - m_new)
        pe = jnp.exp(s - m_new)
        l_sc[...] = alpha * l_sc[...] + jnp.sum(pe, axis=0, keepdims=True)
        acc_sc[...] = alpha * acc_sc[...] + jnp.dot(vt, pe.astype(BF16), preferred_element_type=F32)
        m_sc[...] = m_new

    def pair(jj, c):
        s_a, s_b = scores(2 * jj), scores(2 * jj + 1)
        update(s_a, 2 * jj)
        update(s_b, 2 * jj + 1)
        return c

    lax.fori_loop(0, i // 2, pair, 0)

    @pl.when(i % 2 == 1)
    def _():
        s_a, s_b = scores(i - 1), jnp.where(allowed, scores(i), -jnp.inf)
        update(s_a, i - 1)
        update(s_b, i)

    @pl.when(i % 2 == 0)
    def _():
        update(jnp.where(allowed, scores(i), -jnp.inf), i)

    lam = _lambda_value(lam_ref, lam_init)
    o = acc_sc[...] / l_sc[...]
    o = o[:, :tq] - lam * o[:, tq:]
    o = o * lax.rsqrt(jnp.mean(o * o, axis=0, keepdims=True) + LN_EPS) * g_ref[...]
    o_ref[0] = jnp.transpose(o * (1.0 - lam_init)).astype(BF16)


def attn_prompt(qb, kb, vt, lam_rows, subln_g, lam_init, tq):
    B, L, _ = qb.shape
    return pl.pallas_call(
        functools.partial(_attn_prompt_kernel, lam_init=lam_init),
        grid=(B, ATT_HEADS, L // tq),
        in_specs=[pl.BlockSpec((4, ATT_HEAD_DIM), lambda b, h, i: (0, 0)),
                  pl.BlockSpec((ATT_V_DIM, 1), lambda b, h, i: (0, 0)),
                  pl.BlockSpec((1, tq, LANES), lambda b, h, i: (b, i, h)),
                  pl.BlockSpec((1, L, LANES), lambda b, h, i: (b, 0, h)),
                  pl.BlockSpec((1, ATT_V_DIM, L), lambda b, h, i: (b, h, 0))],
        out_specs=pl.BlockSpec((1, tq, LANES), lambda b, h, i: (b, i, h)),
        out_shape=jax.ShapeDtypeStruct((B, L, ATT_HEADS * ATT_V_DIM), BF16),
        scratch_shapes=[pltpu.VMEM((1, 2 * tq), F32), pltpu.VMEM((1, 2 * tq), F32),
                        pltpu.VMEM((ATT_V_DIM, 2 * tq), F32)],
        compiler_params=_cparams(3),
        name="attn_prompt",
    )(lam_rows, subln_g.reshape(ATT_V_DIM, 1), qb, kb, vt)


def _attn_sample_kernel(lam_ref, g_ref, q_ref, ck_ref, cv_ref, kn_ref, vn_ref, o_ref, *, lam_init, past):
    ls = q_ref.shape[1]
    qs = _split_maps(q_ref[0])
    kp = ck_ref[0].astype(BF16)
    vp = cv_ref[0].astype(BF16)
    kn = kn_ref[0]
    vn = vn_ref[0]
    qpos = past + lax.broadcasted_iota(jnp.int32, (ls, ls), 0)
    kpos = past + lax.broadcasted_iota(jnp.int32, (ls, ls), 1)
    allowed = (kpos // CHUNK) <= (qpos // CHUNK)
    outs = []
    for p in range(2):
        sp = lax.dot_general(qs[p], kp, _NT, preferred_element_type=F32)
        sn = lax.dot_general(qs[p], kn, _NT, preferred_element_type=F32)
        sn = jnp.where(allowed, sn, -jnp.inf)
        m = jnp.maximum(jnp.max(sp, axis=1, keepdims=True), jnp.max(sn, axis=1, keepdims=True))
        pp = jnp.exp(sp - m)
        pn = jnp.exp(sn - m)
        denom = jnp.sum(pp, axis=1, keepdims=True) + jnp.sum(pn, axis=1, keepdims=True)
        acc = (jnp.dot(pp.astype(BF16), vp, preferred_element_type=F32)
               + jnp.dot(pn.astype(BF16), vn, preferred_element_type=F32))
        outs.append(acc / denom)
    lam = _lambda_value(lam_ref, lam_init)
    o_ref[0] = _finish_heads(outs[0], outs[1], lam, g_ref[...], lam_init).astype(BF16)


def attn_sample(qb, cache_k, cache_v, kb, vb, lam_rows, subln_g, lam_init):
    B, Ls, _ = qb.shape
    P = cache_k.shape[1]
    blk = lambda n: pl.BlockSpec((1, n, LANES), lambda b, h: (b, 0, h))
    return pl.pallas_call(
        functools.partial(_attn_sample_kernel, lam_init=lam_init, past=P),
        grid=(B, ATT_HEADS),
        in_specs=[pl.BlockSpec((4, ATT_HEAD_DIM), lambda b, h: (0, 0)),
                  pl.BlockSpec((1, ATT_V_DIM), lambda b, h: (0, 0)),
                  blk(Ls), blk(P), blk(P), blk(Ls), blk(Ls)],
        out_specs=blk(Ls),
        out_shape=jax.ShapeDtypeStruct((B, Ls, ATT_HEADS * ATT_V_DIM), BF16),
        compiler_params=_cparams(2),
        name="attn_sample",
    )(lam_rows, subln_g.reshape(1, ATT_V_DIM), qb, cache_k, cache_v, kb, vb)


def _outproj_kernel(conv_ref, att_ref, x_ref, g1_ref, sh2_ref, sc2_ref, wc_ref, wa_ref, b_ref,
                    lg_ref, lb_ref, x1_ref, hh_ref, h2b_ref, *, alpha):
    bb, tl, d = x_ref.shape
    conv = conv_ref[...].reshape(bb * tl, conv_ref.shape[2])
    att = att_ref[...].reshape(bb * tl, att_ref.shape[2])
    mm = (jnp.dot(conv, wc_ref[...], preferred_element_type=F32)
          + jnp.dot(att, wa_ref[...], preferred_element_type=F32) + b_ref[...])
    y = alpha * x_ref[...] + g1_ref[...] * mm.reshape(bb, tl, d)
    x1 = _ln(y, lg_ref[...], lb_ref[...])
    x1_ref[...] = x1
    h2 = x1 * (1.0 + sc2_ref[...]) + sh2_ref[...]
    h2b_ref[...] = h2.astype(BF16)
    hi, lo = _split_bf16(h2.reshape(bb * tl, d))
    bits = [lax.bitcast_convert_type(part.astype(F32), jnp.uint32) for part in (hi, lo)]
    rows = [b[:, c * LANES:(c + 1) * LANES] for b in bits for c in range(d // LANES)]
    for i in range(len(rows) // 2):
        hh_ref[:, i, :] = (rows[2 * i] >> 16) | rows[2 * i + 1]


def out_proj(conv_out, att, x, g1, sh2, sc2, wc_bf, wa_bf, b_out, ln_g, ln_b, alpha, bb, tl):
    B, L, D = x.shape
    tok = lambda b, l: (b, l, 0)
    mod = lambda b, l: (b, 0, 0)
    const = lambda b, l: (0, 0)
    return pl.pallas_call(
        functools.partial(_outproj_kernel, alpha=alpha),
        grid=(B // bb, L // tl),
        in_specs=[pl.BlockSpec((bb, tl, conv_out.shape[2]), tok),
                  pl.BlockSpec((bb, tl, att.shape[2]), tok),
                  pl.BlockSpec((bb, tl, D), tok),
                  pl.BlockSpec((bb, 1, D), mod), pl.BlockSpec((bb, 1, D), mod), pl.BlockSpec((bb, 1, D), mod),
                  pl.BlockSpec(wc_bf.shape, const), pl.BlockSpec(wa_bf.shape, const),
                  pl.BlockSpec((1, D), const), pl.BlockSpec((1, D), const), pl.BlockSpec((1, D), const)],
        out_specs=[pl.BlockSpec((bb, tl, D), tok),
                   pl.BlockSpec((bb * tl, D // LANES, LANES), lambda b, l: (b * (L // tl) + l, 0, 0)),
                   pl.BlockSpec((bb, tl, D), tok)],
        out_shape=[jax.ShapeDtypeStruct((B, L, D), F32),
                   jax.ShapeDtypeStruct((B * L, D // LANES, LANES), jnp.uint32),
                   jax.ShapeDtypeStruct((B, L, D), BF16)],
        compiler_params=_cparams(2),
        name="out_proj",
    )(conv_out, att, x, g1, sh2, sc2, wc_bf, wa_bf, b_out.reshape(1, D), ln_g.reshape(1, D), ln_b.reshape(1, D))


TOPK_STREAMS = 2


def _extract_top(vals, order, count, out, payload=None):
    big = jnp.float32(1e9)
    out_v, out_p = [], []
    for _ in range(count):
        m = jnp.max(vals, axis=0, keepdims=True)
        pos = jnp.min(jnp.where(vals == m, order, big), axis=0, keepdims=True)
        sel = order == pos
        out_v.append(m)
        out_p.append(pos if payload is None else jnp.max(jnp.where(sel, payload, -1.0), axis=0, keepdims=True))
        vals = jnp.where(sel, -jnp.inf, vals)
        yield
    out.append((jnp.concatenate(out_v, axis=0), jnp.concatenate(out_p, axis=0)))


def _interleave(streams):
    done = [0] * len(streams)
    total = max(n for _, n in streams)
    for tick in range(1, total + 1):
        for s, (gen, n) in enumerate(streams):
            while done[s] * total < tick * n:
                next(gen, None)
                done[s] += 1
    for gen, _ in streams:
        for _ in gen:
            pass


class _TopkWork:
    def __init__(self, h_ref, wq_ref, keys_ref, q_sc, e_sc, g_sc):
        self.keys_ref, self.q_sc, self.e_sc, self.g_sc = keys_ref, q_sc, e_sc, g_sc
        self.n_hp, self.n_keys, _ = keys_ref.shape
        tl = h_ref.shape[0]
        K = PEER_TOPK
        qp = jnp.dot(h_ref[...], wq_ref[...], preferred_element_type=F32)
        for hp in range(self.n_hp):
            q_sc[hp] = qp[:, hp * LANES:(hp + 1) * LANES].astype(BF16)
        self.key_iota = lax.broadcasted_iota(jnp.int32, (self.n_keys, LANES), 0).astype(F32)
        n_cand = K + (K - 1) * SUBLANES
        row = lax.broadcasted_iota(jnp.int32, (n_cand, LANES), 0)
        ci = jnp.where(row < K, 0, ((row - K) >> 3) + 1)
        cj = jnp.where(row < K, row, (row - K) & 7)
        limit = jnp.where(ci == 0, 16, jnp.where(ci == 1, 8, jnp.where(ci == 2, 5, jnp.where(
            ci == 3, 4, jnp.where(ci == 4, 3, jnp.where(ci <= 7, 2, 1))))))
        self.cand_ok = cj < limit
        self.cand_order = (ci * K + cj).astype(F32)
        n_chunks = tl // LANES
        self.streams = TOPK_STREAMS if n_chunks % TOPK_STREAMS == 0 else 1
        self.groups = n_chunks // self.streams
        self.n_steps = (self.n_hp // 2) * self.groups

    CHUNK_STEPS = 3 * PEER_TOPK

    def head_chunk(self, h, col):
        K = PEER_TOPK
        halves = []
        for p in range(2):
            qc = self.q_sc[2 * h + p, pl.ds(col, LANES), :]
            s = lax.dot_general(self.keys_ref[2 * h + p], qc, _NT, preferred_element_type=F32)
            yield from _extract_top(s, self.key_iota, K, halves)
        (sv0, si0), (sv1, si1) = halves
        cv = [sv0[0:1] + sv1]
        ce = [si0[0:1] * self.n_keys + si1]
        for r in range(1, K):
            cv.append(sv0[r:r + 1] + sv1[0:SUBLANES])
            ce.append(si0[r:r + 1] * self.n_keys + si1[0:SUBLANES])
        cand = jnp.where(self.cand_ok, jnp.concatenate(cv, axis=0), -jnp.inf)
        final = []
        yield from _extract_top(cand, self.cand_order, K, final, payload=jnp.concatenate(ce, axis=0))
        fv, fe = final[0]
        ex = jnp.exp(fv - fv[0:1])
        gate = ex / jnp.sum(ex, axis=0, keepdims=True)
        off = pl.multiple_of(h * K, K)
        self.e_sc[pl.ds(off, K), pl.ds(col, LANES)] = fe
        self.g_sc[pl.ds(off, K), pl.ds(col, LANES)] = gate

    def step_streams(self, i):
        h = i // self.groups
        return [(self.head_chunk(h, pl.multiple_of(((i % self.groups) * self.streams + c) * LANES, LANES)),
                 self.CHUNK_STEPS) for c in range(self.streams)]

    def finish(self, e_ref, g_ref):
        e_ref[...] = jnp.transpose(self.e_sc[...]).astype(jnp.int32) * TABLE_ROWS_PER_EXPERT
        g_ref[...] = jnp.transpose(self.g_sc[...])


def _topk_kernel(h_ref, wq_ref, keys_ref, e_ref, g_ref, q_sc, e_sc, g_sc):
    work = _TopkWork(h_ref, wq_ref, keys_ref, q_sc, e_sc, g_sc)

    def step(i, carry):
        _interleave(work.step_streams(i))
        return carry

    lax.fori_loop(0, work.n_steps, step, 0)
    work.finish(e_ref, g_ref)


def _topk_specs(D, wq_bf, keys_bf, tl, blk0):
    n_hp, n_keys, half = keys_bf.shape
    slots = (n_hp // 2) * PEER_TOPK
    assert half == LANES and n_keys == LANES and PEER_TOPK * PEER_TOPK <= 256
    in_specs = [pl.BlockSpec((tl, D), lambda i: (i + blk0, 0)),
                pl.BlockSpec(wq_bf.shape, lambda i: (0, 0), pipeline_mode=pl.Buffered(1)),
                pl.BlockSpec(keys_bf.shape, lambda i: (0, 0, 0), pipeline_mode=pl.Buffered(1))]
    out_specs = [pl.BlockSpec((tl, slots), lambda i: (i, 0))] * 2
    scratch = [pltpu.VMEM((n_hp, tl, LANES), BF16), pltpu.VMEM((slots, tl), F32), pltpu.VMEM((slots, tl), F32)]
    return slots, in_specs, out_specs, scratch


def peer_topk(h2b, wq_bf, keys_bf, tl, blk0=0, nblk=None):
    T, D = h2b.shape
    nblk = T // tl if nblk is None else nblk
    slots, in_specs, out_specs, scratch = _topk_specs(D, wq_bf, keys_bf, tl, blk0)
    return pl.pallas_call(
        _topk_kernel,
        grid=(nblk,),
        in_specs=in_specs,
        out_specs=out_specs,
        out_shape=[jax.ShapeDtypeStruct((nblk * tl, slots), jnp.int32),
                   jax.ShapeDtypeStruct((nblk * tl, slots), F32)],
        scratch_shapes=scratch,
        compiler_params=_cparams(1),
        name="peer_topk",
    )(h2b, wq_bf, keys_bf)


TABLE_ROWS_PER_EXPERT = 4
PEER_TOKEN_BLOCK = 128
PEER_TOKEN_UNROLL = 8


PACK_EXPERT_BLOCK = 512


def _pack_kernel(t_ref, o_ref):
    n = t_ref.shape[0]
    bits = lax.bitcast_convert_type(t_ref[...].astype(BF16).astype(F32), jnp.uint32)
    for i in range(TABLE_ROWS_PER_EXPERT):
        lo = bits[:, (2 * i) * LANES:(2 * i + 1) * LANES]
        hi = bits[:, (2 * i + 1) * LANES:(2 * i + 2) * LANES]
        o_ref[pl.ds(i, n, stride=TABLE_ROWS_PER_EXPERT), :] = (lo >> 16) | hi


def pack_table(tab):
    n, d = tab.shape
    assert d == 2 * LANES * TABLE_ROWS_PER_EXPERT
    tn = min(n, PACK_EXPERT_BLOCK)
    return pl.pallas_call(
        _pack_kernel,
        grid=(n // tn,),
        in_specs=[pl.BlockSpec((tn, d), lambda i: (i, 0))],
        out_specs=pl.BlockSpec((tn * TABLE_ROWS_PER_EXPERT, LANES), lambda i: (i, 0)),
        out_shape=jax.ShapeDtypeStruct((n * TABLE_ROWS_PER_EXPERT, LANES), jnp.uint32),
        compiler_params=_cparams(1),
        name="pack_table",
    )(tab)


def _gather_rows(e_ref, tab_ref, base, slots):
    rows = []
    for j in range(slots):
        start = pl.multiple_of(e_ref[base + j], TABLE_ROWS_PER_EXPERT)
        rows.append(pltpu.bitcast(tab_ref[pl.ds(start, TABLE_ROWS_PER_EXPERT), :], BF16))
    return jnp.concatenate(rows, axis=0)


def _split_bf16(x):
    hi = x.astype(BF16)
    return hi, (x - hi.astype(F32)).astype(BF16)


def _chunk_mask(slots):
    lane = lax.broadcasted_iota(jnp.int32, (SUBLANES, slots * SUBLANES), 1)
    sub = lax.broadcasted_iota(jnp.int32, (SUBLANES, slots * SUBLANES), 0)
    return (lane & (SUBLANES - 1)) == sub


def _slot_spread(slots):
    r = lax.broadcasted_iota(jnp.int32, (slots, slots * SUBLANES), 0)
    c = lax.broadcasted_iota(jnp.int32, (slots, slots * SUBLANES), 1)
    return jnp.where((c >> 3) == r, 1.0, 0.0).astype(BF16)


def _peer_a_tokens(count, t0, z0, e_ref, hh_ref, tab_ref, z_sc, mask, slots):
    for u in range(count):
        t = t0 + u
        g = _gather_rows(e_ref, tab_ref, pl.multiple_of(t * slots, slots), slots)
        hh = pltpu.bitcast(hh_ref[t], BF16)
        z = lax.dot_general(hh, g, _NT, preferred_element_type=F32)
        z8 = z[0:SUBLANES] + z[SUBLANES:2 * SUBLANES]
        z_sc[pl.ds(pl.multiple_of((z0 + u) * SUBLANES, SUBLANES), SUBLANES), :] = jnp.where(mask, z8, 0.0)
        yield


def _peer_a_weights(z_sc, gate, slots):
    tb = z_sc.shape[0] // SUBLANES
    zhi, zlo = _split_bf16(z_sc[...])
    spread = _slot_spread(slots)
    a8 = (lax.dot_general(zhi, spread, _NT, preferred_element_type=F32)
          + lax.dot_general(zlo, spread, _NT, preferred_element_type=F32))
    a = jnp.sum(a8.reshape(tb, SUBLANES, slots), axis=1)
    return 0.5 * a * (1.0 + lax.erf(a * (2.0 ** -0.5))) * gate


def _peer_a_kernel(e_ref, hh_ref, g_ref, tab_ref, w_ref, z_sc, *, slots):
    tb = hh_ref.shape[0]
    mask = _chunk_mask(slots)

    def tokens(i, carry):
        t0 = i * PEER_TOKEN_UNROLL
        for _ in _peer_a_tokens(PEER_TOKEN_UNROLL, t0, t0, e_ref, hh_ref, tab_ref, z_sc, mask, slots):
            pass
        return carry

    lax.fori_loop(0, tb // PEER_TOKEN_UNROLL, tokens, 0)
    w_ref[...] = _peer_a_weights(z_sc, g_ref[...], slots)


def _peer_a_specs(tab_u32, slots, tb, blk0):
    in_specs = [pl.BlockSpec((tb * slots,), lambda i: (i,), memory_space=pltpu.SMEM),
                pl.BlockSpec((tb, SUBLANES, LANES), lambda i: (i + blk0, 0, 0)),
                pl.BlockSpec((tb, slots), lambda i: (i, 0)),
                pl.BlockSpec(tab_u32.shape, lambda i: (0, 0), pipeline_mode=pl.Buffered(1))]
    scratch = [pltpu.VMEM((PEER_TOKEN_BLOCK * SUBLANES, slots * SUBLANES), F32)]
    return in_specs, pl.BlockSpec((tb, slots), lambda i: (i, 0)), scratch


def peer_a(e_flat, hh, gate, tab_u32, slots, blk0=0):
    T = gate.shape[0]
    tb = PEER_TOKEN_BLOCK
    in_specs, out_spec, scratch = _peer_a_specs(tab_u32, slots, tb, blk0)
    return pl.pallas_call(
        functools.partial(_peer_a_kernel, slots=slots),
        grid=(T // tb,),
        in_specs=in_specs,
        out_specs=out_spec,
        out_shape=jax.ShapeDtypeStruct((T, slots), F32),
        scratch_shapes=scratch,
        compiler_params=_cparams(1, VMEM_LIMIT_TABLE),
        name="peer_a",
    )(e_flat, hh, gate, tab_u32)


def _topk_a_kernel(e_ref, hh_ref, gate_ref, tab_ref, h_ref, wq_ref, keys_ref, w_ref, e_out_ref, g_out_ref,
                   z_sc, q_sc, e_sc, g_sc, *, slots):
    tl = hh_ref.shape[0]
    work = _TopkWork(h_ref, wq_ref, keys_ref, q_sc, e_sc, g_sc)
    mask = _chunk_mask(slots)
    per_step = tl // work.n_steps
    flush = PEER_TOKEN_BLOCK // per_step
    assert per_step % PEER_TOKEN_UNROLL == 0 and flush * per_step == PEER_TOKEN_BLOCK

    def step(i, carry):
        gathers = _peer_a_tokens(per_step, i * per_step, (i % flush) * per_step, e_ref, hh_ref, tab_ref, z_sc,
                                 mask, slots)
        _interleave(work.step_streams(i) + [(gathers, per_step)])

        @pl.when(i % flush == flush - 1)
        def _():
            rows = pl.ds(pl.multiple_of((i // flush) * PEER_TOKEN_BLOCK, PEER_TOKEN_BLOCK), PEER_TOKEN_BLOCK)
            w_ref[rows, :] = _peer_a_weights(z_sc, gate_ref[rows, :], slots)

        return carry

    lax.fori_loop(0, work.n_steps, step, 0)
    work.finish(e_out_ref, g_out_ref)


def peer_topk_a(e_flat, hh, gate, tab_u32, h2b, wq_bf, keys_bf, tl, a_blk0, k_blk0):
    T, D = gate.shape[0], h2b.shape[1]
    slots, k_in, k_out, k_scratch = _topk_specs(D, wq_bf, keys_bf, tl, k_blk0)
    a_in, a_out, a_scratch = _peer_a_specs(tab_u32, slots, tl, a_blk0)
    return pl.pallas_call(
        functools.partial(_topk_a_kernel, slots=slots),
        grid=(T // tl,),
        in_specs=a_in + k_in,
        out_specs=[a_out] + k_out,
        out_shape=[jax.ShapeDtypeStruct((T, slots), F32), jax.ShapeDtypeStruct((T, slots), jnp.int32),
                   jax.ShapeDtypeStruct((T, slots), F32)],
        scratch_shapes=a_scratch + k_scratch,
        compiler_params=_cparams(1, VMEM_LIMIT_FUSED),
        name="peer_topk_a",
    )(e_flat, hh, gate, tab_u32, h2b, wq_bf, keys_bf)


def _peer_b_kernel(e_ref, w_ref, tab_ref, o_ref, rep_sc, *, slots):
    tb = o_ref.shape[0]
    spread = _slot_spread(slots)
    whi, wlo = _split_bf16(w_ref[...])
    rep_sc[0] = jnp.dot(whi, spread, preferred_element_type=F32)
    rep_sc[1] = jnp.dot(wlo, spread, preferred_element_type=F32)
    mask = _chunk_mask(slots)
    wide = (SUBLANES, slots * SUBLANES)

    def tokens(i, carry):
        for u in range(PEER_TOKEN_UNROLL):
            t = i * PEER_TOKEN_UNROLL + u
            g = _gather_rows(e_ref, tab_ref, pl.multiple_of(t * slots, slots), slots)
            parts = [jnp.where(mask, jnp.broadcast_to(rep_sc[p, pl.ds(t, 1), :], wide), 0.0).astype(BF16)
                     for p in range(2)]
            r = jnp.dot(jnp.concatenate(parts, axis=0), g, preferred_element_type=F32)
            o_ref[t] = r[0:SUBLANES] + r[SUBLANES:2 * SUBLANES]
        return carry

    lax.fori_loop(0, tb // PEER_TOKEN_UNROLL, tokens, 0)


def peer_b(e_flat, w, tab_u32, slots):
    T = w.shape[0]
    tb = PEER_TOKEN_BLOCK
    return pl.pallas_call(
        functools.partial(_peer_b_kernel, slots=slots),
        grid=(T // tb,),
        in_specs=[pl.BlockSpec((tb * slots,), lambda i: (i,), memory_space=pltpu.SMEM),
                  pl.BlockSpec((tb, slots), lambda i: (i, 0)),
                  pl.BlockSpec(tab_u32.shape, lambda i: (0, 0), pipeline_mode=pl.Buffered(1))],
        out_specs=pl.BlockSpec((tb, SUBLANES, LANES), lambda i: (i, 0, 0)),
        out_shape=jax.ShapeDtypeStruct((T, SUBLANES, LANES), F32),
        scratch_shapes=[pltpu.VMEM((2, tb, slots * SUBLANES), F32)],
        compiler_params=_cparams(1, VMEM_LIMIT_TABLE),
        name="peer_b",
    )(e_flat, w, tab_u32)


def _final_kernel(x_ref, ff_ref, g2_ref, lg_ref, lb_ref, o_ref, *, alpha):
    bb, tl, d = x_ref.shape
    ff = jnp.concatenate([ff_ref[:, c, :] for c in range(d // LANES)], axis=1)
    y = alpha * x_ref[...] + g2_ref[...] * ff.reshape(bb, tl, d)
    o_ref[...] = _ln(y, lg_ref[...], lb_ref[...])


def final_ln(x1, ff, g2, ln_g, ln_b, alpha, bb, tl):
    B, L, D = x1.shape
    tok = lambda b, l: (b, l, 0)
    const = lambda b, l: (0, 0)
    return pl.pallas_call(
        functools.partial(_final_kernel, alpha=alpha),
        grid=(B // bb, L // tl),
        in_specs=[pl.BlockSpec((bb, tl, D), tok),
                  pl.BlockSpec((bb * tl, D // LANES, LANES), lambda b, l: (b * (L // tl) + l, 0, 0)),
                  pl.BlockSpec((bb, 1, D), lambda b, l: (b, 0, 0)),
                  pl.BlockSpec((1, D), const), pl.BlockSpec((1, D), const)],
        out_specs=pl.BlockSpec((bb, tl, D), tok),
        out_shape=jax.ShapeDtypeStruct((B, L, D), F32),
        compiler_params=_cparams(2),
        name="final_ln",
    )(x1, ff, g2, ln_g.reshape(1, D), ln_b.reshape(1, D))


def _rope_tables(past, length, n_groups):
    half = ATT_HEAD_DIM // 2
    inv = 1.0 / (ROPE_THETA ** (jnp.arange(half, dtype=F32) / half))
    ang = (past + jnp.arange(length)).astype(F32)[:, None] * inv[None, :]
    cos = jnp.cos(ang)
    sin = jnp.sin(ang)
    cos_t = jnp.tile(jnp.concatenate([cos, cos], axis=-1), (1, n_groups))
    sin_t = jnp.tile(jnp.concatenate([-sin, sin], axis=-1), (1, n_groups))
    return cos_t, sin_t


def _token_tiles(B, L, target):
    if L >= target:
        return 1, target
    bb = max(1, min(B, target // L))
    while B % bb:
        bb -= 1
    return bb, L


TOPK_TOKEN_BLOCK = 512
PEER_MAX_SPLIT = 4


def _peer_select_and_weigh(h2b, hh, wts, slots):
    T = h2b.shape[0]
    tl = min(T, TOPK_TOKEN_BLOCK)
    wq, keys, tab = wts["w_query"], wts["sub_keys"], wts["u_tab"]
    split = max(s for s in range(1, PEER_MAX_SPLIT + 1) if T % (s * tl) == 0)
    if split == 1 or tl % PEER_TOKEN_BLOCK:
        e, gate = peer_topk(h2b, wq, keys, tl)
        return e, peer_a(e.reshape(T * slots), hh, gate, tab, slots)
    nblk = T // (split * tl)
    chunk = nblk * tl
    e_c, gate_c = peer_topk(h2b, wq, keys, tl, 0, nblk)
    es, ws = [e_c], []
    for c in range(1, split):
        w_c, e_n, gate_n = peer_topk_a(e_c.reshape(chunk * slots), hh, gate_c, tab, h2b, wq, keys, tl,
                                       (c - 1) * nblk, c * nblk)
        ws.append(w_c)
        es.append(e_n)
        e_c, gate_c = e_n, gate_n
    ws.append(peer_a(e_c.reshape(chunk * slots), hh, gate_c, tab, slots, (split - 1) * chunk // PEER_TOKEN_BLOCK))
    return jnp.concatenate(es, axis=0), jnp.concatenate(ws, axis=0)


def _layer(x, mod, conv_hist, k_past, v_past, wts, lam_init, alpha):
    B, L, D = x.shape
    conv_dim = wts["w_dw"].shape[1]
    qk_dim = ATT_HEADS * 2 * ATT_HEAD_DIM
    att_dim = ATT_HEADS * ATT_V_DIM
    width = wts["w_dw"].shape[0]
    assert L >= width - 1 and width - 1 <= HIST_PAD
    sh1, sc1, g1, sh2, sc2, g2 = [m[:, None, :] for m in jnp.split(mod, 6, axis=-1)]
    past = 0 if k_past is None else k_past.shape[1]
    cos, sin = _rope_tables(past, L, qk_dim // ATT_HEAD_DIM)
    bb, tl = _token_tiles(B, L, 512)

    u, k, v, qb, kb, vb = in_proj(x, sh1, sc1, wts["w_in"], wts["b_in"], cos, sin, bb, tl,
                                  conv_dim, qk_dim, att_dim, transposed_v=k_past is None)
    hist_pad = jnp.pad(conv_hist, ((0, 0), (HIST_PAD - (width - 1), 0), (0, 0)))
    conv_out = conv_ln(u, hist_pad, wts["w_dw"], wts["b_dw"], wts["conv_ln_g"], wts["conv_ln_b"], min(L, 256))
    conv_state = u[:, L - (width - 1):, :]

    if k_past is None:
        att = attn_prompt(qb, kb, vb, wts["lam_rows"], wts["subln_g"], lam_init, min(L, 256))
    else:
        att = attn_sample(qb, k_past.reshape(B, past, qk_dim), v_past.reshape(B, past, att_dim), kb, vb,
                          wts["lam_rows"], wts["subln_g"], lam_init)

    x1, hh, h2b = out_proj(conv_out, att, x, g1, sh2, sc2, wts["w_out_conv"], wts["w_out_att"], wts["b_out"],
                           wts["ln1_g"], wts["ln1_b"], alpha, bb, tl)

    T = B * L
    slots = (wts["sub_keys"].shape[0] // 2) * PEER_TOPK
    e, w = _peer_select_and_weigh(h2b.reshape(T, D), hh, wts, slots)
    ff = peer_b(e.reshape(T * slots), w, wts["v_tab"], slots)
    out = final_ln(x1, ff, g2, wts["ln2_g"], wts["ln2_b"], alpha, bb, tl)
    return out, conv_state, k, v


def kernel(x_prompt, x_sample, cache_k, cache_v, cache_conv, c_prompt, c_sample, w_ada, b_ada, w_in, b_in, w_dw, b_dw, conv_ln_g, conv_ln_b, lam_q1, lam_k1, lam_q2, lam_k2, subln_g, w_out, b_out, ln1_g, ln1_b, w_query, sub_keys, u_tab, v_tab, ln2_g, ln2_b):
    depth = w_ada.shape[0]
    D = x_prompt.shape[-1]
    assert D == SUBLANES * LANES
    alpha = (2 * depth) ** 0.25
    Bp, Bs = c_prompt.shape[0], c_sample.shape[0]
    xp, xs = x_prompt, x_sample
    outs = [[] for _ in range(6)]
    for l in range(depth):
        lam_init = 0.8 - 0.6 * math.exp(-0.3 * l)
        conv_dim = w_dw.shape[2]
        n_exp = u_tab.shape[1]
        wts = {
            "w_in": w_in[l].astype(BF16), "b_in": b_in[l], "w_dw": w_dw[l], "b_dw": b_dw[l],
            "conv_ln_g": conv_ln_g[l], "conv_ln_b": conv_ln_b[l],
            "lam_rows": jnp.stack([lam_q1[l], lam_k1[l], lam_q2[l], lam_k2[l]]).astype(F32),
            "subln_g": subln_g[l],
            "w_out_conv": w_out[l, :conv_dim].astype(BF16), "w_out_att": w_out[l, conv_dim:].astype(BF16),
            "b_out": b_out[l], "ln1_g": ln1_g[l], "ln1_b": ln1_b[l],
            "w_query": w_query[l].astype(BF16),
            "sub_keys": sub_keys[l].astype(BF16).reshape(-1, sub_keys.shape[3], sub_keys.shape[4]),
            "u_tab": pack_table(u_tab[l]), "v_tab": pack_table(v_tab[l]),
            "ln2_g": ln2_g[l], "ln2_b": ln2_b[l],
        }
        c_all = jnp.concatenate([c_prompt, c_sample], axis=0)
        pad = (-c_all.shape[0]) % 16
        c_all = jnp.pad(c_all, ((0, pad), (0, 0)))
        mod = ada_mod(c_all, w_ada[l].astype(BF16), b_ada[l])
        hist0 = jnp.zeros((Bp, w_dw.shape[1] - 1, conv_dim), xp.dtype)
        xp, cp, kp, vp = _layer(xp, mod[:Bp], hist0, None, None, wts, lam_init, alpha)
        xs, cs, kn, vn = _layer(xs, mod[Bp:Bp + Bs], cache_conv[l], cache_k[l], cache_v[l], wts, lam_init, alpha)
        for lst, val in zip(outs, (kp, vp, cp, kn, vn, cs)):
            lst.append(val)
    Lp, Ls = x_prompt.shape[1], x_sample.shape[1]
    k_prompt = jnp.stack(outs[0]).reshape(depth, Bp, Lp, ATT_HEADS, 2, ATT_HEAD_DIM)
    v_prompt = jnp.stack(outs[1]).reshape(depth, Bp, Lp, ATT_HEADS, ATT_V_DIM)
    conv_prompt = jnp.stack(outs[2])
    k_sample = jnp.stack(outs[3]).reshape(depth, Bs, Ls, ATT_HEADS, 2, ATT_HEAD_DIM)
    v_sample = jnp.stack(outs[4]).reshape(depth, Bs, Ls, ATT_HEADS, ATT_V_DIM)
    conv_sample = jnp.stack(outs[5])
    return (xp, xs, k_prompt, v_prompt, conv_prompt, k_sample, v_sample, conv_sample)
```

```python
import functools
import math

import jax
import jax.numpy as jnp
import numpy as np
from jax import lax
from jax.experimental import pallas as pl
from jax.experimental.pallas import tpu as pltpu

F32 = jnp.float32
BF16 = jnp.bfloat16

CHUNK = 64
ATT_HEADS = 4
ATT_HEAD_DIM = 64
ATT_V_DIM = 2 * ATT_HEAD_DIM
ROPE_THETA = 10000.0
PEER_TOPK = 16
LN_EPS = 1e-5

LANES = 128
SUBLANES = 8
VMEM_LIMIT_DEFAULT = 48 * 1024 * 1024
VMEM_LIMIT_TABLE = 56 * 1024 * 1024
VMEM_LIMIT_FUSED = 60 * 1024 * 1024

_NT = (((1,), (1,)), ((), ()))


def _cparams(n_axes, vmem=VMEM_LIMIT_DEFAULT):
    return pltpu.CompilerParams(dimension_semantics=("arbitrary",) * n_axes, vmem_limit_bytes=vmem)


def _ln(y, g, b):
    mu = jnp.mean(y, axis=-1, keepdims=True)
    d = y - mu
    var = jnp.mean(d * d, axis=-1, keepdims=True)
    return d * lax.rsqrt(var + LN_EPS) * g + b


def _ada_kernel(c_ref, w_ref, b_ref, o_ref):
    c = c_ref[...]
    s = (c * jax.nn.sigmoid(c)).astype(BF16)
    o_ref[...] = jnp.dot(s, w_ref[...], preferred_element_type=F32) + b_ref[...]


def ada_mod(c, w_bf, b):
    rows, d = c.shape
    n = w_bf.shape[1]
    tn = min(n, 1024)
    return pl.pallas_call(
        _ada_kernel,
        grid=(n // tn,),
        in_specs=[pl.BlockSpec((rows, d), lambda j: (0, 0)),
                  pl.BlockSpec((d, tn), lambda j: (0, j)),
                  pl.BlockSpec((1, tn), lambda j: (0, j))],
        out_specs=pl.BlockSpec((rows, tn), lambda j: (0, j)),
        out_shape=jax.ShapeDtypeStruct((rows, n), F32),
        compiler_params=_cparams(1),
        name="ada_mod",
    )(c, w_bf, b.reshape(1, n))


def _inproj_kernel(x_ref, sh_ref, sc_ref, w_ref, b_ref, cos_ref, sin_ref,
                   wvt_ref, bvt_ref, u_ref, k_ref, v_ref, qb_ref, kb_ref, vb_ref, *, conv_dim, qk_dim,
                   transposed_v):
    bb, tl, d = x_ref.shape
    h = x_ref[...] * (1.0 + sc_ref[...]) + sh_ref[...]
    hb = h.reshape(bb * tl, d).astype(BF16)
    z = jnp.dot(hb, w_ref[...], preferred_element_type=F32) + b_ref[...]
    c0, c1, c2, c3 = conv_dim, 2 * conv_dim, 2 * conv_dim + qk_dim, 2 * conv_dim + 2 * qk_dim
    u = z[:, :c0] * jax.nn.sigmoid(z[:, c0:c1])
    u_ref[...] = u.reshape(bb, tl, conv_dim)

    lane = lax.broadcasted_iota(jnp.int32, (bb * tl, qk_dim), 1)
    first_half = (lane % ATT_HEAD_DIM) < (ATT_HEAD_DIM // 2)
    cos = jnp.broadcast_to(cos_ref[...][None], (bb, tl, qk_dim)).reshape(bb * tl, qk_dim)
    sin = jnp.broadcast_to(sin_ref[...][None], (bb, tl, qk_dim)).reshape(bb * tl, qk_dim)

    def rope(t):
        swapped = jnp.where(first_half,
                            pltpu.roll(t, qk_dim - ATT_HEAD_DIM // 2, axis=1),
                            pltpu.roll(t, ATT_HEAD_DIM // 2, axis=1))
        return t * cos + swapped * sin

    q = rope(z[:, c1:c2]) * (ATT_HEAD_DIM ** -0.5)
    k = rope(z[:, c2:c3])
    v = z[:, c3:]
    k_ref[...] = k.reshape(bb, tl, qk_dim)
    v_ref[...] = v.reshape(bb, tl, v.shape[1])
    qb_ref[...] = q.astype(BF16).reshape(bb, tl, qk_dim)
    kb_ref[...] = k.astype(BF16).reshape(bb, tl, qk_dim)
    if transposed_v:
        vt = lax.dot_general(wvt_ref[...], hb, _NT, preferred_element_type=F32) + bvt_ref[...]
        vb_ref[0] = vt.astype(BF16)
    else:
        vb_ref[...] = v.astype(BF16).reshape(bb, tl, v.shape[1])


def in_proj(x, sh1, sc1, w_bf, b_in, cos, sin, bb, tl, conv_dim, qk_dim, att_dim, transposed_v):
    B, L, D = x.shape
    n = w_bf.shape[1]
    assert not transposed_v or bb == 1
    tok = lambda b, l: (b, l, 0)
    mod = lambda b, l: (b, 0, 0)
    const = lambda b, l: (0, 0)
    out_shapes = [jax.ShapeDtypeStruct((B, L, conv_dim), F32),
                  jax.ShapeDtypeStruct((B, L, qk_dim), F32),
                  jax.ShapeDtypeStruct((B, L, att_dim), F32),
                  jax.ShapeDtypeStruct((B, L, qk_dim), BF16),
                  jax.ShapeDtypeStruct((B, L, qk_dim), BF16),
                  jax.ShapeDtypeStruct((B, L, att_dim), BF16)]
    out_specs = [pl.BlockSpec((bb, tl, s.shape[2]), tok) for s in out_shapes]
    if transposed_v:
        out_shapes[5] = jax.ShapeDtypeStruct((B, att_dim, L), BF16)
        out_specs[5] = pl.BlockSpec((1, att_dim, tl), lambda b, l: (b, 0, l))
    wvt = jnp.transpose(w_bf[:, n - att_dim:])
    bvt = b_in[n - att_dim:].reshape(att_dim, 1)
    return pl.pallas_call(
        functools.partial(_inproj_kernel, conv_dim=conv_dim, qk_dim=qk_dim, transposed_v=transposed_v),
        grid=(B // bb, L // tl),
        in_specs=[pl.BlockSpec((bb, tl, D), tok),
                  pl.BlockSpec((bb, 1, D), mod),
                  pl.BlockSpec((bb, 1, D), mod),
                  pl.BlockSpec((D, n), const),
                  pl.BlockSpec((1, n), const),
                  pl.BlockSpec((tl, qk_dim), lambda b, l: (l, 0)),
                  pl.BlockSpec((tl, qk_dim), lambda b, l: (l, 0)),
                  pl.BlockSpec((att_dim, D), const),
                  pl.BlockSpec((att_dim, 1), const)],
        out_specs=out_specs,
        out_shape=out_shapes,
        compiler_params=_cparams(2),
        name="in_proj",
    )(x, sh1, sc1, w_bf, b_in.reshape(1, n), cos, sin, wvt, bvt)


HIST_PAD = 32


def _conv_kernel(u_ref, prev_ref, hist_ref, w_ref, b_ref, g_ref, beta_ref, o_ref, win_ref, *, width):
    i = pl.program_id(1)
    tl = u_ref.shape[1]

    @pl.when(i == 0)
    def _():
        win_ref[0:HIST_PAD, :] = hist_ref[0]

    @pl.when(i != 0)
    def _():
        win_ref[0:HIST_PAD, :] = prev_ref[0]

    win_ref[HIST_PAD:HIST_PAD + tl, :] = u_ref[0]
    off = HIST_PAD - (width - 1)
    acc = w_ref[0:1, :] * win_ref[off:off + tl, :]
    for j in range(1, width):
        acc = acc + w_ref[j:j + 1, :] * win_ref[off + j:off + j + tl, :]
    y = _ln(acc + b_ref[...], g_ref[...], beta_ref[...])
    o_ref[0] = (y * jax.nn.sigmoid(y)).astype(BF16)


def conv_ln(u, hist_pad, w_dw, b_dw, ln_g, ln_b, tl):
    B, L, C = u.shape
    width = w_dw.shape[0]
    steps = tl // HIST_PAD
    row = lambda b, i: (0, 0)
    return pl.pallas_call(
        functools.partial(_conv_kernel, width=width),
        grid=(B, L // tl),
        in_specs=[pl.BlockSpec((1, tl, C), lambda b, i: (b, i, 0)),
                  pl.BlockSpec((1, HIST_PAD, C), lambda b, i: (b, jnp.maximum(i * steps - 1, 0), 0)),
                  pl.BlockSpec((1, HIST_PAD, C), lambda b, i: (b, 0, 0)),
                  pl.BlockSpec((width, C), row),
                  pl.BlockSpec((1, C), row), pl.BlockSpec((1, C), row), pl.BlockSpec((1, C), row)],
        out_specs=pl.BlockSpec((1, tl, C), lambda b, i: (b, i, 0)),
        out_shape=jax.ShapeDtypeStruct((B, L, C), BF16),
        scratch_shapes=[pltpu.VMEM((HIST_PAD + tl, C), F32)],
        compiler_params=_cparams(2),
        name="conv_ln",
    )(u, u, hist_pad, w_dw, b_dw.reshape(1, C), ln_g.reshape(1, C), ln_b.reshape(1, C))


def _lambda_value(lam_ref, lam_init):
    s1 = jnp.sum(lam_ref[0:1, :] * lam_ref[1:2, :], axis=1, keepdims=True)
    s2 = jnp.sum(lam_ref[2:3, :] * lam_ref[3:4, :], axis=1, keepdims=True)
    return jnp.exp(s1) - jnp.exp(s2) + lam_init


def _split_maps(q):
    lane = lax.broadcasted_iota(jnp.int32, q.shape, 1)
    zero = jnp.zeros_like(q)
    return jnp.where(lane < ATT_HEAD_DIM, q, zero), jnp.where(lane >= ATT_HEAD_DIM, q, zero)


def _finish_heads(o0, o1, lam, g, lam_init):
    o = o0 - lam * o1
    o = o * lax.rsqrt(jnp.mean(o * o, axis=-1, keepdims=True) + LN_EPS) * g
    return o * (1.0 - lam_init)


def _attn_prompt_kernel(lam_ref, g_ref, q_ref, k_ref, vt_ref, o_ref, m_sc, l_sc, acc_sc, *, lam_init):
    i = pl.program_id(2)
    tq = q_ref.shape[1]
    q2 = jnp.concatenate(_split_maps(q_ref[0]), axis=0)
    m_sc[...] = jnp.full(m_sc.shape, -jnp.inf, F32)
    l_sc[...] = jnp.zeros(l_sc.shape, F32)
    acc_sc[...] = jnp.zeros(acc_sc.shape, F32)

    kpos = lax.broadcasted_iota(jnp.int32, (tq, 2 * tq), 0)
    qpos = lax.broadcasted_iota(jnp.int32, (tq, 2 * tq), 1) % tq
    allowed = (kpos // CHUNK) <= (qpos // CHUNK)

    def scores(j):
        kb = k_ref[0, pl.ds(pl.multiple_of(j * tq, tq), tq), :]
        return lax.dot_general(kb, q2, _NT, preferred_element_type=F32)

    def update(s, j):
        vt = vt_ref[0, :, pl.ds(pl.multiple_of(j * tq, tq), tq)]
        m_prev = m_sc[...]
        m_new = jnp.maximum(m_prev, jnp.max(s, axis=0, keepdims=True))
        alpha = jnp.exp(m_prev - m_new)
        pe = jnp.exp(s - m_new)
        l_sc[...] = alpha * l_sc[...] + jnp.sum(pe, axis=0, keepdims=True)
        acc_sc[...] = alpha * acc_sc[...] + jnp.dot(vt, pe.astype(BF16), preferred_element_type=F32)
        m_sc[...] = m_new

    def pair(jj, c):
        s_a, s_b = scores(2 * jj), scores(2 * jj + 1)
        update(s_a, 2 * jj)
        update(s_b, 2 * jj + 1)
        return c

    lax.fori_loop(0, i // 2, pair, 0)

    @pl.when(i % 2 == 1)
    def _():
        s_a, s_b = scores(i - 1), jnp.where(allowed, scores(i), -jnp.inf)
        update(s_a, i - 1)
        update(s_b, i)

    @pl.when(i % 2 == 0)
    def _():
        update(jnp.where(allowed, scores(i), -jnp.inf), i)

    lam = _lambda_value(lam_ref, lam_init)
    o = acc_sc[...] / l_sc[...]
    o = o[:, :tq] - lam * o[:, tq:]
    o = o * lax.rsqrt(jnp.mean(o * o, axis=0, keepdims=True) + LN_EPS) * g_ref[...]
    o_ref[0] = jnp.transpose(o * (1.0 - lam_init)).astype(BF16)


def attn_prompt(qb, kb, vt, lam_rows, subln_g, lam_init, tq):
    B, L, _ = qb.shape
    return pl.pallas_call(
        functools.partial(_attn_prompt_kernel, lam_init=lam_init),
        grid=(B, ATT_HEADS, L // tq),
        in_specs=[pl.BlockSpec((4, ATT_HEAD_DIM), lambda b, h, i: (0, 0)),
                  pl.BlockSpec((ATT_V_DIM, 1), lambda b, h, i: (0, 0)),
                  pl.BlockSpec((1, tq, LANES), lambda b, h, i: (b, i, h)),
                  pl.BlockSpec((1, L, LANES), lambda b, h, i: (b, 0, h)),
                  pl.BlockSpec((1, ATT_V_DIM, L), lambda b, h, i: (b, h, 0))],
        out_specs=pl.BlockSpec((1, tq, LANES), lambda b, h, i: (b, i, h)),
        out_shape=jax.ShapeDtypeStruct((B, L, ATT_HEADS * ATT_V_DIM), BF16),
        scratch_shapes=[pltpu.VMEM((1, 2 * tq), F32), pltpu.VMEM((1, 2 * tq), F32),
                        pltpu.VMEM((ATT_V_DIM, 2 * tq), F32)],
        compiler_params=_cparams(3),
        name="attn_prompt",
    )(lam_rows, subln_g.reshape(ATT_V_DIM, 1), qb, kb, vt)


def _attn_sample_kernel(lam_ref, g_ref, q_ref, ck_ref, cv_ref, kn_ref, vn_ref, o_ref, *, lam_init, past):
    ls = q_ref.shape[1]
    qs = _split_maps(q_ref[0])
    kp = ck_ref[0].astype(BF16)
    vp = cv_ref[0].astype(BF16)
    kn = kn_ref[0]
    vn = vn_ref[0]
    qpos = past + lax.broadcasted_iota(jnp.int32, (ls, ls), 0)
    kpos = past + lax.broadcasted_iota(jnp.int32, (ls, ls), 1)
    allowed = (kpos // CHUNK) <= (qpos // CHUNK)
    outs = []
    for p in range(2):
        sp = lax.dot_general(qs[p], kp, _NT, preferred_element_type=F32)
        sn = lax.dot_general(qs[p], kn, _NT, preferred_element_type=F32)
        sn = jnp.where(allowed, sn, -jnp.inf)
        m = jnp.maximum(jnp.max(sp, axis=1, keepdims=True), jnp.max(sn, axis=1, keepdims=True))
        pp = jnp.exp(sp - m)
        pn = jnp.exp(sn - m)
        denom = jnp.sum(pp, axis=1, keepdims=True) + jnp.sum(pn, axis=1, keepdims=True)
        acc = (jnp.dot(pp.astype(BF16), vp, preferred_element_type=F32)
               + jnp.dot(pn.astype(BF16), vn, preferred_element_type=F32))
        outs.append(acc / denom)
    lam = _lambda_value(lam_ref, lam_init)
    o_ref[0] = _finish_heads(outs[0], outs[1], lam, g_ref[...], lam_init).astype(BF16)


def attn_sample(qb, cache_k, cache_v, kb, vb, lam_rows, subln_g, lam_init):
    B, Ls, _ = qb.shape
    P = cache_k.shape[1]
    blk = lambda n: pl.BlockSpec((1, n, LANES), lambda b, h: (b, 0, h))
    return pl.pallas_call(
        functools.partial(_attn_sample_kernel, lam_init=lam_init, past=P),
        grid=(B, ATT_HEADS),
        in_specs=[pl.BlockSpec((4, ATT_HEAD_DIM), lambda b, h: (0, 0)),
                  pl.BlockSpec((1, ATT_V_DIM), lambda b, h: (0, 0)),
                  blk(Ls), blk(P), blk(P), blk(Ls), blk(Ls)],
        out_specs=blk(Ls),
        out_shape=jax.ShapeDtypeStruct((B, Ls, ATT_HEADS * ATT_V_DIM), BF16),
        compiler_params=_cparams(2),
        name="attn_sample",
    )(lam_rows, subln_g.reshape(1, ATT_V_DIM), qb, cache_k, cache_v, kb, vb)


def _outproj_kernel(conv_ref, att_ref, x_ref, g1_ref, sh2_ref, sc2_ref, wc_ref, wa_ref, b_ref,
                    lg_ref, lb_ref, x1_ref, hh_ref, h2b_ref, *, alpha):
    bb, tl, d = x_ref.shape
    conv = conv_ref[...].reshape(bb * tl, conv_ref.shape[2])
    att = att_ref[...].reshape(bb * tl, att_ref.shape[2])
    mm = (jnp.dot(conv, wc_ref[...], preferred_element_type=F32)
          + jnp.dot(att, wa_ref[...], preferred_element_type=F32) + b_ref[...])
    y = alpha * x_ref[...] + g1_ref[...] * mm.reshape(bb, tl, d)
    x1 = _ln(y, lg_ref[...], lb_ref[...])
    x1_ref[...] = x1
    h2 = x1 * (1.0 + sc2_ref[...]) + sh2_ref[...]
    h2b_ref[...] = h2.astype(BF16)
    hi, lo = _split_bf16(h2.reshape(bb * tl, d))
    bits = [lax.bitcast_convert_type(part.astype(F32), jnp.uint32) for part in (hi, lo)]
    rows = [b[:, c * LANES:(c + 1) * LANES] for b in bits for c in range(d // LANES)]
    for i in range(len(rows) // 2):
        hh_ref[:, i, :] = (rows[2 * i] >> 16) | rows[2 * i + 1]


def out_proj(conv_out, att, x, g1, sh2, sc2, wc_bf, wa_bf, b_out, ln_g, ln_b, alpha, bb, tl):
    B, L, D = x.shape
    tok = lambda b, l: (b, l, 0)
    mod = lambda b, l: (b, 0, 0)
    const = lambda b, l: (0, 0)
    return pl.pallas_call(
        functools.partial(_outproj_kernel, alpha=alpha),
        grid=(B // bb, L // tl),
        in_specs=[pl.BlockSpec((bb, tl, conv_out.shape[2]), tok),
                  pl.BlockSpec((bb, tl, att.shape[2]), tok),
                  pl.BlockSpec((bb, tl, D), tok),
                  pl.BlockSpec((bb, 1, D), mod), pl.BlockSpec((bb, 1, D), mod), pl.BlockSpec((bb, 1, D), mod),
                  pl.BlockSpec(wc_bf.shape, const), pl.BlockSpec(wa_bf.shape, const),
                  pl.BlockSpec((1, D), const), pl.BlockSpec((1, D), const), pl.BlockSpec((1, D), const)],
        out_specs=[pl.BlockSpec((bb, tl, D), tok),
                   pl.BlockSpec((bb * tl, D // LANES, LANES), lambda b, l: (b * (L // tl) + l, 0, 0)),
                   pl.BlockSpec((bb, tl, D), tok)],
        out_shape=[jax.ShapeDtypeStruct((B, L, D), F32),
                   jax.ShapeDtypeStruct((B * L, D // LANES, LANES), jnp.uint32),
                   jax.ShapeDtypeStruct((B, L, D), BF16)],
        compiler_params=_cparams(2),
        name="out_proj",
    )(conv_out, att, x, g1, sh2, sc2, wc_bf, wa_bf, b_out.reshape(1, D), ln_g.reshape(1, D), ln_b.reshape(1, D))


TOPK_STREAMS = 2


def _extract_top(vals, order, count, out, payload=None):
    big = jnp.float32(1e9)
    out_v, out_p = [], []
    for _ in range(count):
        m = jnp.max(vals, axis=0, keepdims=True)
        pos = jnp.min(jnp.where(vals == m, order, big), axis=0, keepdims=True)
        sel = order == pos
        out_v.append(m)
        out_p.append(pos if payload is None else jnp.max(jnp.where(sel, payload, -1.0), axis=0, keepdims=True))
        vals = jnp.where(sel, -jnp.inf, vals)
        yield
    out.append((jnp.concatenate(out_v, axis=0), jnp.concatenate(out_p, axis=0)))


def _interleave(streams):
    done = [0] * len(streams)
    total = max(n for _, n in streams)
    for tick in range(1, total + 1):
        for s, (gen, n) in enumerate(streams):
            while done[s] * total < tick * n:
                next(gen, None)
                done[s] += 1
    for gen, _ in streams:
        for _ in gen:
            pass


class _TopkWork:
    def __init__(self, h_ref, wq_ref, keys_ref, q_sc, e_sc, g_sc, max_streams=TOPK_STREAMS):
        self.keys_ref, self.q_sc, self.e_sc, self.g_sc = keys_ref, q_sc, e_sc, g_sc
        self.n_hp, self.n_keys, _ = keys_ref.shape
        tl = h_ref.shape[0]
        K = PEER_TOPK
        qp = jnp.dot(h_ref[...], wq_ref[...], preferred_element_type=F32)
        for hp in range(self.n_hp):
            q_sc[hp] = qp[:, hp * LANES:(hp + 1) * LANES].astype(BF16)
        self.key_iota = lax.broadcasted_iota(jnp.int32, (self.n_keys, LANES), 0).astype(F32)
        n_cand = K + (K - 1) * SUBLANES
        row = lax.broadcasted_iota(jnp.int32, (n_cand, LANES), 0)
        ci = jnp.where(row < K, 0, ((row - K) >> 3) + 1)
        cj = jnp.where(row < K, row, (row - K) & 7)
        limit = jnp.where(ci == 0, 16, jnp.where(ci == 1, 8, jnp.where(ci == 2, 5, jnp.where(
            ci == 3, 4, jnp.where(ci == 4, 3, jnp.where(ci <= 7, 2, 1))))))
        self.cand_ok = cj < limit
        self.cand_order = (ci * K + cj).astype(F32)
        n_chunks = tl // LANES
        self.streams = max_streams if n_chunks % max_streams == 0 else 1
        self.groups = n_chunks // self.streams
        self.n_steps = (self.n_hp // 2) * self.groups

    CHUNK_STEPS = 3 * PEER_TOPK

    def head_chunk(self, h, col):
        K = PEER_TOPK
        halves = []
        for p in range(2):
            qc = self.q_sc[2 * h + p, pl.ds(col, LANES), :]
            s = lax.dot_general(self.keys_ref[2 * h + p], qc, _NT, preferred_element_type=F32)
            yield from _extract_top(s, self.key_iota, K, halves)
        (sv0, si0), (sv1, si1) = halves
        cv = [sv0[0:1] + sv1]
        ce = [si0[0:1] * self.n_keys + si1]
        for r in range(1, K):
            cv.append(sv0[r:r + 1] + sv1[0:SUBLANES])
            ce.append(si0[r:r + 1] * self.n_keys + si1[0:SUBLANES])
        cand = jnp.where(self.cand_ok, jnp.concatenate(cv, axis=0), -jnp.inf)
        final = []
        yield from _extract_top(cand, self.cand_order, K, final, payload=jnp.concatenate(ce, axis=0))
        fv, fe = final[0]
        ex = jnp.exp(fv - fv[0:1])
        gate = ex / jnp.sum(ex, axis=0, keepdims=True)
        off = pl.multiple_of(h * K, K)
        self.e_sc[pl.ds(off, K), pl.ds(col, LANES)] = fe
        self.g_sc[pl.ds(off, K), pl.ds(col, LANES)] = gate

    def step_streams(self, i):
        h = i // self.groups
        return [(self.head_chunk(h, pl.multiple_of(((i % self.groups) * self.streams + c) * LANES, LANES)),
                 self.CHUNK_STEPS) for c in range(self.streams)]

    def finish(self, e_ref, g_ref):
        e_ref[...] = jnp.transpose(self.e_sc[...]).astype(jnp.int32) * TABLE_ROWS_PER_EXPERT
        g_ref[...] = jnp.transpose(self.g_sc[...])


def _topk_kernel(h_ref, wq_ref, keys_ref, e_ref, g_ref, q_sc, e_sc, g_sc):
    work = _TopkWork(h_ref, wq_ref, keys_ref, q_sc, e_sc, g_sc)

    def step(i, carry):
        _interleave(work.step_streams(i))
        return carry

    lax.fori_loop(0, work.n_steps, step, 0)
    work.finish(e_ref, g_ref)


def _topk_specs(D, wq_bf, keys_bf, tl, blk0):
    n_hp, n_keys, half = keys_bf.shape
    slots = (n_hp // 2) * PEER_TOPK
    assert half == LANES and n_keys == LANES and PEER_TOPK * PEER_TOPK <= 256
    in_specs = [pl.BlockSpec((tl, D), lambda i: (i + blk0, 0)),
                pl.BlockSpec(wq_bf.shape, lambda i: (0, 0), pipeline_mode=pl.Buffered(1)),
                pl.BlockSpec(keys_bf.shape, lambda i: (0, 0, 0), pipeline_mode=pl.Buffered(1))]
    out_specs = [pl.BlockSpec((tl, slots), lambda i: (i, 0))] * 2
    scratch = [pltpu.VMEM((n_hp, tl, LANES), BF16), pltpu.VMEM((slots, tl), F32), pltpu.VMEM((slots, tl), F32)]
    return slots, in_specs, out_specs, scratch


def peer_topk(h2b, wq_bf, keys_bf, tl, blk0=0, nblk=None):
    T, D = h2b.shape
    nblk = T // tl if nblk is None else nblk
    slots, in_specs, out_specs, scratch = _topk_specs(D, wq_bf, keys_bf, tl, blk0)
    return pl.pallas_call(
        _topk_kernel,
        grid=(nblk,),
        in_specs=in_specs,
        out_specs=out_specs,
        out_shape=[jax.ShapeDtypeStruct((nblk * tl, slots), jnp.int32),
                   jax.ShapeDtypeStruct((nblk * tl, slots), F32)],
        scratch_shapes=scratch,
        compiler_params=_cparams(1),
        name="peer_topk",
    )(h2b, wq_bf, keys_bf)


TABLE_ROWS_PER_EXPERT = 4
PEER_TOKEN_BLOCK = 128
PEER_TOKEN_UNROLL = 8
GATHER_ID_WINDOW = 16


PACK_EXPERT_BLOCK = 512


def _pack_kernel(t_ref, o_ref):
    n = t_ref.shape[0]
    bits = lax.bitcast_convert_type(t_ref[...].astype(BF16).astype(F32), jnp.uint32)
    for i in range(TABLE_ROWS_PER_EXPERT):
        lo = bits[:, (2 * i) * LANES:(2 * i + 1) * LANES]
        hi = bits[:, (2 * i + 1) * LANES:(2 * i + 2) * LANES]
        o_ref[pl.ds(i, n, stride=TABLE_ROWS_PER_EXPERT), :] = (lo >> 16) | hi


def pack_table(tab):
    n, d = tab.shape
    assert d == 2 * LANES * TABLE_ROWS_PER_EXPERT
    tn = min(n, PACK_EXPERT_BLOCK)
    return pl.pallas_call(
        _pack_kernel,
        grid=(n // tn,),
        in_specs=[pl.BlockSpec((tn, d), lambda i: (i, 0))],
        out_specs=pl.BlockSpec((tn * TABLE_ROWS_PER_EXPERT, LANES), lambda i: (i, 0)),
        out_shape=jax.ShapeDtypeStruct((n * TABLE_ROWS_PER_EXPERT, LANES), jnp.uint32),
        compiler_params=_cparams(1),
        name="pack_table",
    )(tab)


def _gather_rows(e_ref, tab_ref, base, slots):
    rows = []
    for j0 in range(0, slots, GATHER_ID_WINDOW):
        ids = e_ref.at[pl.ds(pl.multiple_of(base + j0, GATHER_ID_WINDOW), GATHER_ID_WINDOW)]
        for j in range(GATHER_ID_WINDOW):
            start = pl.multiple_of(ids[j], TABLE_ROWS_PER_EXPERT)
            rows.append(pltpu.bitcast(tab_ref[pl.ds(start, TABLE_ROWS_PER_EXPERT), :], BF16))
    return jnp.concatenate(rows, axis=0)


def _split_bf16(x):
    hi = x.astype(BF16)
    return hi, (x - hi.astype(F32)).astype(BF16)


def _chunk_mask(slots):
    lane = lax.broadcasted_iota(jnp.int32, (SUBLANES, slots * SUBLANES), 1)
    sub = lax.broadcasted_iota(jnp.int32, (SUBLANES, slots * SUBLANES), 0)
    return (lane & (SUBLANES - 1)) == sub


def _slot_spread(slots):
    r = lax.broadcasted_iota(jnp.int32, (slots, slots * SUBLANES), 0)
    c = lax.broadcasted_iota(jnp.int32, (slots, slots * SUBLANES), 1)
    return jnp.where((c >> 3) == r, 1.0, 0.0).astype(BF16)


def _peer_a_tokens(count, t0, z0, e_ref, hh_ref, tab_ref, z_sc, mask, slots):
    for u in range(count):
        t = t0 + u
        g = _gather_rows(e_ref, tab_ref, pl.multiple_of(t * slots, slots), slots)
        hh = pltpu.bitcast(hh_ref[t], BF16)
        z = lax.dot_general(hh, g, _NT, preferred_element_type=F32)
        z8 = z[0:SUBLANES] + z[SUBLANES:2 * SUBLANES]
        z_sc[pl.ds(pl.multiple_of((z0 + u) * SUBLANES, SUBLANES), SUBLANES), :] = jnp.where(mask, z8, 0.0)
        yield


def _peer_a_weights(z_sc, gate, slots):
    tb = z_sc.shape[0] // SUBLANES
    zhi, zlo = _split_bf16(z_sc[...])
    spread = _slot_spread(slots)
    a8 = (lax.dot_general(zhi, spread, _NT, preferred_element_type=F32)
          + lax.dot_general(zlo, spread, _NT, preferred_element_type=F32))
    a = jnp.sum(a8.reshape(tb, SUBLANES, slots), axis=1)
    return 0.5 * a * (1.0 + lax.erf(a * (2.0 ** -0.5))) * gate


def _peer_a_kernel(e_ref, hh_ref, g_ref, tab_ref, w_ref, z_sc, *, slots):
    tb = hh_ref.shape[0]
    mask = _chunk_mask(slots)

    def tokens(i, carry):
        t0 = i * PEER_TOKEN_UNROLL
        for _ in _peer_a_tokens(PEER_TOKEN_UNROLL, t0, t0, e_ref, hh_ref, tab_ref, z_sc, mask, slots):
            pass
        return carry

    lax.fori_loop(0, tb // PEER_TOKEN_UNROLL, tokens, 0)
    w_ref[...] = _peer_a_weights(z_sc, g_ref[...], slots)


def _peer_a_specs(tab_u32, slots, tb, blk0):
    in_specs = [pl.BlockSpec((tb * slots,), lambda i: (i,), memory_space=pltpu.SMEM),
                pl.BlockSpec((tb, SUBLANES, LANES), lambda i: (i + blk0, 0, 0)),
                pl.BlockSpec((tb, slots), lambda i: (i, 0)),
                pl.BlockSpec(tab_u32.shape, lambda i: (0, 0), pipeline_mode=pl.Buffered(1))]
    scratch = [pltpu.VMEM((PEER_TOKEN_BLOCK * SUBLANES, slots * SUBLANES), F32)]
    return in_specs, pl.BlockSpec((tb, slots), lambda i: (i, 0)), scratch


def peer_a(e_flat, hh, gate, tab_u32, slots, blk0=0):
    T = gate.shape[0]
    tb = PEER_TOKEN_BLOCK
    in_specs, out_spec, scratch = _peer_a_specs(tab_u32, slots, tb, blk0)
    return pl.pallas_call(
        functools.partial(_peer_a_kernel, slots=slots),
        grid=(T // tb,),
        in_specs=in_specs,
        out_specs=out_spec,
        out_shape=jax.ShapeDtypeStruct((T, slots), F32),
        scratch_shapes=scratch,
        compiler_params=_cparams(1, VMEM_LIMIT_TABLE),
        name="peer_a",
    )(e_flat, hh, gate, tab_u32)


def _topk_a_kernel(e_ref, hh_ref, gate_ref, tab_ref, h_ref, wq_ref, keys_ref, w_ref, e_out_ref, g_out_ref,
                   z_sc, q_sc, e_sc, g_sc, *, slots):
    tl = hh_ref.shape[0]
    work = _TopkWork(h_ref, wq_ref, keys_ref, q_sc, e_sc, g_sc, max_streams=1)
    mask = _chunk_mask(slots)
    per_step = tl // work.n_steps
    flush = PEER_TOKEN_BLOCK // per_step
    assert per_step % PEER_TOKEN_UNROLL == 0 and flush * per_step == PEER_TOKEN_BLOCK

    def step(i, carry):
        gathers = _peer_a_tokens(per_step, i * per_step, (i % flush) * per_step, e_ref, hh_ref, tab_ref, z_sc,
                                 mask, slots)
        _interleave(work.step_streams(i) + [(gathers, per_step)])

        @pl.when(i % flush == flush - 1)
        def _():
            rows = pl.ds(pl.multiple_of((i // flush) * PEER_TOKEN_BLOCK, PEER_TOKEN_BLOCK), PEER_TOKEN_BLOCK)
            w_ref[rows, :] = _peer_a_weights(z_sc, gate_ref[rows, :], slots)

        return carry

    lax.fori_loop(0, work.n_steps, step, 0)
    work.finish(e_out_ref, g_out_ref)


def peer_topk_a(e_flat, hh, gate, tab_u32, h2b, wq_bf, keys_bf, tl, a_blk0, k_blk0):
    T, D = gate.shape[0], h2b.shape[1]
    slots, k_in, k_out, k_scratch = _topk_specs(D, wq_bf, keys_bf, tl, k_blk0)
    a_in, a_out, a_scratch = _peer_a_specs(tab_u32, slots, tl, a_blk0)
    return pl.pallas_call(
        functools.partial(_topk_a_kernel, slots=slots),
        grid=(T // tl,),
        in_specs=a_in + k_in,
        out_specs=[a_out] + k_out,
        out_shape=[jax.ShapeDtypeStruct((T, slots), F32), jax.ShapeDtypeStruct((T, slots), jnp.int32),
                   jax.ShapeDtypeStruct((T, slots), F32)],
        scratch_shapes=a_scratch + k_scratch,
        compiler_params=_cparams(1, VMEM_LIMIT_FUSED),
        name="peer_topk_a",
    )(e_flat, hh, gate, tab_u32, h2b, wq_bf, keys_bf)


def _peer_b_kernel(e_ref, w_ref, tab_ref, o_ref, rep_sc, *, slots):
    tb = o_ref.shape[0]
    spread = _slot_spread(slots)
    whi, wlo = _split_bf16(w_ref[...])
    rep_sc[0] = jnp.dot(whi, spread, preferred_element_type=F32)
    rep_sc[1] = jnp.dot(wlo, spread, preferred_element_type=F32)
    mask = _chunk_mask(slots)
    wide = (SUBLANES, slots * SUBLANES)

    def tokens(i, carry):
        for u in range(PEER_TOKEN_UNROLL):
            t = i * PEER_TOKEN_UNROLL + u
            g = _gather_rows(e_ref, tab_ref, pl.multiple_of(t * slots, slots), slots)
            parts = [jnp.where(mask, jnp.broadcast_to(rep_sc[p, pl.ds(t, 1), :], wide), 0.0).astype(BF16)
                     for p in range(2)]
            r = jnp.dot(jnp.concatenate(parts, axis=0), g, preferred_element_type=F32)
            o_ref[t] = r[0:SUBLANES] + r[SUBLANES:2 * SUBLANES]
        return carry

    lax.fori_loop(0, tb // PEER_TOKEN_UNROLL, tokens, 0)


def peer_b(e_flat, w, tab_u32, slots):
    T = w.shape[0]
    tb = PEER_TOKEN_BLOCK
    return pl.pallas_call(
        functools.partial(_peer_b_kernel, slots=slots),
        grid=(T // tb,),
        in_specs=[pl.BlockSpec((tb * slots,), lambda i: (i,), memory_space=pltpu.SMEM),
                  pl.BlockSpec((tb, slots), lambda i: (i, 0)),
                  pl.BlockSpec(tab_u32.shape, lambda i: (0, 0), pipeline_mode=pl.Buffered(1))],
        out_specs=pl.BlockSpec((tb, SUBLANES, LANES), lambda i: (i, 0, 0)),
        out_shape=jax.ShapeDtypeStruct((T, SUBLANES, LANES), F32),
        scratch_shapes=[pltpu.VMEM((2, tb, slots * SUBLANES), F32)],
        compiler_params=_cparams(1, VMEM_LIMIT_TABLE),
        name="peer_b",
    )(e_flat, w, tab_u32)


def _final_kernel(x_ref, ff_ref, g2_ref, lg_ref, lb_ref, o_ref, *, alpha):
    bb, tl, d = x_ref.shape
    ff = jnp.concatenate([ff_ref[:, c, :] for c in range(d // LANES)], axis=1)
    y = alpha * x_ref[...] + g2_ref[...] * ff.reshape(bb, tl, d)
    o_ref[...] = _ln(y, lg_ref[...], lb_ref[...])


def final_ln(x1, ff, g2, ln_g, ln_b, alpha, bb, tl):
    B, L, D = x1.shape
    tok = lambda b, l: (b, l, 0)
    const = lambda b, l: (0, 0)
    return pl.pallas_call(
        functools.partial(_final_kernel, alpha=alpha),
        grid=(B // bb, L // tl),
        in_specs=[pl.BlockSpec((bb, tl, D), tok),
                  pl.BlockSpec((bb * tl, D // LANES, LANES), lambda b, l: (b * (L // tl) + l, 0, 0)),
                  pl.BlockSpec((bb, 1, D), lambda b, l: (b, 0, 0)),
                  pl.BlockSpec((1, D), const), pl.BlockSpec((1, D), const)],
        out_specs=pl.BlockSpec((bb, tl, D), tok),
        out_shape=jax.ShapeDtypeStruct((B, L, D), F32),
        compiler_params=_cparams(2),
        name="final_ln",
    )(x1, ff, g2, ln_g.reshape(1, D), ln_b.reshape(1, D))


def _rope_tables(past, length, n_groups):
    half = ATT_HEAD_DIM // 2
    inv = 1.0 / (ROPE_THETA ** (jnp.arange(half, dtype=F32) / half))
    ang = (past + jnp.arange(length)).astype(F32)[:, None] * inv[None, :]
    cos = jnp.cos(ang)
    sin = jnp.sin(ang)
    cos_t = jnp.tile(jnp.concatenate([cos, cos], axis=-1), (1, n_groups))
    sin_t = jnp.tile(jnp.concatenate([-sin, sin], axis=-1), (1, n_groups))
    return cos_t, sin_t


def _token_tiles(B, L, target):
    if L >= target:
        return 1, target
    bb = max(1, min(B, target // L))
    while B % bb:
        bb -= 1
    return bb, L


TOPK_TOKEN_BLOCK = 512
PEER_MAX_SPLIT = 8


def _peer_select_and_weigh(h2b, hh, wts, slots):
    T = h2b.shape[0]
    tl = min(T, TOPK_TOKEN_BLOCK)
    wq, keys, tab = wts["w_query"], wts["sub_keys"], wts["u_tab"]
    split = max(s for s in range(1, PEER_MAX_SPLIT + 1) if T % (s * tl) == 0)
    if split == 1 or tl % PEER_TOKEN_BLOCK:
        e, gate = peer_topk(h2b, wq, keys, tl)
        return e, peer_a(e.reshape(T * slots), hh, gate, tab, slots)
    nblk = T // (split * tl)
    chunk = nblk * tl
    e_c, gate_c = peer_topk(h2b, wq, keys, tl, 0, nblk)
    es, ws = [e_c], []
    for c in range(1, split):
        w_c, e_n, gate_n = peer_topk_a(e_c.reshape(chunk * slots), hh, gate_c, tab, h2b, wq, keys, tl,
                                       (c - 1) * nblk, c * nblk)
        ws.append(w_c)
        es.append(e_n)
        e_c, gate_c = e_n, gate_n
    ws.append(peer_a(e_c.reshape(chunk * slots), hh, gate_c, tab, slots, (split - 1) * chunk // PEER_TOKEN_BLOCK))
    return jnp.concatenate(es, axis=0), jnp.concatenate(ws, axis=0)


def _layer(x, mod, conv_hist, k_past, v_past, wts, lam_init, alpha):
    B, L, D = x.shape
    conv_dim = wts["w_dw"].shape[1]
    qk_dim = ATT_HEADS * 2 * ATT_HEAD_DIM
    att_dim = ATT_HEADS * ATT_V_DIM
    width = wts["w_dw"].shape[0]
    assert L >= width - 1 and width - 1 <= HIST_PAD
    sh1, sc1, g1, sh2, sc2, g2 = [m[:, None, :] for m in jnp.split(mod, 6, axis=-1)]
    past = 0 if k_past is None else k_past.shape[1]
    cos, sin = _rope_tables(past, L, qk_dim // ATT_HEAD_DIM)
    bb, tl = _token_tiles(B, L, 512)

    u, k, v, qb, kb, vb = in_proj(x, sh1, sc1, wts["w_in"], wts["b_in"], cos, sin, bb, tl,
                                  conv_dim, qk_dim, att_dim, transposed_v=k_past is None)
    hist_pad = jnp.pad(conv_hist, ((0, 0), (HIST_PAD - (width - 1), 0), (0, 0)))
    conv_out = conv_ln(u, hist_pad, wts["w_dw"], wts["b_dw"], wts["conv_ln_g"], wts["conv_ln_b"], min(L, 256))
    conv_state = u[:, L - (width - 1):, :]

    if k_past is None:
        att = attn_prompt(qb, kb, vb, wts["lam_rows"], wts["subln_g"], lam_init, min(L, 256))
    else:
        att = attn_sample(qb, k_past.reshape(B, past, qk_dim), v_past.reshape(B, past, att_dim), kb, vb,
                          wts["lam_rows"], wts["subln_g"], lam_init)

    x1, hh, h2b = out_proj(conv_out, att, x, g1, sh2, sc2, wts["w_out_conv"], wts["w_out_att"], wts["b_out"],
                           wts["ln1_g"], wts["ln1_b"], alpha, bb, tl)

    T = B * L
    slots = (wts["sub_keys"].shape[0] // 2) * PEER_TOPK
    e, w = _peer_select_and_weigh(h2b.reshape(T, D), hh, wts, slots)
    ff = peer_b(e.reshape(T * slots), w, wts["v_tab"], slots)
    out = final_ln(x1, ff, g2, wts["ln2_g"], wts["ln2_b"], alpha, bb, tl)
    return out, conv_state, k, v


def kernel(x_prompt, x_sample, cache_k, cache_v, cache_conv, c_prompt, c_sample, w_ada, b_ada, w_in, b_in, w_dw, b_dw, conv_ln_g, conv_ln_b, lam_q1, lam_k1, lam_q2, lam_k2, subln_g, w_out, b_out, ln1_g, ln1_b, w_query, sub_keys, u_tab, v_tab, ln2_g, ln2_b):
    depth = w_ada.shape[0]
    D = x_prompt.shape[-1]
    assert D == SUBLANES * LANES
    alpha = (2 * depth) ** 0.25
    Bp, Bs = c_prompt.shape[0], c_sample.shape[0]
    xp, xs = x_prompt, x_sample
    outs = [[] for _ in range(6)]
    for l in range(depth):
        lam_init = 0.8 - 0.6 * math.exp(-0.3 * l)
        conv_dim = w_dw.shape[2]
        n_exp = u_tab.shape[1]
        wts = {
            "w_in": w_in[l].astype(BF16), "b_in": b_in[l], "w_dw": w_dw[l], "b_dw": b_dw[l],
            "conv_ln_g": conv_ln_g[l], "conv_ln_b": conv_ln_b[l],
            "lam_rows": jnp.stack([lam_q1[l], lam_k1[l], lam_q2[l], lam_k2[l]]).astype(F32),
            "subln_g": subln_g[l],
            "w_out_conv": w_out[l, :conv_dim].astype(BF16), "w_out_att": w_out[l, conv_dim:].astype(BF16),
            "b_out": b_out[l], "ln1_g": ln1_g[l], "ln1_b": ln1_b[l],
            "w_query": w_query[l].astype(BF16),
            "sub_keys": sub_keys[l].astype(BF16).reshape(-1, sub_keys.shape[3], sub_keys.shape[4]),
            "u_tab": pack_table(u_tab[l]), "v_tab": pack_table(v_tab[l]),
            "ln2_g": ln2_g[l], "ln2_b": ln2_b[l],
        }
        c_all = jnp.concatenate([c_prompt, c_sample], axis=0)
        pad = (-c_all.shape[0]) % 16
        c_all = jnp.pad(c_all, ((0, pad), (0, 0)))
        mod = ada_mod(c_all, w_ada[l].astype(BF16), b_ada[l])
        hist0 = jnp.zeros((Bp, w_dw.shape[1] - 1, conv_dim), xp.dtype)
        xp, cp, kp, vp = _layer(xp, mod[:Bp], hist0, None, None, wts, lam_init, alpha)
        xs, cs, kn, vn = _layer(xs, mod[Bp:Bp + Bs], cache_conv[l], cache_k[l], cache_v[l], wts, lam_init, alpha)
        for lst, val in zip(outs, (kp, vp, cp, kn, vn, cs)):
            lst.append(val)
    Lp, Ls = x_prompt.shape[1], x_sample.shape[1]
    k_prompt = jnp.stack(outs[0]).reshape(depth, Bp, Lp, ATT_HEADS, 2, ATT_HEAD_DIM)
    v_prompt = jnp.stack(outs[1]).reshape(depth, Bp, Lp, ATT_HEADS, ATT_V_DIM)
    conv_prompt = jnp.stack(outs[2])
    k_sample = jnp.stack(outs[3]).reshape(depth, Bs, Ls, ATT_HEADS, 2, ATT_HEAD_DIM)
    v_sample = jnp.stack(outs[4]).reshape(depth, Bs, Ls, ATT_HEADS, ATT_V_DIM)
    conv_sample = jnp.stack(outs[5])
    return (xp, xs, k_prompt, v_prompt, conv_prompt, k_sample, v_sample, conv_sample)
```

```python
import functools
import math

import jax
import jax.numpy as jnp
import numpy as np
from jax import lax
from jax.experimental import pallas as pl
from jax.experimental.pallas import tpu as pltpu

F32 = jnp.float32
BF16 = jnp.bfloat16

CHUNK = 64
ATT_HEADS = 4
ATT_HEAD_DIM = 64
ATT_V_DIM = 2 * ATT_HEAD_DIM
ROPE_THETA = 10000.0
PEER_TOPK = 16
LN_EPS = 1e-5

LANES = 128
SUBLANES = 8
VMEM_LIMIT_DEFAULT = 48 * 1024 * 1024
VMEM_LIMIT_TABLE = 56 * 1024 * 1024
VMEM_LIMIT_FUSED = 60 * 1024 * 1024

_NT = (((1,), (1,)), ((), ()))


def _cparams(n_axes, vmem=VMEM_LIMIT_DEFAULT):
    return pltpu.CompilerParams(dimension_semantics=("arbitrary",) * n_axes, vmem_limit_bytes=vmem)


def _ln(y, g, b):
    mu = jnp.mean(y, axis=-1, keepdims=True)
    d = y - mu
    var = jnp.mean(d * d, axis=-1, keepdims=True)
    return d * lax.rsqrt(var + LN_EPS) * g + b


def _ada_kernel(c_ref, w_ref, b_ref, o_ref):
    c = c_ref[...]
    s = (c * jax.nn.sigmoid(c)).astype(BF16)
    o_ref[...] = jnp.dot(s, w_ref[...], preferred_element_type=F32) + b_ref[...]


def ada_mod(c, w_bf, b):
    rows, d = c.shape
    n = w_bf.shape[1]
    tn = min(n, 1024)
    return pl.pallas_call(
        _ada_kernel,
        grid=(n // tn,),
        in_specs=[pl.BlockSpec((rows, d), lambda j: (0, 0)),
                  pl.BlockSpec((d, tn), lambda j: (0, j)),
                  pl.BlockSpec((1, tn), lambda j: (0, j))],
        out_specs=pl.BlockSpec((rows, tn), lambda j: (0, j)),
        out_shape=jax.ShapeDtypeStruct((rows, n), F32),
        compiler_params=_cparams(1),
        name="ada_mod",
    )(c, w_bf, b.reshape(1, n))


def _inproj_kernel(x_ref, sh_ref, sc_ref, w_ref, b_ref, cos_ref, sin_ref,
                   wvt_ref, bvt_ref, u_ref, k_ref, v_ref, qb_ref, kb_ref, vb_ref, *, conv_dim, qk_dim,
                   transposed_v):
    bb, tl, d = x_ref.shape
    h = x_ref[...] * (1.0 + sc_ref[...]) + sh_ref[...]
    hb = h.reshape(bb * tl, d).astype(BF16)
    z = jnp.dot(hb, w_ref[...], preferred_element_type=F32) + b_ref[...]
    c0, c1, c2, c3 = conv_dim, 2 * conv_dim, 2 * conv_dim + qk_dim, 2 * conv_dim + 2 * qk_dim
    u = z[:, :c0] * jax.nn.sigmoid(z[:, c0:c1])
    u_ref[...] = u.reshape(bb, tl, conv_dim)

    lane = lax.broadcasted_iota(jnp.int32, (bb * tl, qk_dim), 1)
    first_half = (lane % ATT_HEAD_DIM) < (ATT_HEAD_DIM // 2)
    cos = jnp.broadcast_to(cos_ref[...][None], (bb, tl, qk_dim)).reshape(bb * tl, qk_dim)
    sin = jnp.broadcast_to(sin_ref[...][None], (bb, tl, qk_dim)).reshape(bb * tl, qk_dim)

    def rope(t):
        swapped = jnp.where(first_half,
                            pltpu.roll(t, qk_dim - ATT_HEAD_DIM // 2, axis=1),
                            pltpu.roll(t, ATT_HEAD_DIM // 2, axis=1))
        return t * cos + swapped * sin

    q = rope(z[:, c1:c2]) * (ATT_HEAD_DIM ** -0.5)
    k = rope(z[:, c2:c3])
    v = z[:, c3:]
    k_ref[...] = k.reshape(bb, tl, qk_dim)
    v_ref[...] = v.reshape(bb, tl, v.shape[1])
    qb_ref[...] = q.astype(BF16).reshape(bb, tl, qk_dim)
    kb_ref[...] = k.astype(BF16).reshape(bb, tl, qk_dim)
    if transposed_v:
        vt = lax.dot_general(wvt_ref[...], hb, _NT, preferred_element_type=F32) + bvt_ref[...]
        vb_ref[0] = vt.astype(BF16)
    else:
        vb_ref[...] = v.astype(BF16).reshape(bb, tl, v.shape[1])


def in_proj(x, sh1, sc1, w_bf, b_in, cos, sin, bb, tl, conv_dim, qk_dim, att_dim, transposed_v):
    B, L, D = x.shape
    n = w_bf.shape[1]
    assert not transposed_v or bb == 1
    tok = lambda b, l: (b, l, 0)
    mod = lambda b, l: (b, 0, 0)
    const = lambda b, l: (0, 0)
    out_shapes = [jax.ShapeDtypeStruct((B, L, conv_dim), F32),
                  jax.ShapeDtypeStruct((B, L, qk_dim), F32),
                  jax.ShapeDtypeStruct((B, L, att_dim), F32),
                  jax.ShapeDtypeStruct((B, L, qk_dim), BF16),
                  jax.ShapeDtypeStruct((B, L, qk_dim), BF16),
                  jax.ShapeDtypeStruct((B, L, att_dim), BF16)]
    out_specs = [pl.BlockSpec((bb, tl, s.shape[2]), tok) for s in out_shapes]
    if transposed_v:
        out_shapes[5] = jax.ShapeDtypeStruct((B, att_dim, L), BF16)
        out_specs[5] = pl.BlockSpec((1, att_dim, tl), lambda b, l: (b, 0, l))
    wvt = jnp.transpose(w_bf[:, n - att_dim:])
    bvt = b_in[n - att_dim:].reshape(att_dim, 1)
    return pl.pallas_call(
        functools.partial(_inproj_kernel, conv_dim=conv_dim, qk_dim=qk_dim, transposed_v=transposed_v),
        grid=(B // bb, L // tl),
        in_specs=[pl.BlockSpec((bb, tl, D), tok),
                  pl.BlockSpec((bb, 1, D), mod),
                  pl.BlockSpec((bb, 1, D), mod),
                  pl.BlockSpec((D, n), const),
                  pl.BlockSpec((1, n), const),
                  pl.BlockSpec((tl, qk_dim), lambda b, l: (l, 0)),
                  pl.BlockSpec((tl, qk_dim), lambda b, l: (l, 0)),
                  pl.BlockSpec((att_dim, D), const),
                  pl.BlockSpec((att_dim, 1), const)],
        out_specs=out_specs,
        out_shape=out_shapes,
        compiler_params=_cparams(2),
        name="in_proj",
    )(x, sh1, sc1, w_bf, b_in.reshape(1, n), cos, sin, wvt, bvt)


HIST_PAD = 32


def _conv_kernel(u_ref, prev_ref, hist_ref, w_ref, b_ref, g_ref, beta_ref, o_ref, win_ref, *, width):
    i = pl.program_id(1)
    tl = u_ref.shape[1]

    @pl.when(i == 0)
    def _():
        win_ref[0:HIST_PAD, :] = hist_ref[0]

    @pl.when(i != 0)
    def _():
        win_ref[0:HIST_PAD, :] = prev_ref[0]

    win_ref[HIST_PAD:HIST_PAD + tl, :] = u_ref[0]
    off = HIST_PAD - (width - 1)
    acc = w_ref[0:1, :] * win_ref[off:off + tl, :]
    for j in range(1, width):
        acc = acc + w_ref[j:j + 1, :] * win_ref[off + j:off + j + tl, :]
    y = _ln(acc + b_ref[...], g_ref[...], beta_ref[...])
    o_ref[0] = (y * jax.nn.sigmoid(y)).astype(BF16)


def conv_ln(u, hist_pad, w_dw, b_dw, ln_g, ln_b, tl):
    B, L, C = u.shape
    width = w_dw.shape[0]
    steps = tl // HIST_PAD
    row = lambda b, i: (0, 0)
    return pl.pallas_call(
        functools.partial(_conv_kernel, width=width),
        grid=(B, L // tl),
        in_specs=[pl.BlockSpec((1, tl, C), lambda b, i: (b, i, 0)),
                  pl.BlockSpec((1, HIST_PAD, C), lambda b, i: (b, jnp.maximum(i * steps - 1, 0), 0)),
                  pl.BlockSpec((1, HIST_PAD, C), lambda b, i: (b, 0, 0)),
                  pl.BlockSpec((width, C), row),
                  pl.BlockSpec((1, C), row), pl.BlockSpec((1, C), row), pl.BlockSpec((1, C), row)],
        out_specs=pl.BlockSpec((1, tl, C), lambda b, i: (b, i, 0)),
        out_shape=jax.ShapeDtypeStruct((B, L, C), BF16),
        scratch_shapes=[pltpu.VMEM((HIST_PAD + tl, C), F32)],
        compiler_params=_cparams(2),
        name="conv_ln",
    )(u, u, hist_pad, w_dw, b_dw.reshape(1, C), ln_g.reshape(1, C), ln_b.reshape(1, C))


def _lambda_value(lam_ref, lam_init):
    s1 = jnp.sum(lam_ref[0:1, :] * lam_ref[1:2, :], axis=1, keepdims=True)
    s2 = jnp.sum(lam_ref[2:3, :] * lam_ref[3:4, :], axis=1, keepdims=True)
    return jnp.exp(s1) - jnp.exp(s2) + lam_init


def _split_maps(q):
    lane = lax.broadcasted_iota(jnp.int32, q.shape, 1)
    zero = jnp.zeros_like(q)
    return jnp.where(lane < ATT_HEAD_DIM, q, zero), jnp.where(lane >= ATT_HEAD_DIM, q, zero)


def _finish_heads(o0, o1, lam, g, lam_init):
    o = o0 - lam * o1
    o = o * lax.rsqrt(jnp.mean(o * o, axis=-1, keepdims=True) + LN_EPS) * g
    return o * (1.0 - lam_init)


def _attn_prompt_kernel(lam_ref, g_ref, q_ref, k_ref, vt_ref, o_ref, m_sc, l_sc, acc_sc, *, lam_init):
    i = pl.program_id(2)
    tq = q_ref.shape[1]
    q2 = jnp.concatenate(_split_maps(q_ref[0]), axis=0)
    m_sc[...] = jnp.full(m_sc.shape, -jnp.inf, F32)
    l_sc[...] = jnp.zeros(l_sc.shape, F32)
    acc_sc[...] = jnp.zeros(acc_sc.shape, F32)

    kpos = lax.broadcasted_iota(jnp.int32, (tq, 2 * tq), 0)
    qpos = lax.broadcasted_iota(jnp.int32, (tq, 2 * tq), 1) % tq
    allowed = (kpos // CHUNK) <= (qpos // CHUNK)

    def scores(j):
        kb = k_ref[0, pl.ds(pl.multiple_of(j * tq, tq), tq), :]
        return lax.dot_general(kb, q2, _NT, preferred_element_type=F32)

    def update(s, j):
        vt = vt_ref[0, :, pl.ds(pl.multiple_of(j * tq, tq), tq)]
        m_prev = m_sc[...]
        m_new = jnp.maximum(m_prev, jnp.max(s, axis=0, keepdims=True))
        alpha = jnp.exp(m_prev - m_new)
        pe = jnp.exp(s - m_new)
        l_sc[...] = alpha * l_sc[...] + jnp.sum(pe, axis=0, keepdims=True)
        acc_sc[...] = alpha * acc_sc[...] + jnp.dot(vt, pe.astype(BF16), preferred_element_type=F32)
        m_sc[...] = m_new

    def pair(jj, c):
        s_a, s_b = scores(2 * jj), scores(2 * jj + 1)
        update(s_a, 2 * jj)
        update(s_b, 2 * jj + 1)
        return c

    lax.fori_loop(0, i // 2, pair, 0)

    @pl.when(i % 2 == 1)
    def _():
        s_a, s_b = scores(i - 1), jnp.where(allowed, scores(i), -jnp.inf)
        update(s_a, i - 1)
        update(s_b, i)

    @pl.when(i % 2 == 0)
    def _():
        update(jnp.where(allowed, scores(i), -jnp.inf), i)

    lam = _lambda_value(lam_ref, lam_init)
    o = acc_sc[...] / l_sc[...]
    o = o[:, :tq] - lam * o[:, tq:]
    o = o * lax.rsqrt(jnp.mean(o * o, axis=0, keepdims=True) + LN_EPS) * g_ref[...]
    o_ref[0] = jnp.transpose(o * (1.0 - lam_init)).astype(BF16)


def attn_prompt(qb, kb, vt, lam_rows, subln_g, lam_init, tq):
    B, L, _ = qb.shape
    return pl.pallas_call(
        functools.partial(_attn_prompt_kernel, lam_init=lam_init),
        grid=(B, ATT_HEADS, L // tq),
        in_specs=[pl.BlockSpec((4, ATT_HEAD_DIM), lambda b, h, i: (0, 0)),
                  pl.BlockSpec((ATT_V_DIM, 1), lambda b, h, i: (0, 0)),
                  pl.BlockSpec((1, tq, LANES), lambda b, h, i: (b, i, h)),
                  pl.BlockSpec((1, L, LANES), lambda b, h, i: (b, 0, h)),
                  pl.BlockSpec((1, ATT_V_DIM, L), lambda b, h, i: (b, h, 0))],
        out_specs=pl.BlockSpec((1, tq, LANES), lambda b, h, i: (b, i, h)),
        out_shape=jax.ShapeDtypeStruct((B, L, ATT_HEADS * ATT_V_DIM), BF16),
        scratch_shapes=[pltpu.VMEM((1, 2 * tq), F32), pltpu.VMEM((1, 2 * tq), F32),
                        pltpu.VMEM((ATT_V_DIM, 2 * tq), F32)],
        compiler_params=_cparams(3),
        name="attn_prompt",
    )(lam_rows, subln_g.reshape(ATT_V_DIM, 1), qb, kb, vt)


def _attn_sample_kernel(lam_ref, g_ref, q_ref, ck_ref, cv_ref, kn_ref, vn_ref, o_ref, *, lam_init, past):
    ls = q_ref.shape[1]
    qs = _split_maps(q_ref[0])
    kp = ck_ref[0].astype(BF16)
    vp = cv_ref[0].astype(BF16)
    kn = kn_ref[0]
    vn = vn_ref[0]
    qpos = past + lax.broadcasted_iota(jnp.int32, (ls, ls), 0)
    kpos = past + lax.broadcasted_iota(jnp.int32, (ls, ls), 1)
    allowed = (kpos // CHUNK) <= (qpos // CHUNK)
    outs = []
    for p in range(2):
        sp = lax.dot_general(qs[p], kp, _NT, preferred_element_type=F32)
        sn = lax.dot_general(qs[p], kn, _NT, preferred_element_type=F32)
        sn = jnp.where(allowed, sn, -jnp.inf)
        m = jnp.maximum(jnp.max(sp, axis=1, keepdims=True), jnp.max(sn, axis=1, keepdims=True))
        pp = jnp.exp(sp - m)
        pn = jnp.exp(sn - m)
        denom = jnp.sum(pp, axis=1, keepdims=True) + jnp.sum(pn, axis=1, keepdims=True)
        acc = (jnp.dot(pp.astype(BF16), vp, preferred_element_type=F32)
               + jnp.dot(pn.astype(BF16), vn, preferred_element_type=F32))
        outs.append(acc / denom)
    lam = _lambda_value(lam_ref, lam_init)
    o_ref[0] = _finish_heads(outs[0], outs[1], lam, g_ref[...], lam_init).astype(BF16)


def attn_sample(qb, cache_k, cache_v, kb, vb, lam_rows, subln_g, lam_init):
    B, Ls, _ = qb.shape
    P = cache_k.shape[1]
    blk = lambda n: pl.BlockSpec((1, n, LANES), lambda b, h: (b, 0, h))
    return pl.pallas_call(
        functools.partial(_attn_sample_kernel, lam_init=lam_init, past=P),
        grid=(B, ATT_HEADS),
        in_specs=[pl.BlockSpec((4, ATT_HEAD_DIM), lambda b, h: (0, 0)),
                  pl.BlockSpec((1, ATT_V_DIM), lambda b, h: (0, 0)),
                  blk(Ls), blk(P), blk(P), blk(Ls), blk(Ls)],
        out_specs=blk(Ls),
        out_shape=jax.ShapeDtypeStruct((B, Ls, ATT_HEADS * ATT_V_DIM), BF16),
        compiler_params=_cparams(2),
        name="attn_sample",
    )(lam_rows, subln_g.reshape(1, ATT_V_DIM), qb, cache_k, cache_v, kb, vb)


def _outproj_kernel(conv_ref, att_ref, x_ref, g1_ref, sh2_ref, sc2_ref, wc_ref, wa_ref, b_ref,
                    lg_ref, lb_ref, x1_ref, hh_ref, h2b_ref, *, alpha):
    bb, tl, d = x_ref.shape
    conv = conv_ref[...].reshape(bb * tl, conv_ref.shape[2])
    att = att_ref[...].reshape(bb * tl, att_ref.shape[2])
    mm = (jnp.dot(conv, wc_ref[...], preferred_element_type=F32)
          + jnp.dot(att, wa_ref[...], preferred_element_type=F32) + b_ref[...])
    y = alpha * x_ref[...] + g1_ref[...] * mm.reshape(bb, tl, d)
    x1 = _ln(y, lg_ref[...], lb_ref[...])
    x1_ref[...] = x1
    h2 = x1 * (1.0 + sc2_ref[...]) + sh2_ref[...]
    h2b_ref[...] = h2.astype(BF16)
    hi, lo = _split_bf16(h2.reshape(bb * tl, d))
    bits = [lax.bitcast_convert_type(part.astype(F32), jnp.uint32) for part in (hi, lo)]
    rows = [b[:, c * LANES:(c + 1) * LANES] for b in bits for c in range(d // LANES)]
    for i in range(len(rows) // 2):
        hh_ref[:, i, :] = (rows[2 * i] >> 16) | rows[2 * i + 1]


def out_proj(conv_out, att, x, g1, sh2, sc2, wc_bf, wa_bf, b_out, ln_g, ln_b, alpha, bb, tl):
    B, L, D = x.shape
    tok = lambda b, l: (b, l, 0)
    mod = lambda b, l: (b, 0, 0)
    const = lambda b, l: (0, 0)
    return pl.pallas_call(
        functools.partial(_outproj_kernel, alpha=alpha),
        grid=(B // bb, L // tl),
        in_specs=[pl.BlockSpec((bb, tl, conv_out.shape[2]), tok),
                  pl.BlockSpec((bb, tl, att.shape[2]), tok),
                  pl.BlockSpec((bb, tl, D), tok),
                  pl.BlockSpec((bb, 1, D), mod), pl.BlockSpec((bb, 1, D), mod), pl.BlockSpec((bb, 1, D), mod),
                  pl.BlockSpec(wc_bf.shape, const), pl.BlockSpec(wa_bf.shape, const),
                  pl.BlockSpec((1, D), const), pl.BlockSpec((1, D), const), pl.BlockSpec((1, D), const)],
        out_specs=[pl.BlockSpec((bb, tl, D), tok),
                   pl.BlockSpec((bb * tl, D // LANES, LANES), lambda b, l: (b * (L // tl) + l, 0, 0)),
                   pl.BlockSpec((bb, tl, D), tok)],
        out_shape=[jax.ShapeDtypeStruct((B, L, D), F32),
                   jax.ShapeDtypeStruct((B * L, D // LANES, LANES), jnp.uint32),
                   jax.ShapeDtypeStruct((B, L, D), BF16)],
        compiler_params=_cparams(2),
        name="out_proj",
    )(conv_out, att, x, g1, sh2, sc2, wc_bf, wa_bf, b_out.reshape(1, D), ln_g.reshape(1, D), ln_b.reshape(1, D))


TOPK_STREAMS = 2


def _extract_top(vals, order, count, out, payload=None):
    big = jnp.float32(1e9)
    out_v, out_p = [], []
    for _ in range(count):
        m = jnp.max(vals, axis=0, keepdims=True)
        pos = jnp.min(jnp.where(vals == m, order, big), axis=0, keepdims=True)
        sel = order == pos
        out_v.append(m)
        out_p.append(pos if payload is None else jnp.max(jnp.where(sel, payload, -1.0), axis=0, keepdims=True))
        vals = jnp.where(sel, -jnp.inf, vals)
        yield
    out.append((jnp.concatenate(out_v, axis=0), jnp.concatenate(out_p, axis=0)))


def _interleave(streams):
    done = [0] * len(streams)
    total = max(n for _, n in streams)
    for tick in range(1, total + 1):
        for s, (gen, n) in enumerate(streams):
            while done[s] * total < tick * n:
                next(gen, None)
                done[s] += 1
    for gen, _ in streams:
        for _ in gen:
            pass


class _TopkWork:
    def __init__(self, h_ref, wq_ref, keys_ref, q_sc, e_sc, g_sc, max_streams=TOPK_STREAMS):
        self.keys_ref, self.q_sc, self.e_sc, self.g_sc = keys_ref, q_sc, e_sc, g_sc
        self.n_hp, self.n_keys, _ = keys_ref.shape
        tl = h_ref.shape[0]
        K = PEER_TOPK
        qp = jnp.dot(h_ref[...], wq_ref[...], preferred_element_type=F32)
        for hp in range(self.n_hp):
            q_sc[hp] = qp[:, hp * LANES:(hp + 1) * LANES].astype(BF16)
        self.key_iota = lax.broadcasted_iota(jnp.int32, (self.n_keys, LANES), 0).astype(F32)
        n_cand = K + (K - 1) * SUBLANES
        row = lax.broadcasted_iota(jnp.int32, (n_cand, LANES), 0)
        ci = jnp.where(row < K, 0, ((row - K) >> 3) + 1)
        cj = jnp.where(row < K, row, (row - K) & 7)
        limit = jnp.where(ci == 0, 16, jnp.where(ci == 1, 8, jnp.where(ci == 2, 5, jnp.where(
            ci == 3, 4, jnp.where(ci == 4, 3, jnp.where(ci <= 7, 2, 1))))))
        self.cand_ok = cj < limit
        self.cand_order = (ci * K + cj).astype(F32)
        n_chunks = tl // LANES
        self.streams = max_streams if n_chunks % max_streams == 0 else 1
        self.groups = n_chunks // self.streams
        self.n_steps = (self.n_hp // 2) * self.groups

    CHUNK_STEPS = 4 * PEER_TOPK - 1

    def head_chunk(self, h, col):
        K = PEER_TOPK
        halves = []
        for p in range(2):
            qc = self.q_sc[2 * h + p, pl.ds(col, LANES), :]
            s = lax.dot_general(self.keys_ref[2 * h + p], qc, _NT, preferred_element_type=F32)
            yield from _extract_top(s, self.key_iota, K, halves)
        (sv0, si0), (sv1, si1) = halves
        cv = [sv0[0:1] + sv1]
        ce = [si0[0:1] * self.n_keys + si1]
        for r in range(1, K):
            cv.append(sv0[r:r + 1] + sv1[0:SUBLANES])
            ce.append(si0[r:r + 1] * self.n_keys + si1[0:SUBLANES])
            yield
        cand = jnp.where(self.cand_ok, jnp.concatenate(cv, axis=0), -jnp.inf)
        final = []
        yield from _extract_top(cand, self.cand_order, K, final, payload=jnp.concatenate(ce, axis=0))
        fv, fe = final[0]
        ex = jnp.exp(fv - fv[0:1])
        gate = ex / jnp.sum(ex, axis=0, keepdims=True)
        off = pl.multiple_of(h * K, K)
        self.e_sc[pl.ds(off, K), pl.ds(col, LANES)] = fe
        self.g_sc[pl.ds(off, K), pl.ds(col, LANES)] = gate

    def step_streams(self, i):
        h = i // self.groups
        return [(self.head_chunk(h, pl.multiple_of(((i % self.groups) * self.streams + c) * LANES, LANES)),
                 self.CHUNK_STEPS) for c in range(self.streams)]

    def finish(self, e_ref, g_ref):
        e_ref[...] = jnp.transpose(self.e_sc[...]).astype(jnp.int32) * TABLE_ROWS_PER_EXPERT
        g_ref[...] = jnp.transpose(self.g_sc[...])


def _topk_kernel(h_ref, wq_ref, keys_ref, e_ref, g_ref, q_sc, e_sc, g_sc):
    work = _TopkWork(h_ref, wq_ref, keys_ref, q_sc, e_sc, g_sc)

    def step(i, carry):
        _interleave(work.step_streams(i))
        return carry

    lax.fori_loop(0, work.n_steps, step, 0)
    work.finish(e_ref, g_ref)


def _topk_specs(D, wq_bf, keys_bf, tl, blk0):
    n_hp, n_keys, half = keys_bf.shape
    slots = (n_hp // 2) * PEER_TOPK
    assert half == LANES and n_keys == LANES and PEER_TOPK * PEER_TOPK <= 256
    in_specs = [pl.BlockSpec((tl, D), lambda i: (i + blk0, 0)),
                pl.BlockSpec(wq_bf.shape, lambda i: (0, 0), pipeline_mode=pl.Buffered(1)),
                pl.BlockSpec(keys_bf.shape, lambda i: (0, 0, 0), pipeline_mode=pl.Buffered(1))]
    out_specs = [pl.BlockSpec((tl, slots), lambda i: (i, 0))] * 2
    scratch = [pltpu.VMEM((n_hp, tl, LANES), BF16), pltpu.VMEM((slots, tl), F32), pltpu.VMEM((slots, tl), F32)]
    return slots, in_specs, out_specs, scratch


def peer_topk(h2b, wq_bf, keys_bf, tl, blk0=0, nblk=None):
    T, D = h2b.shape
    nblk = T // tl if nblk is None else nblk
    slots, in_specs, out_specs, scratch = _topk_specs(D, wq_bf, keys_bf, tl, blk0)
    return pl.pallas_call(
        _topk_kernel,
        grid=(nblk,),
        in_specs=in_specs,
        out_specs=out_specs,
        out_shape=[jax.ShapeDtypeStruct((nblk * tl, slots), jnp.int32),
                   jax.ShapeDtypeStruct((nblk * tl, slots), F32)],
        scratch_shapes=scratch,
        compiler_params=_cparams(1),
        name="peer_topk",
    )(h2b, wq_bf, keys_bf)


TABLE_ROWS_PER_EXPERT = 4
PEER_TOKEN_BLOCK = 128
PEER_TOKEN_UNROLL = 16
GATHER_ID_WINDOW = 16


PACK_EXPERT_BLOCK = 512


def _pack_kernel(t_ref, o_ref):
    n = t_ref.shape[0]
    bits = lax.bitcast_convert_type(t_ref[...].astype(BF16).astype(F32), jnp.uint32)
    for i in range(TABLE_ROWS_PER_EXPERT):
        lo = bits[:, (2 * i) * LANES:(2 * i + 1) * LANES]
        hi = bits[:, (2 * i + 1) * LANES:(2 * i + 2) * LANES]
        o_ref[pl.ds(i, n, stride=TABLE_ROWS_PER_EXPERT), :] = (lo >> 16) | hi


def pack_table(tab):
    n, d = tab.shape
    assert d == 2 * LANES * TABLE_ROWS_PER_EXPERT
    tn = min(n, PACK_EXPERT_BLOCK)
    return pl.pallas_call(
        _pack_kernel,
        grid=(n // tn,),
        in_specs=[pl.BlockSpec((tn, d), lambda i: (i, 0))],
        out_specs=pl.BlockSpec((tn * TABLE_ROWS_PER_EXPERT, LANES), lambda i: (i, 0)),
        out_shape=jax.ShapeDtypeStruct((n * TABLE_ROWS_PER_EXPERT, LANES), jnp.uint32),
        compiler_params=_cparams(1),
        name="pack_table",
    )(tab)


def _gather_rows(e_ref, tab_ref, base, slots):
    rows = []
    for j0 in range(0, slots, GATHER_ID_WINDOW):
        ids = e_ref.at[pl.ds(pl.multiple_of(base + j0, GATHER_ID_WINDOW), GATHER_ID_WINDOW)]
        for j in range(GATHER_ID_WINDOW):
            start = pl.multiple_of(ids[j], TABLE_ROWS_PER_EXPERT)
            rows.append(pltpu.bitcast(tab_ref[pl.ds(start, TABLE_ROWS_PER_EXPERT), :], BF16))
    return jnp.concatenate(rows, axis=0)


def _split_bf16(x):
    hi = x.astype(BF16)
    return hi, (x - hi.astype(F32)).astype(BF16)


def _chunk_mask(slots):
    lane = lax.broadcasted_iota(jnp.int32, (SUBLANES, slots * SUBLANES), 1)
    sub = lax.broadcasted_iota(jnp.int32, (SUBLANES, slots * SUBLANES), 0)
    return (lane & (SUBLANES - 1)) == sub


def _slot_spread(slots):
    r = lax.broadcasted_iota(jnp.int32, (slots, slots * SUBLANES), 0)
    c = lax.broadcasted_iota(jnp.int32, (slots, slots * SUBLANES), 1)
    return jnp.where((c >> 3) == r, 1.0, 0.0).astype(BF16)


def _peer_a_tokens(count, t0, z0, e_ref, hh_ref, tab_ref, z_sc, mask, slots):
    for u in range(count):
        t = t0 + u
        g = _gather_rows(e_ref, tab_ref, pl.multiple_of(t * slots, slots), slots)
        hh = pltpu.bitcast(hh_ref[t], BF16)
        z = lax.dot_general(hh, g, _NT, preferred_element_type=F32)
        z8 = z[0:SUBLANES] + z[SUBLANES:2 * SUBLANES]
        z_sc[pl.ds(z0 + u, 1), :] = jnp.sum(jnp.where(mask, z8, 0.0), axis=0, keepdims=True)
        yield


def _peer_a_weights(z_sc, gate, slots):
    zhi, zlo = _split_bf16(z_sc[...])
    spread = _slot_spread(slots)
    a = (lax.dot_general(zhi, spread, _NT, preferred_element_type=F32)
         + lax.dot_general(zlo, spread, _NT, preferred_element_type=F32))
    return 0.5 * a * (1.0 + lax.erf(a * (2.0 ** -0.5))) * gate


def _peer_a_kernel(e_ref, hh_ref, g_ref, tab_ref, w_ref, z_sc, *, slots):
    tb = hh_ref.shape[0]
    mask = _chunk_mask(slots)

    def tokens(i, carry):
        t0 = i * PEER_TOKEN_UNROLL
        for _ in _peer_a_tokens(PEER_TOKEN_UNROLL, t0, t0, e_ref, hh_ref, tab_ref, z_sc, mask, slots):
            pass
        return carry

    lax.fori_loop(0, tb // PEER_TOKEN_UNROLL, tokens, 0)
    w_ref[...] = _peer_a_weights(z_sc, g_ref[...], slots)


def _peer_a_specs(tab_u32, slots, tb, blk0):
    in_specs = [pl.BlockSpec((tb * slots,), lambda i: (i,), memory_space=pltpu.SMEM),
                pl.BlockSpec((tb, SUBLANES, LANES), lambda i: (i + blk0, 0, 0)),
                pl.BlockSpec((tb, slots), lambda i: (i, 0)),
                pl.BlockSpec(tab_u32.shape, lambda i: (0, 0), pipeline_mode=pl.Buffered(1))]
    scratch = [pltpu.VMEM((PEER_TOKEN_BLOCK, slots * SUBLANES), F32)]
    return in_specs, pl.BlockSpec((tb, slots), lambda i: (i, 0)), scratch


def peer_a(e_flat, hh, gate, tab_u32, slots, blk0=0):
    T = gate.shape[0]
    tb = PEER_TOKEN_BLOCK
    in_specs, out_spec, scratch = _peer_a_specs(tab_u32, slots, tb, blk0)
    return pl.pallas_call(
        functools.partial(_peer_a_kernel, slots=slots),
        grid=(T // tb,),
        in_specs=in_specs,
        out_specs=out_spec,
        out_shape=jax.ShapeDtypeStruct((T, slots), F32),
        scratch_shapes=scratch,
        compiler_params=_cparams(1, VMEM_LIMIT_TABLE),
        name="peer_a",
    )(e_flat, hh, gate, tab_u32)


def _topk_a_kernel(e_ref, hh_ref, gate_ref, tab_ref, h_ref, wq_ref, keys_ref, w_ref, e_out_ref, g_out_ref,
                   z_sc, q_sc, e_sc, g_sc, *, slots):
    tl = hh_ref.shape[0]
    work = _TopkWork(h_ref, wq_ref, keys_ref, q_sc, e_sc, g_sc, max_streams=1)
    mask = _chunk_mask(slots)
    per_step = tl // work.n_steps
    flush = PEER_TOKEN_BLOCK // per_step
    assert per_step % PEER_TOKEN_UNROLL == 0 and flush * per_step == PEER_TOKEN_BLOCK

    def step(i, carry):
        gathers = _peer_a_tokens(per_step, i * per_step, (i % flush) * per_step, e_ref, hh_ref, tab_ref, z_sc,
                                 mask, slots)
        _interleave(work.step_streams(i) + [(gathers, per_step)])

        @pl.when(i % flush == flush - 1)
        def _():
            rows = pl.ds(pl.multiple_of((i // flush) * PEER_TOKEN_BLOCK, PEER_TOKEN_BLOCK), PEER_TOKEN_BLOCK)
            w_ref[rows, :] = _peer_a_weights(z_sc, gate_ref[rows, :], slots)

        return carry

    lax.fori_loop(0, work.n_steps, step, 0)
    work.finish(e_out_ref, g_out_ref)


def peer_topk_a(e_flat, hh, gate, tab_u32, h2b, wq_bf, keys_bf, tl, a_blk0, k_blk0):
    T, D = gate.shape[0], h2b.shape[1]
    slots, k_in, k_out, k_scratch = _topk_specs(D, wq_bf, keys_bf, tl, k_blk0)
    a_in, a_out, a_scratch = _peer_a_specs(tab_u32, slots, tl, a_blk0)
    return pl.pallas_call(
        functools.partial(_topk_a_kernel, slots=slots),
        grid=(T // tl,),
        in_specs=a_in + k_in,
        out_specs=[a_out] + k_out,
        out_shape=[jax.ShapeDtypeStruct((T, slots), F32), jax.ShapeDtypeStruct((T, slots), jnp.int32),
                   jax.ShapeDtypeStruct((T, slots), F32)],
        scratch_shapes=a_scratch + k_scratch,
        compiler_params=_cparams(1, VMEM_LIMIT_FUSED),
        name="peer_topk_a",
    )(e_flat, hh, gate, tab_u32, h2b, wq_bf, keys_bf)


def _peer_b_kernel(e_ref, w_ref, tab_ref, o_ref, rep_sc, *, slots):
    tb = o_ref.shape[0]
    spread = _slot_spread(slots)
    whi, wlo = _split_bf16(w_ref[...])
    rep_sc[0] = jnp.dot(whi, spread, preferred_element_type=F32)
    rep_sc[1] = jnp.dot(wlo, spread, preferred_element_type=F32)
    mask = _chunk_mask(slots)
    wide = (SUBLANES, slots * SUBLANES)

    def tokens(i, carry):
        for u in range(PEER_TOKEN_UNROLL):
            t = i * PEER_TOKEN_UNROLL + u
            g = _gather_rows(e_ref, tab_ref, pl.multiple_of(t * slots, slots), slots)
            parts = [jnp.where(mask, jnp.broadcast_to(rep_sc[p, pl.ds(t, 1), :], wide), 0.0).astype(BF16)
                     for p in range(2)]
            r = jnp.dot(jnp.concatenate(parts, axis=0), g, preferred_element_type=F32)
            o_ref[t] = r[0:SUBLANES] + r[SUBLANES:2 * SUBLANES]
        return carry

    lax.fori_loop(0, tb // PEER_TOKEN_UNROLL, tokens, 0)


def peer_b(e_flat, w, tab_u32, slots):
    T = w.shape[0]
    tb = PEER_TOKEN_BLOCK
    return pl.pallas_call(
        functools.partial(_peer_b_kernel, slots=slots),
        grid=(T // tb,),
        in_specs=[pl.BlockSpec((tb * slots,), lambda i: (i,), memory_space=pltpu.SMEM),
                  pl.BlockSpec((tb, slots), lambda i: (i, 0)),
                  pl.BlockSpec(tab_u32.shape, lambda i: (0, 0), pipeline_mode=pl.Buffered(1))],
        out_specs=pl.BlockSpec((tb, SUBLANES, LANES), lambda i: (i, 0, 0)),
        out_shape=jax.ShapeDtypeStruct((T, SUBLANES, LANES), F32),
        scratch_shapes=[pltpu.VMEM((2, tb, slots * SUBLANES), F32)],
        compiler_params=_cparams(1, VMEM_LIMIT_TABLE),
        name="peer_b",
    )(e_flat, w, tab_u32)


def _final_kernel(x_ref, ff_ref, g2_ref, lg_ref, lb_ref, o_ref, *, alpha):
    bb, tl, d = x_ref.shape
    ff = jnp.concatenate([ff_ref[:, c, :] for c in range(d // LANES)], axis=1)
    y = alpha * x_ref[...] + g2_ref[...] * ff.reshape(bb, tl, d)
    o_ref[...] = _ln(y, lg_ref[...], lb_ref[...])


def final_ln(x1, ff, g2, ln_g, ln_b, alpha, bb, tl):
    B, L, D = x1.shape
    tok = lambda b, l: (b, l, 0)
    const = lambda b, l: (0, 0)
    return pl.pallas_call(
        functools.partial(_final_kernel, alpha=alpha),
        grid=(B // bb, L // tl),
        in_specs=[pl.BlockSpec((bb, tl, D), tok),
                  pl.BlockSpec((bb * tl, D // LANES, LANES), lambda b, l: (b * (L // tl) + l, 0, 0)),
                  pl.BlockSpec((bb, 1, D), lambda b, l: (b, 0, 0)),
                  pl.BlockSpec((1, D), const), pl.BlockSpec((1, D), const)],
        out_specs=pl.BlockSpec((bb, tl, D), tok),
        out_shape=jax.ShapeDtypeStruct((B, L, D), F32),
        compiler_params=_cparams(2),
        name="final_ln",
    )(x1, ff, g2, ln_g.reshape(1, D), ln_b.reshape(1, D))


def _rope_tables(past, length, n_groups):
    half = ATT_HEAD_DIM // 2
    inv = 1.0 / (ROPE_THETA ** (jnp.arange(half, dtype=F32) / half))
    ang = (past + jnp.arange(length)).astype(F32)[:, None] * inv[None, :]
    cos = jnp.cos(ang)
    sin = jnp.sin(ang)
    cos_t = jnp.tile(jnp.concatenate([cos, cos], axis=-1), (1, n_groups))
    sin_t = jnp.tile(jnp.concatenate([-sin, sin], axis=-1), (1, n_groups))
    return cos_t, sin_t


def _token_tiles(B, L, target):
    if L >= target:
        return 1, target
    bb = max(1, min(B, target // L))
    while B % bb:
        bb -= 1
    return bb, L


TOPK_TOKEN_BLOCK = 512
PEER_MAX_SPLIT = 8


def _peer_select_and_weigh(h2b, hh, wts, slots):
    T = h2b.shape[0]
    tl = min(T, TOPK_TOKEN_BLOCK)
    wq, keys, tab = wts["w_query"], wts["sub_keys"], wts["u_tab"]
    split = max(s for s in range(1, PEER_MAX_SPLIT + 1) if T % (s * tl) == 0)
    if split == 1 or tl % PEER_TOKEN_BLOCK:
        e, gate = peer_topk(h2b, wq, keys, tl)
        return e, peer_a(e.reshape(T * slots), hh, gate, tab, slots)
    nblk = T // (split * tl)
    chunk = nblk * tl
    e_c, gate_c = peer_topk(h2b, wq, keys, tl, 0, nblk)
    es, ws = [e_c], []
    for c in range(1, split):
        w_c, e_n, gate_n = peer_topk_a(e_c.reshape(chunk * slots), hh, gate_c, tab, h2b, wq, keys, tl,
                                       (c - 1) * nblk, c * nblk)
        ws.append(w_c)
        es.append(e_n)
        e_c, gate_c = e_n, gate_n
    ws.append(peer_a(e_c.reshape(chunk * slots), hh, gate_c, tab, slots, (split - 1) * chunk // PEER_TOKEN_BLOCK))
    return jnp.concatenate(es, axis=0), jnp.concatenate(ws, axis=0)


def _layer(x, mod, conv_hist, k_past, v_past, wts, lam_init, alpha):
    B, L, D = x.shape
    conv_dim = wts["w_dw"].shape[1]
    qk_dim = ATT_HEADS * 2 * ATT_HEAD_DIM
    att_dim = ATT_HEADS * ATT_V_DIM
    width = wts["w_dw"].shape[0]
    assert L >= width - 1 and width - 1 <= HIST_PAD
    sh1, sc1, g1, sh2, sc2, g2 = [m[:, None, :] for m in jnp.split(mod, 6, axis=-1)]
    past = 0 if k_past is None else k_past.shape[1]
    cos, sin = _rope_tables(past, L, qk_dim // ATT_HEAD_DIM)
    bb, tl = _token_tiles(B, L, 512)

    u, k, v, qb, kb, vb = in_proj(x, sh1, sc1, wts["w_in"], wts["b_in"], cos, sin, bb, tl,
                                  conv_dim, qk_dim, att_dim, transposed_v=k_past is None)
    hist_pad = jnp.pad(conv_hist, ((0, 0), (HIST_PAD - (width - 1), 0), (0, 0)))
    conv_out = conv_ln(u, hist_pad, wts["w_dw"], wts["b_dw"], wts["conv_ln_g"], wts["conv_ln_b"], min(L, 256))
    conv_state = u[:, L - (width - 1):, :]

    if k_past is None:
        att = attn_prompt(qb, kb, vb, wts["lam_rows"], wts["subln_g"], lam_init, min(L, 256))
    else:
        att = attn_sample(qb, k_past.reshape(B, past, qk_dim), v_past.reshape(B, past, att_dim), kb, vb,
                          wts["lam_rows"], wts["subln_g"], lam_init)

    x1, hh, h2b = out_proj(conv_out, att, x, g1, sh2, sc2, wts["w_out_conv"], wts["w_out_att"], wts["b_out"],
                           wts["ln1_g"], wts["ln1_b"], alpha, bb, tl)

    T = B * L
    slots = (wts["sub_keys"].shape[0] // 2) * PEER_TOPK
    e, w = _peer_select_and_weigh(h2b.reshape(T, D), hh, wts, slots)
    ff = peer_b(e.reshape(T * slots), w, wts["v_tab"], slots)
    out = final_ln(x1, ff, g2, wts["ln2_g"], wts["ln2_b"], alpha, bb, tl)
    return out, conv_state, k, v


def kernel(x_prompt, x_sample, cache_k, cache_v, cache_conv, c_prompt, c_sample, w_ada, b_ada, w_in, b_in, w_dw, b_dw, conv_ln_g, conv_ln_b, lam_q1, lam_k1, lam_q2, lam_k2, subln_g, w_out, b_out, ln1_g, ln1_b, w_query, sub_keys, u_tab, v_tab, ln2_g, ln2_b):
    depth = w_ada.shape[0]
    D = x_prompt.shape[-1]
    assert D == SUBLANES * LANES
    alpha = (2 * depth) ** 0.25
    Bp, Bs = c_prompt.shape[0], c_sample.shape[0]
    xp, xs = x_prompt, x_sample
    outs = [[] for _ in range(6)]
    for l in range(depth):
        lam_init = 0.8 - 0.6 * math.exp(-0.3 * l)
        conv_dim = w_dw.shape[2]
        n_exp = u_tab.shape[1]
        wts = {
            "w_in": w_in[l].astype(BF16), "b_in": b_in[l], "w_dw": w_dw[l], "b_dw": b_dw[l],
            "conv_ln_g": conv_ln_g[l], "conv_ln_b": conv_ln_b[l],
            "lam_rows": jnp.stack([lam_q1[l], lam_k1[l], lam_q2[l], lam_k2[l]]).astype(F32),
            "subln_g": subln_g[l],
            "w_out_conv": w_out[l, :conv_dim].astype(BF16), "w_out_att": w_out[l, conv_dim:].astype(BF16),
            "b_out": b_out[l], "ln1_g": ln1_g[l], "ln1_b": ln1_b[l],
            "w_query": w_query[l].astype(BF16),
            "sub_keys": sub_keys[l].astype(BF16).reshape(-1, sub_keys.shape[3], sub_keys.shape[4]),
            "u_tab": pack_table(u_tab[l]), "v_tab": pack_table(v_tab[l]),
            "ln2_g": ln2_g[l], "ln2_b": ln2_b[l],
        }
        c_all = jnp.concatenate([c_prompt, c_sample], axis=0)
        pad = (-c_all.shape[0]) % 16
        c_all = jnp.pad(c_all, ((0, pad), (0, 0)))
        mod = ada_mod(c_all, w_ada[l].astype(BF16), b_ada[l])
        hist0 = jnp.zeros((Bp, w_dw.shape[1] - 1, conv_dim), xp.dtype)
        xp, cp, kp, vp = _layer(xp, mod[:Bp], hist0, None, None, wts, lam_init, alpha)
        xs, cs, kn, vn = _layer(xs, mod[Bp:Bp + Bs], cache_conv[l], cache_k[l], cache_v[l], wts, lam_init, alpha)
        for lst, val in zip(outs, (kp, vp, cp, kn, vn, cs)):
            lst.append(val)
    Lp, Ls = x_prompt.shape[1], x_sample.shape[1]
    k_prompt = jnp.stack(outs[0]).reshape(depth, Bp, Lp, ATT_HEADS, 2, ATT_HEAD_DIM)
    v_prompt = jnp.stack(outs[1]).reshape(depth, Bp, Lp, ATT_HEADS, ATT_V_DIM)
    conv_prompt = jnp.stack(outs[2])
    k_sample = jnp.stack(outs[3]).reshape(depth, Bs, Ls, ATT_HEADS, 2, ATT_HEAD_DIM)
    v_sample = jnp.stack(outs[4]).reshape(depth, Bs, Ls, ATT_HEADS, ATT_V_DIM)
    conv_sample = jnp.stack(outs[5])
    return (xp, xs, k_prompt, v_prompt, conv_prompt, k_sample, v_sample, conv_sample)
```

```python
import functools
import math

import jax
import jax.numpy as jnp
import numpy as np
from jax import lax
from jax.experimental import pallas as pl
from jax.experimental.pallas import tpu as pltpu

F32 = jnp.float32
BF16 = jnp.bfloat16

CHUNK = 64
ATT_HEADS = 4
ATT_HEAD_DIM = 64
ATT_V_DIM = 2 * ATT_HEAD_DIM
ROPE_THETA = 10000.0
PEER_TOPK = 16
LN_EPS = 1e-5

LANES = 128
SUBLANES = 8
VMEM_LIMIT_DEFAULT = 48 * 1024 * 1024
VMEM_LIMIT_TABLE = 56 * 1024 * 1024
VMEM_LIMIT_FUSED = 60 * 1024 * 1024

_NT = (((1,), (1,)), ((), ()))


def _cparams(n_axes, vmem=VMEM_LIMIT_DEFAULT):
    return pltpu.CompilerParams(dimension_semantics=("arbitrary",) * n_axes, vmem_limit_bytes=vmem)


def _ln(y, g, b):
    mu = jnp.mean(y, axis=-1, keepdims=True)
    d = y - mu
    var = jnp.mean(d * d, axis=-1, keepdims=True)
    return d * lax.rsqrt(var + LN_EPS) * g + b


def _ada_kernel(c_ref, w_ref, b_ref, o_ref):
    c = c_ref[...]
    s = (c * jax.nn.sigmoid(c)).astype(BF16)
    o_ref[...] = jnp.dot(s, w_ref[...], preferred_element_type=F32) + b_ref[...]


def ada_mod(c, w_bf, b):
    rows, d = c.shape
    n = w_bf.shape[1]
    tn = min(n, 1024)
    return pl.pallas_call(
        _ada_kernel,
        grid=(n // tn,),
        in_specs=[pl.BlockSpec((rows, d), lambda j: (0, 0)),
                  pl.BlockSpec((d, tn), lambda j: (0, j)),
                  pl.BlockSpec((1, tn), lambda j: (0, j))],
        out_specs=pl.BlockSpec((rows, tn), lambda j: (0, j)),
        out_shape=jax.ShapeDtypeStruct((rows, n), F32),
        compiler_params=_cparams(1),
        name="ada_mod",
    )(c, w_bf, b.reshape(1, n))


def _inproj_kernel(x_ref, sh_ref, sc_ref, w_ref, b_ref, cos_ref, sin_ref,
                   wvt_ref, bvt_ref, u_ref, k_ref, v_ref, qb_ref, kb_ref, vb_ref, *, conv_dim, qk_dim,
                   transposed_v):
    bb, tl, d = x_ref.shape
    h = x_ref[...] * (1.0 + sc_ref[...]) + sh_ref[...]
    hb = h.reshape(bb * tl, d).astype(BF16)
    z = jnp.dot(hb, w_ref[...], preferred_element_type=F32) + b_ref[...]
    c0, c1, c2, c3 = conv_dim, 2 * conv_dim, 2 * conv_dim + qk_dim, 2 * conv_dim + 2 * qk_dim
    u = z[:, :c0] * jax.nn.sigmoid(z[:, c0:c1])
    u_ref[...] = u.reshape(bb, tl, conv_dim)

    lane = lax.broadcasted_iota(jnp.int32, (bb * tl, qk_dim), 1)
    first_half = (lane % ATT_HEAD_DIM) < (ATT_HEAD_DIM // 2)
    cos = jnp.broadcast_to(cos_ref[...][None], (bb, tl, qk_dim)).reshape(bb * tl, qk_dim)
    sin = jnp.broadcast_to(sin_ref[...][None], (bb, tl, qk_dim)).reshape(bb * tl, qk_dim)

    def rope(t):
        swapped = jnp.where(first_half,
                            pltpu.roll(t, qk_dim - ATT_HEAD_DIM // 2, axis=1),
                            pltpu.roll(t, ATT_HEAD_DIM // 2, axis=1))
        return t * cos + swapped * sin

    q = rope(z[:, c1:c2]) * (ATT_HEAD_DIM ** -0.5)
    k = rope(z[:, c2:c3])
    v = z[:, c3:]
    k_ref[...] = k.reshape(bb, tl, qk_dim)
    v_ref[...] = v.reshape(bb, tl, v.shape[1])
    qb_ref[...] = q.astype(BF16).reshape(bb, tl, qk_dim)
    kb_ref[...] = k.astype(BF16).reshape(bb, tl, qk_dim)
    if transposed_v:
        vt = lax.dot_general(wvt_ref[...], hb, _NT, preferred_element_type=F32) + bvt_ref[...]
        vb_ref[0] = vt.astype(BF16)
    else:
        vb_ref[...] = v.astype(BF16).reshape(bb, tl, v.shape[1])


def in_proj(x, sh1, sc1, w_bf, b_in, cos, sin, bb, tl, conv_dim, qk_dim, att_dim, transposed_v):
    B, L, D = x.shape
    n = w_bf.shape[1]
    assert not transposed_v or bb == 1
    tok = lambda b, l: (b, l, 0)
    mod = lambda b, l: (b, 0, 0)
    const = lambda b, l: (0, 0)
    out_shapes = [jax.ShapeDtypeStruct((B, L, conv_dim), F32),
                  jax.ShapeDtypeStruct((B, L, qk_dim), F32),
                  jax.ShapeDtypeStruct((B, L, att_dim), F32),
                  jax.ShapeDtypeStruct((B, L, qk_dim), BF16),
                  jax.ShapeDtypeStruct((B, L, qk_dim), BF16),
                  jax.ShapeDtypeStruct((B, L, att_dim), BF16)]
    out_specs = [pl.BlockSpec((bb, tl, s.shape[2]), tok) for s in out_shapes]
    if transposed_v:
        out_shapes[5] = jax.ShapeDtypeStruct((B, att_dim, L), BF16)
        out_specs[5] = pl.BlockSpec((1, att_dim, tl), lambda b, l: (b, 0, l))
    wvt = jnp.transpose(w_bf[:, n - att_dim:])
    bvt = b_in[n - att_dim:].reshape(att_dim, 1)
    return pl.pallas_call(
        functools.partial(_inproj_kernel, conv_dim=conv_dim, qk_dim=qk_dim, transposed_v=transposed_v),
        grid=(B // bb, L // tl),
        in_specs=[pl.BlockSpec((bb, tl, D), tok),
                  pl.BlockSpec((bb, 1, D), mod),
                  pl.BlockSpec((bb, 1, D), mod),
                  pl.BlockSpec((D, n), const),
                  pl.BlockSpec((1, n), const),
                  pl.BlockSpec((tl, qk_dim), lambda b, l: (l, 0)),
                  pl.BlockSpec((tl, qk_dim), lambda b, l: (l, 0)),
                  pl.BlockSpec((att_dim, D), const),
                  pl.BlockSpec((att_dim, 1), const)],
        out_specs=out_specs,
        out_shape=out_shapes,
        compiler_params=_cparams(2),
        name="in_proj",
    )(x, sh1, sc1, w_bf, b_in.reshape(1, n), cos, sin, wvt, bvt)


HIST_PAD = 32


def _conv_kernel(u_ref, prev_ref, hist_ref, w_ref, b_ref, g_ref, beta_ref, o_ref, win_ref, sh_ref, *, width):
    i = pl.program_id(1)
    tl = u_ref.shape[1]

    @pl.when(i == 0)
    def _():
        win_ref[0:HIST_PAD, :] = hist_ref[0]

    @pl.when(i != 0)
    def _():
        win_ref[0:HIST_PAD, :] = prev_ref[0]

    win_ref[HIST_PAD:HIST_PAD + tl, :] = u_ref[0]
    span = tl + HIST_PAD - SUBLANES
    for r in range(1, SUBLANES):
        sh_ref[r - 1] = win_ref[r:r + span, :]
    off = HIST_PAD - (width - 1)
    acc = None
    for j in range(width):
        r, q = (off + j) % SUBLANES, (off + j) // SUBLANES
        rows = pl.ds(q * SUBLANES, tl)
        tap = win_ref[rows, :] if r == 0 else sh_ref[r - 1, rows, :]
        term = w_ref[j:j + 1, :] * tap
        acc = term if acc is None else acc + term
    y = _ln(acc + b_ref[...], g_ref[...], beta_ref[...])
    o_ref[0] = (y * jax.nn.sigmoid(y)).astype(BF16)


def conv_ln(u, hist_pad, w_dw, b_dw, ln_g, ln_b, tl):
    B, L, C = u.shape
    width = w_dw.shape[0]
    steps = tl // HIST_PAD
    row = lambda b, i: (0, 0)
    return pl.pallas_call(
        functools.partial(_conv_kernel, width=width),
        grid=(B, L // tl),
        in_specs=[pl.BlockSpec((1, tl, C), lambda b, i: (b, i, 0)),
                  pl.BlockSpec((1, HIST_PAD, C), lambda b, i: (b, jnp.maximum(i * steps - 1, 0), 0)),
                  pl.BlockSpec((1, HIST_PAD, C), lambda b, i: (b, 0, 0)),
                  pl.BlockSpec((width, C), row),
                  pl.BlockSpec((1, C), row), pl.BlockSpec((1, C), row), pl.BlockSpec((1, C), row)],
        out_specs=pl.BlockSpec((1, tl, C), lambda b, i: (b, i, 0)),
        out_shape=jax.ShapeDtypeStruct((B, L, C), BF16),
        scratch_shapes=[pltpu.VMEM((HIST_PAD + tl, C), F32),
                        pltpu.VMEM((SUBLANES - 1, tl + HIST_PAD - SUBLANES, C), F32)],
        compiler_params=_cparams(2),
        name="conv_ln",
    )(u, u, hist_pad, w_dw, b_dw.reshape(1, C), ln_g.reshape(1, C), ln_b.reshape(1, C))


def _lambda_value(lam_ref, lam_init):
    s1 = jnp.sum(lam_ref[0:1, :] * lam_ref[1:2, :], axis=1, keepdims=True)
    s2 = jnp.sum(lam_ref[2:3, :] * lam_ref[3:4, :], axis=1, keepdims=True)
    return jnp.exp(s1) - jnp.exp(s2) + lam_init


def _split_maps(q):
    lane = lax.broadcasted_iota(jnp.int32, q.shape, 1)
    zero = jnp.zeros_like(q)
    return jnp.where(lane < ATT_HEAD_DIM, q, zero), jnp.where(lane >= ATT_HEAD_DIM, q, zero)


def _finish_heads(o0, o1, lam, g, lam_init):
    o = o0 - lam * o1
    o = o * lax.rsqrt(jnp.mean(o * o, axis=-1, keepdims=True) + LN_EPS) * g
    return o * (1.0 - lam_init)


def _attn_prompt_kernel(lam_ref, g_ref, q_ref, k_ref, vt_ref, o_ref, m_sc, l_sc, acc_sc, *, lam_init):
    i = pl.program_id(2)
    tq = q_ref.shape[1]
    q2 = jnp.concatenate(_split_maps(q_ref[0]), axis=0)
    m_sc[...] = jnp.full(m_sc.shape, -jnp.inf, F32)
    l_sc[...] = jnp.zeros(l_sc.shape, F32)
    acc_sc[...] = jnp.zeros(acc_sc.shape, F32)

    kpos = lax.broadcasted_iota(jnp.int32, (tq, 2 * tq), 0)
    qpos = lax.broadcasted_iota(jnp.int32, (tq, 2 * tq), 1) % tq
    allowed = (kpos // CHUNK) <= (qpos // CHUNK)

    def scores(j):
        kb = k_ref[0, pl.ds(pl.multiple_of(j * tq, tq), tq), :]
        return lax.dot_general(kb, q2, _NT, preferred_element_type=F32)

    def update(s, j):
        vt = vt_ref[0, :, pl.ds(pl.multiple_of(j * tq, tq), tq)]
        m_prev = m_sc[...]
        m_new = jnp.maximum(m_prev, jnp.max(s, axis=0, keepdims=True))
        alpha = jnp.exp(m_prev - m_new)
        pe = jnp.exp(s - m_new)
        l_sc[...] = alpha * l_sc[...] + jnp.sum(pe, axis=0, keepdims=True)
        acc_sc[...] = alpha * acc_sc[...] + jnp.dot(vt, pe.astype(BF16), preferred_element_type=F32)
        m_sc[...] = m_new

    def pair(jj, c):
        s_a, s_b = scores(2 * jj), scores(2 * jj + 1)
        update(s_a, 2 * jj)
        update(s_b, 2 * jj + 1)
        return c

    lax.fori_loop(0, i // 2, pair, 0)

    @pl.when(i % 2 == 1)
    def _():
        s_a, s_b = scores(i - 1), jnp.where(allowed, scores(i), -jnp.inf)
        update(s_a, i - 1)
        update(s_b, i)

    @pl.when(i % 2 == 0)
    def _():
        update(jnp.where(allowed, scores(i), -jnp.inf), i)

    lam = _lambda_value(lam_ref, lam_init)
    o = acc_sc[...] / l_sc[...]
    o = o[:, :tq] - lam * o[:, tq:]
    o = o * lax.rsqrt(jnp.mean(o * o, axis=0, keepdims=True) + LN_EPS) * g_ref[...]
    o_ref[0] = jnp.transpose(o * (1.0 - lam_init)).astype(BF16)


def attn_prompt(qb, kb, vt, lam_rows, subln_g, lam_init, tq):
    B, L, _ = qb.shape
    return pl.pallas_call(
        functools.partial(_attn_prompt_kernel, lam_init=lam_init),
        grid=(B, ATT_HEADS, L // tq),
        in_specs=[pl.BlockSpec((4, ATT_HEAD_DIM), lambda b, h, i: (0, 0)),
                  pl.BlockSpec((ATT_V_DIM, 1), lambda b, h, i: (0, 0)),
                  pl.BlockSpec((1, tq, LANES), lambda b, h, i: (b, i, h)),
                  pl.BlockSpec((1, L, LANES), lambda b, h, i: (b, 0, h)),
                  pl.BlockSpec((1, ATT_V_DIM, L), lambda b, h, i: (b, h, 0))],
        out_specs=pl.BlockSpec((1, tq, LANES), lambda b, h, i: (b, i, h)),
        out_shape=jax.ShapeDtypeStruct((B, L, ATT_HEADS * ATT_V_DIM), BF16),
        scratch_shapes=[pltpu.VMEM((1, 2 * tq), F32), pltpu.VMEM((1, 2 * tq), F32),
                        pltpu.VMEM((ATT_V_DIM, 2 * tq), F32)],
        compiler_params=_cparams(3),
        name="attn_prompt",
    )(lam_rows, subln_g.reshape(ATT_V_DIM, 1), qb, kb, vt)


def _attn_sample_kernel(lam_ref, g_ref, q_ref, ck_ref, cv_ref, kn_ref, vn_ref, o_ref, *, lam_init, past):
    ls = q_ref.shape[1]
    qs = _split_maps(q_ref[0])
    kp = ck_ref[0].astype(BF16)
    vp = cv_ref[0].astype(BF16)
    kn = kn_ref[0]
    vn = vn_ref[0]
    qpos = past + lax.broadcasted_iota(jnp.int32, (ls, ls), 0)
    kpos = past + lax.broadcasted_iota(jnp.int32, (ls, ls), 1)
    allowed = (kpos // CHUNK) <= (qpos // CHUNK)
    outs = []
    for p in range(2):
        sp = lax.dot_general(qs[p], kp, _NT, preferred_element_type=F32)
        sn = lax.dot_general(qs[p], kn, _NT, preferred_element_type=F32)
        sn = jnp.where(allowed, sn, -jnp.inf)
        m = jnp.maximum(jnp.max(sp, axis=1, keepdims=True), jnp.max(sn, axis=1, keepdims=True))
        pp = jnp.exp(sp - m)
        pn = jnp.exp(sn - m)
        denom = jnp.sum(pp, axis=1, keepdims=True) + jnp.sum(pn, axis=1, keepdims=True)
        acc = (jnp.dot(pp.astype(BF16), vp, preferred_element_type=F32)
               + jnp.dot(pn.astype(BF16), vn, preferred_element_type=F32))
        outs.append(acc / denom)
    lam = _lambda_value(lam_ref, lam_init)
    o_ref[0] = _finish_heads(outs[0], outs[1], lam, g_ref[...], lam_init).astype(BF16)


def attn_sample(qb, cache_k, cache_v, kb, vb, lam_rows, subln_g, lam_init):
    B, Ls, _ = qb.shape
    P = cache_k.shape[1]
    blk = lambda n: pl.BlockSpec((1, n, LANES), lambda b, h: (b, 0, h))
    return pl.pallas_call(
        functools.partial(_attn_sample_kernel, lam_init=lam_init, past=P),
        grid=(B, ATT_HEADS),
        in_specs=[pl.BlockSpec((4, ATT_HEAD_DIM), lambda b, h: (0, 0)),
                  pl.BlockSpec((1, ATT_V_DIM), lambda b, h: (0, 0)),
                  blk(Ls), blk(P), blk(P), blk(Ls), blk(Ls)],
        out_specs=blk(Ls),
        out_shape=jax.ShapeDtypeStruct((B, Ls, ATT_HEADS * ATT_V_DIM), BF16),
        compiler_params=_cparams(2),
        name="attn_sample",
    )(lam_rows, subln_g.reshape(1, ATT_V_DIM), qb, cache_k, cache_v, kb, vb)


def _outproj_kernel(conv_ref, att_ref, x_ref, g1_ref, sh2_ref, sc2_ref, wc_ref, wa_ref, b_ref,
                    lg_ref, lb_ref, x1_ref, hh_ref, h2b_ref, *, alpha):
    bb, tl, d = x_ref.shape
    conv = conv_ref[...].reshape(bb * tl, conv_ref.shape[2])
    att = att_ref[...].reshape(bb * tl, att_ref.shape[2])
    mm = (jnp.dot(conv, wc_ref[...], preferred_element_type=F32)
          + jnp.dot(att, wa_ref[...], preferred_element_type=F32) + b_ref[...])
    y = alpha * x_ref[...] + g1_ref[...] * mm.reshape(bb, tl, d)
    x1 = _ln(y, lg_ref[...], lb_ref[...])
    x1_ref[...] = x1
    h2 = x1 * (1.0 + sc2_ref[...]) + sh2_ref[...]
    h2b_ref[...] = h2.astype(BF16)
    hi, lo = _split_bf16(h2.reshape(bb * tl, d))
    bits = [lax.bitcast_convert_type(part.astype(F32), jnp.uint32) for part in (hi, lo)]
    rows = [b[:, c * LANES:(c + 1) * LANES] for b in bits for c in range(d // LANES)]
    for i in range(len(rows) // 2):
        hh_ref[:, i, :] = (rows[2 * i] >> 16) | rows[2 * i + 1]


def out_proj(conv_out, att, x, g1, sh2, sc2, wc_bf, wa_bf, b_out, ln_g, ln_b, alpha, bb, tl):
    B, L, D = x.shape
    tok = lambda b, l: (b, l, 0)
    mod = lambda b, l: (b, 0, 0)
    const = lambda b, l: (0, 0)
    return pl.pallas_call(
        functools.partial(_outproj_kernel, alpha=alpha),
        grid=(B // bb, L // tl),
        in_specs=[pl.BlockSpec((bb, tl, conv_out.shape[2]), tok),
                  pl.BlockSpec((bb, tl, att.shape[2]), tok),
                  pl.BlockSpec((bb, tl, D), tok),
                  pl.BlockSpec((bb, 1, D), mod), pl.BlockSpec((bb, 1, D), mod), pl.BlockSpec((bb, 1, D), mod),
                  pl.BlockSpec(wc_bf.shape, const), pl.BlockSpec(wa_bf.shape, const),
                  pl.BlockSpec((1, D), const), pl.BlockSpec((1, D), const), pl.BlockSpec((1, D), const)],
        out_specs=[pl.BlockSpec((bb, tl, D), tok),
                   pl.BlockSpec((bb * tl, D // LANES, LANES), lambda b, l: (b * (L // tl) + l, 0, 0)),
                   pl.BlockSpec((bb, tl, D), tok)],
        out_shape=[jax.ShapeDtypeStruct((B, L, D), F32),
                   jax.ShapeDtypeStruct((B * L, D // LANES, LANES), jnp.uint32),
                   jax.ShapeDtypeStruct((B, L, D), BF16)],
        compiler_params=_cparams(2),
        name="out_proj",
    )(conv_out, att, x, g1, sh2, sc2, wc_bf, wa_bf, b_out.reshape(1, D), ln_g.reshape(1, D), ln_b.reshape(1, D))


TOPK_STREAMS = 2


def _extract_top(vals, order, count, out, payload=None):
    big = jnp.float32(1e9)
    out_v, out_p = [], []
    for _ in range(count):
        m = jnp.max(vals, axis=0, keepdims=True)
        pos = jnp.min(jnp.where(vals == m, order, big), axis=0, keepdims=True)
        sel = order == pos
        out_v.append(m)
        out_p.append(pos if payload is None else jnp.max(jnp.where(sel, payload, -1.0), axis=0, keepdims=True))
        vals = jnp.where(sel, -jnp.inf, vals)
        yield
    out.append((jnp.concatenate(out_v, axis=0), jnp.concatenate(out_p, axis=0)))


def _interleave(streams):
    done = [0] * len(streams)
    total = max(n for _, n in streams)
    for tick in range(1, total + 1):
        for s, (gen, n) in enumerate(streams):
            while done[s] * total < tick * n:
                next(gen, None)
                done[s] += 1
    for gen, _ in streams:
        for _ in gen:
            pass


class _TopkWork:
    def __init__(self, h_ref, wq_ref, keys_ref, q_sc, e_sc, g_sc, max_streams=TOPK_STREAMS, alternate_halves=True):
        self.keys_ref, self.q_sc, self.e_sc, self.g_sc = keys_ref, q_sc, e_sc, g_sc
        self.alternate_halves = alternate_halves
        self.n_hp, self.n_keys, _ = keys_ref.shape
        tl = h_ref.shape[0]
        K = PEER_TOPK
        qp = jnp.dot(h_ref[...], wq_ref[...], preferred_element_type=F32)
        for hp in range(self.n_hp):
            q_sc[hp] = qp[:, hp * LANES:(hp + 1) * LANES].astype(BF16)
        self.key_iota = lax.broadcasted_iota(jnp.int32, (self.n_keys, LANES), 0).astype(F32)
        n_cand = K + (K - 1) * SUBLANES
        row = lax.broadcasted_iota(jnp.int32, (n_cand, LANES), 0)
        ci = jnp.where(row < K, 0, ((row - K) >> 3) + 1)
        cj = jnp.where(row < K, row, (row - K) & 7)
        limit = jnp.where(ci == 0, 16, jnp.where(ci == 1, 8, jnp.where(ci == 2, 5, jnp.where(
            ci == 3, 4, jnp.where(ci == 4, 3, jnp.where(ci <= 7, 2, 1))))))
        self.cand_ok = cj < limit
        self.cand_order = (ci * K + cj).astype(F32)
        n_chunks = tl // LANES
        self.streams = max_streams if n_chunks % max_streams == 0 else 1
        self.groups = n_chunks // self.streams
        self.n_steps = (self.n_hp // 2) * self.groups

    CHUNK_STEPS = 3 * PEER_TOPK - 1

    def head_chunk(self, h, col):
        K = PEER_TOPK
        halves = ([], [])

        def take_half(p):
            qc = self.q_sc[2 * h + p, pl.ds(col, LANES), :]
            s = lax.dot_general(self.keys_ref[2 * h + p], qc, _NT, preferred_element_type=F32)
            return _extract_top(s, self.key_iota, K, halves[p])

        if self.alternate_halves:
            takes = [take_half(0), take_half(1)]
            for _ in range(K):
                for take in takes:
                    next(take)
                yield
            for take in takes:
                next(take, None)
        else:
            for p in range(2):
                take = take_half(p)
                for r in range(K):
                    next(take)
                    if (p * K + r) % 2:
                        yield
                next(take, None)
        (sv0, si0), (sv1, si1) = halves[0][0], halves[1][0]
        cv = [sv0[0:1] + sv1]
        ce = [si0[0:1] * self.n_keys + si1]
        for r in range(1, K):
            cv.append(sv0[r:r + 1] + sv1[0:SUBLANES])
            ce.append(si0[r:r + 1] * self.n_keys + si1[0:SUBLANES])
            yield
        cand = jnp.where(self.cand_ok, jnp.concatenate(cv, axis=0), -jnp.inf)
        final = []
        yield from _extract_top(cand, self.cand_order, K, final, payload=jnp.concatenate(ce, axis=0))
        fv, fe = final[0]
        ex = jnp.exp(fv - fv[0:1])
        gate = ex / jnp.sum(ex, axis=0, keepdims=True)
        off = pl.multiple_of(h * K, K)
        self.e_sc[pl.ds(off, K), pl.ds(col, LANES)] = fe
        self.g_sc[pl.ds(off, K), pl.ds(col, LANES)] = gate

    def step_streams(self, i):
        h = i // self.groups
        return [(self.head_chunk(h, pl.multiple_of(((i % self.groups) * self.streams + c) * LANES, LANES)),
                 self.CHUNK_STEPS) for c in range(self.streams)]

    def finish(self, e_ref, g_ref):
        e_ref[...] = jnp.transpose(self.e_sc[...]).astype(jnp.int32) * TABLE_ROWS_PER_EXPERT
        g_ref[...] = jnp.transpose(self.g_sc[...])


def _topk_kernel(h_ref, wq_ref, keys_ref, e_ref, g_ref, q_sc, e_sc, g_sc):
    work = _TopkWork(h_ref, wq_ref, keys_ref, q_sc, e_sc, g_sc)

    def step(i, carry):
        _interleave(work.step_streams(i))
        return carry

    lax.fori_loop(0, work.n_steps, step, 0)
    work.finish(e_ref, g_ref)


def _topk_specs(D, wq_bf, keys_bf, tl, blk0):
    n_hp, n_keys, half = keys_bf.shape
    slots = (n_hp // 2) * PEER_TOPK
    assert half == LANES and n_keys == LANES and PEER_TOPK * PEER_TOPK <= 256
    in_specs = [pl.BlockSpec((tl, D), lambda i: (i + blk0, 0)),
                pl.BlockSpec(wq_bf.shape, lambda i: (0, 0), pipeline_mode=pl.Buffered(1)),
                pl.BlockSpec(keys_bf.shape, lambda i: (0, 0, 0), pipeline_mode=pl.Buffered(1))]
    out_specs = [pl.BlockSpec((tl, slots), lambda i: (i, 0))] * 2
    scratch = [pltpu.VMEM((n_hp, tl, LANES), BF16), pltpu.VMEM((slots, tl), F32), pltpu.VMEM((slots, tl), F32)]
    return slots, in_specs, out_specs, scratch


def peer_topk(h2b, wq_bf, keys_bf, tl, blk0=0, nblk=None):
    T, D = h2b.shape
    nblk = T // tl if nblk is None else nblk
    slots, in_specs, out_specs, scratch = _topk_specs(D, wq_bf, keys_bf, tl, blk0)
    return pl.pallas_call(
        _topk_kernel,
        grid=(nblk,),
        in_specs=in_specs,
        out_specs=out_specs,
        out_shape=[jax.ShapeDtypeStruct((nblk * tl, slots), jnp.int32),
                   jax.ShapeDtypeStruct((nblk * tl, slots), F32)],
        scratch_shapes=scratch,
        compiler_params=_cparams(1),
        name="peer_topk",
    )(h2b, wq_bf, keys_bf)


TABLE_ROWS_PER_EXPERT = 4
PEER_TOKEN_BLOCK = 128
PEER_TOKEN_UNROLL = 16
GATHER_ID_WINDOW = 16


PACK_EXPERT_BLOCK = 512


def _pack_kernel(t_ref, o_ref):
    n = t_ref.shape[0]
    bits = lax.bitcast_convert_type(t_ref[...].astype(BF16).astype(F32), jnp.uint32)
    for i in range(TABLE_ROWS_PER_EXPERT):
        lo = bits[:, (2 * i) * LANES:(2 * i + 1) * LANES]
        hi = bits[:, (2 * i + 1) * LANES:(2 * i + 2) * LANES]
        o_ref[pl.ds(i, n, stride=TABLE_ROWS_PER_EXPERT), :] = (lo >> 16) | hi


def pack_table(tab):
    n, d = tab.shape
    assert d == 2 * LANES * TABLE_ROWS_PER_EXPERT
    tn = min(n, PACK_EXPERT_BLOCK)
    return pl.pallas_call(
        _pack_kernel,
        grid=(n // tn,),
        in_specs=[pl.BlockSpec((tn, d), lambda i: (i, 0))],
        out_specs=pl.BlockSpec((tn * TABLE_ROWS_PER_EXPERT, LANES), lambda i: (i, 0)),
        out_shape=jax.ShapeDtypeStruct((n * TABLE_ROWS_PER_EXPERT, LANES), jnp.uint32),
        compiler_params=_cparams(1),
        name="pack_table",
    )(tab)


def _gather_rows(e_ref, tab_ref, base, slots):
    rows = []
    for j0 in range(0, slots, GATHER_ID_WINDOW):
        ids = e_ref.at[pl.ds(pl.multiple_of(base + j0, GATHER_ID_WINDOW), GATHER_ID_WINDOW)]
        for j in range(GATHER_ID_WINDOW):
            start = pl.multiple_of(ids[j], TABLE_ROWS_PER_EXPERT)
            rows.append(pltpu.bitcast(tab_ref[pl.ds(start, TABLE_ROWS_PER_EXPERT), :], BF16))
    return jnp.concatenate(rows, axis=0)


def _split_bf16(x):
    hi = x.astype(BF16)
    return hi, (x - hi.astype(F32)).astype(BF16)


def _chunk_mask(slots):
    lane = lax.broadcasted_iota(jnp.int32, (SUBLANES, slots * SUBLANES), 1)
    sub = lax.broadcasted_iota(jnp.int32, (SUBLANES, slots * SUBLANES), 0)
    return (lane & (SUBLANES - 1)) == sub


def _slot_spread(slots):
    r = lax.broadcasted_iota(jnp.int32, (slots, slots * SUBLANES), 0)
    c = lax.broadcasted_iota(jnp.int32, (slots, slots * SUBLANES), 1)
    return jnp.where((c >> 3) == r, 1.0, 0.0).astype(BF16)


def _peer_a_tokens(count, t0, z0, e_ref, hh_ref, tab_ref, z_sc, mask, slots):
    for u in range(count):
        t = t0 + u
        g = _gather_rows(e_ref, tab_ref, pl.multiple_of(t * slots, slots), slots)
        hh = pltpu.bitcast(hh_ref[t], BF16)
        z = lax.dot_general(hh, g, _NT, preferred_element_type=F32)
        z8 = z[0:SUBLANES] + z[SUBLANES:2 * SUBLANES]
        z_sc[pl.ds(z0 + u, 1), :] = jnp.sum(jnp.where(mask, z8, 0.0), axis=0, keepdims=True)
        yield


def _peer_a_weights(z_sc, gate, slots):
    zhi, zlo = _split_bf16(z_sc[...])
    spread = _slot_spread(slots)
    a = (lax.dot_general(zhi, spread, _NT, preferred_element_type=F32)
         + lax.dot_general(zlo, spread, _NT, preferred_element_type=F32))
    return 0.5 * a * (1.0 + lax.erf(a * (2.0 ** -0.5))) * gate


def _peer_a_kernel(e_ref, hh_ref, g_ref, tab_ref, w_ref, z_sc, *, slots):
    tb = hh_ref.shape[0]
    mask = _chunk_mask(slots)

    def tokens(i, carry):
        t0 = i * PEER_TOKEN_UNROLL
        for _ in _peer_a_tokens(PEER_TOKEN_UNROLL, t0, t0, e_ref, hh_ref, tab_ref, z_sc, mask, slots):
            pass
        return carry

    lax.fori_loop(0, tb // PEER_TOKEN_UNROLL, tokens, 0)
    w_ref[...] = _peer_a_weights(z_sc, g_ref[...], slots)


def _peer_a_specs(tab_u32, slots, tb, blk0):
    in_specs = [pl.BlockSpec((tb * slots,), lambda i: (i,), memory_space=pltpu.SMEM),
                pl.BlockSpec((tb, SUBLANES, LANES), lambda i: (i + blk0, 0, 0)),
                pl.BlockSpec((tb, slots), lambda i: (i, 0)),
                pl.BlockSpec(tab_u32.shape, lambda i: (0, 0), pipeline_mode=pl.Buffered(1))]
    scratch = [pltpu.VMEM((PEER_TOKEN_BLOCK, slots * SUBLANES), F32)]
    return in_specs, pl.BlockSpec((tb, slots), lambda i: (i, 0)), scratch


def peer_a(e_flat, hh, gate, tab_u32, slots, blk0=0):
    T = gate.shape[0]
    tb = PEER_TOKEN_BLOCK
    in_specs, out_spec, scratch = _peer_a_specs(tab_u32, slots, tb, blk0)
    return pl.pallas_call(
        functools.partial(_peer_a_kernel, slots=slots),
        grid=(T // tb,),
        in_specs=in_specs,
        out_specs=out_spec,
        out_shape=jax.ShapeDtypeStruct((T, slots), F32),
        scratch_shapes=scratch,
        compiler_params=_cparams(1, VMEM_LIMIT_TABLE),
        name="peer_a",
    )(e_flat, hh, gate, tab_u32)


def _topk_a_kernel(e_ref, hh_ref, gate_ref, tab_ref, h_ref, wq_ref, keys_ref, w_ref, e_out_ref, g_out_ref,
                   z_sc, q_sc, e_sc, g_sc, *, slots):
    tl = hh_ref.shape[0]
    work = _TopkWork(h_ref, wq_ref, keys_ref, q_sc, e_sc, g_sc, max_streams=1, alternate_halves=False)
    mask = _chunk_mask(slots)
    per_step = tl // work.n_steps
    flush = PEER_TOKEN_BLOCK // per_step
    assert per_step % PEER_TOKEN_UNROLL == 0 and flush * per_step == PEER_TOKEN_BLOCK

    def step(i, carry):
        gathers = _peer_a_tokens(per_step, i * per_step, (i % flush) * per_step, e_ref, hh_ref, tab_ref, z_sc,
                                 mask, slots)
        _interleave(work.step_streams(i) + [(gathers, per_step)])

        @pl.when(i % flush == flush - 1)
        def _():
            rows = pl.ds(pl.multiple_of((i // flush) * PEER_TOKEN_BLOCK, PEER_TOKEN_BLOCK), PEER_TOKEN_BLOCK)
            w_ref[rows, :] = _peer_a_weights(z_sc, gate_ref[rows, :], slots)

        return carry

    lax.fori_loop(0, work.n_steps, step, 0)
    work.finish(e_out_ref, g_out_ref)


def peer_topk_a(e_flat, hh, gate, tab_u32, h2b, wq_bf, keys_bf, tl, a_blk0, k_blk0):
    T, D = gate.shape[0], h2b.shape[1]
    slots, k_in, k_out, k_scratch = _topk_specs(D, wq_bf, keys_bf, tl, k_blk0)
    a_in, a_out, a_scratch = _peer_a_specs(tab_u32, slots, tl, a_blk0)
    return pl.pallas_call(
        functools.partial(_topk_a_kernel, slots=slots),
        grid=(T // tl,),
        in_specs=a_in + k_in,
        out_specs=[a_out] + k_out,
        out_shape=[jax.ShapeDtypeStruct((T, slots), F32), jax.ShapeDtypeStruct((T, slots), jnp.int32),
                   jax.ShapeDtypeStruct((T, slots), F32)],
        scratch_shapes=a_scratch + k_scratch,
        compiler_params=_cparams(1, VMEM_LIMIT_FUSED),
        name="peer_topk_a",
    )(e_flat, hh, gate, tab_u32, h2b, wq_bf, keys_bf)


def _peer_b_kernel(e_ref, w_ref, tab_ref, o_ref, rep_sc, *, slots):
    tb = o_ref.shape[0]
    spread = _slot_spread(slots)
    whi, wlo = _split_bf16(w_ref[...])
    rep_sc[0] = jnp.dot(whi, spread, preferred_element_type=F32)
    rep_sc[1] = jnp.dot(wlo, spread, preferred_element_type=F32)
    mask = _chunk_mask(slots)
    wide = (SUBLANES, slots * SUBLANES)

    def tokens(i, carry):
        for u in range(PEER_TOKEN_UNROLL):
            t = i * PEER_TOKEN_UNROLL + u
            g = _gather_rows(e_ref, tab_ref, pl.multiple_of(t * slots, slots), slots)
            parts = [jnp.where(mask, jnp.broadcast_to(rep_sc[p, pl.ds(t, 1), :], wide), 0.0).astype(BF16)
                     for p in range(2)]
            r = jnp.dot(jnp.concatenate(parts, axis=0), g, preferred_element_type=F32)
            o_ref[t] = r[0:SUBLANES] + r[SUBLANES:2 * SUBLANES]
        return carry

    lax.fori_loop(0, tb // PEER_TOKEN_UNROLL, tokens, 0)


def peer_b(e_flat, w, tab_u32, slots):
    T = w.shape[0]
    tb = PEER_TOKEN_BLOCK
    return pl.pallas_call(
        functools.partial(_peer_b_kernel, slots=slots),
        grid=(T // tb,),
        in_specs=[pl.BlockSpec((tb * slots,), lambda i: (i,), memory_space=pltpu.SMEM),
                  pl.BlockSpec((tb, slots), lambda i: (i, 0)),
                  pl.BlockSpec(tab_u32.shape, lambda i: (0, 0), pipeline_mode=pl.Buffered(1))],
        out_specs=pl.BlockSpec((tb, SUBLANES, LANES), lambda i: (i, 0, 0)),
        out_shape=jax.ShapeDtypeStruct((T, SUBLANES, LANES), F32),
        scratch_shapes=[pltpu.VMEM((2, tb, slots * SUBLANES), F32)],
        compiler_params=_cparams(1, VMEM_LIMIT_TABLE),
        name="peer_b",
    )(e_flat, w, tab_u32)


def _final_kernel(x_ref, ff_ref, g2_ref, lg_ref, lb_ref, o_ref, *, alpha):
    bb, tl, d = x_ref.shape
    ff = jnp.concatenate([ff_ref[:, c, :] for c in range(d // LANES)], axis=1)
    y = alpha * x_ref[...] + g2_ref[...] * ff.reshape(bb, tl, d)
    o_ref[...] = _ln(y, lg_ref[...], lb_ref[...])


def final_ln(x1, ff, g2, ln_g, ln_b, alpha, bb, tl):
    B, L, D = x1.shape
    tok = lambda b, l: (b, l, 0)
    const = lambda b, l: (0, 0)
    return pl.pallas_call(
        functools.partial(_final_kernel, alpha=alpha),
        grid=(B // bb, L // tl),
        in_specs=[pl.BlockSpec((bb, tl, D), tok),
                  pl.BlockSpec((bb * tl, D // LANES, LANES), lambda b, l: (b * (L // tl) + l, 0, 0)),
                  pl.BlockSpec((bb, 1, D), lambda b, l: (b, 0, 0)),
                  pl.BlockSpec((1, D), const), pl.BlockSpec((1, D), const)],
        out_specs=pl.BlockSpec((bb, tl, D), tok),
        out_shape=jax.ShapeDtypeStruct((B, L, D), F32),
        compiler_params=_cparams(2),
        name="final_ln",
    )(x1, ff, g2, ln_g.reshape(1, D), ln_b.reshape(1, D))


def _rope_tables(past, length, n_groups):
    half = ATT_HEAD_DIM // 2
    inv = 1.0 / (ROPE_THETA ** (jnp.arange(half, dtype=F32) / half))
    ang = (past + jnp.arange(length)).astype(F32)[:, None] * inv[None, :]
    cos = jnp.cos(ang)
    sin = jnp.sin(ang)
    cos_t = jnp.tile(jnp.concatenate([cos, cos], axis=-1), (1, n_groups))
    sin_t = jnp.tile(jnp.concatenate([-sin, sin], axis=-1), (1, n_groups))
    return cos_t, sin_t


def _token_tiles(B, L, target):
    if L >= target:
        return 1, target
    bb = max(1, min(B, target // L))
    while B % bb:
        bb -= 1
    return bb, L


TOPK_TOKEN_BLOCK = 512
PEER_MAX_SPLIT = 8


def _peer_select_and_weigh(h2b, hh, wts, slots):
    T = h2b.shape[0]
    tl = min(T, TOPK_TOKEN_BLOCK)
    wq, keys, tab = wts["w_query"], wts["sub_keys"], wts["u_tab"]
    split = max(s for s in range(1, PEER_MAX_SPLIT + 1) if T % (s * tl) == 0)
    if split == 1 or tl % PEER_TOKEN_BLOCK:
        e, gate = peer_topk(h2b, wq, keys, tl)
        return e, peer_a(e.reshape(T * slots), hh, gate, tab, slots)
    nblk = T // (split * tl)
    chunk = nblk * tl
    e_c, gate_c = peer_topk(h2b, wq, keys, tl, 0, nblk)
    es, ws = [e_c], []
    for c in range(1, split):
        w_c, e_n, gate_n = peer_topk_a(e_c.reshape(chunk * slots), hh, gate_c, tab, h2b, wq, keys, tl,
                                       (c - 1) * nblk, c * nblk)
        ws.append(w_c)
        es.append(e_n)
        e_c, gate_c = e_n, gate_n
    ws.append(peer_a(e_c.reshape(chunk * slots), hh, gate_c, tab, slots, (split - 1) * chunk // PEER_TOKEN_BLOCK))
    return jnp.concatenate(es, axis=0), jnp.concatenate(ws, axis=0)


def _layer(x, mod, conv_hist, k_past, v_past, wts, lam_init, alpha):
    B, L, D = x.shape
    conv_dim = wts["w_dw"].shape[1]
    qk_dim = ATT_HEADS * 2 * ATT_HEAD_DIM
    att_dim = ATT_HEADS * ATT_V_DIM
    width = wts["w_dw"].shape[0]
    assert L >= width - 1 and width - 1 <= HIST_PAD
    sh1, sc1, g1, sh2, sc2, g2 = [m[:, None, :] for m in jnp.split(mod, 6, axis=-1)]
    past = 0 if k_past is None else k_past.shape[1]
    cos, sin = _rope_tables(past, L, qk_dim // ATT_HEAD_DIM)
    bb, tl = _token_tiles(B, L, 512)

    u, k, v, qb, kb, vb = in_proj(x, sh1, sc1, wts["w_in"], wts["b_in"], cos, sin, bb, tl,
                                  conv_dim, qk_dim, att_dim, transposed_v=k_past is None)
    hist_pad = jnp.pad(conv_hist, ((0, 0), (HIST_PAD - (width - 1), 0), (0, 0)))
    conv_out = conv_ln(u, hist_pad, wts["w_dw"], wts["b_dw"], wts["conv_ln_g"], wts["conv_ln_b"], min(L, 256))
    conv_state = u[:, L - (width - 1):, :]

    if k_past is None:
        att = attn_prompt(qb, kb, vb, wts["lam_rows"], wts["subln_g"], lam_init, min(L, 256))
    else:
        att = attn_sample(qb, k_past.reshape(B, past, qk_dim), v_past.reshape(B, past, att_dim), kb, vb,
                          wts["lam_rows"], wts["subln_g"], lam_init)

    x1, hh, h2b = out_proj(conv_out, att, x, g1, sh2, sc2, wts["w_out_conv"], wts["w_out_att"], wts["b_out"],
                           wts["ln1_g"], wts["ln1_b"], alpha, bb, tl)

    T = B * L
    slots = (wts["sub_keys"].shape[0] // 2) * PEER_TOPK
    e, w = _peer_select_and_weigh(h2b.reshape(T, D), hh, wts, slots)
    ff = peer_b(e.reshape(T * slots), w, wts["v_tab"], slots)
    out = final_ln(x1, ff, g2, wts["ln2_g"], wts["ln2_b"], alpha, bb, tl)
    return out, conv_state, k, v


def kernel(x_prompt, x_sample, cache_k, cache_v, cache_conv, c_prompt, c_sample, w_ada, b_ada, w_in, b_in, w_dw, b_dw, conv_ln_g, conv_ln_b, lam_q1, lam_k1, lam_q2, lam_k2, subln_g, w_out, b_out, ln1_g, ln1_b, w_query, sub_keys, u_tab, v_tab, ln2_g, ln2_b):
    depth = w_ada.shape[0]
    D = x_prompt.shape[-1]
    assert D == SUBLANES * LANES
    alpha = (2 * depth) ** 0.25
    Bp, Bs = c_prompt.shape[0], c_sample.shape[0]
    xp, xs = x_prompt, x_sample
    outs = [[] for _ in range(6)]
    for l in range(depth):
        lam_init = 0.8 - 0.6 * math.exp(-0.3 * l)
        conv_dim = w_dw.shape[2]
        n_exp = u_tab.shape[1]
        wts = {
            "w_in": w_in[l].astype(BF16), "b_in": b_in[l], "w_dw": w_dw[l], "b_dw": b_dw[l],
            "conv_ln_g": conv_ln_g[l], "conv_ln_b": conv_ln_b[l],
            "lam_rows": jnp.stack([lam_q1[l], lam_k1[l], lam_q2[l], lam_k2[l]]).astype(F32),
            "subln_g": subln_g[l],
            "w_out_conv": w_out[l, :conv_dim].astype(BF16), "w_out_att": w_out[l, conv_dim:].astype(BF16),
            "b_out": b_out[l], "ln1_g": ln1_g[l], "ln1_b": ln1_b[l],
            "w_query": w_query[l].astype(BF16),
            "sub_keys": sub_keys[l].astype(BF16).reshape(-1, sub_keys.shape[3], sub_keys.shape[4]),
            "u_tab": pack_table(u_tab[l]), "v_tab": pack_table(v_tab[l]),
            "ln2_g": ln2_g[l], "ln2_b": ln2_b[l],
        }
        c_all = jnp.concatenate([c_prompt, c_sample], axis=0)
        pad = (-c_all.shape[0]) % 16
        c_all = jnp.pad(c_all, ((0, pad), (0, 0)))
        mod = ada_mod(c_all, w_ada[l].astype(BF16), b_ada[l])
        hist0 = jnp.zeros((Bp, w_dw.shape[1] - 1, conv_dim), xp.dtype)
        xp, cp, kp, vp = _layer(xp, mod[:Bp], hist0, None, None, wts, lam_init, alpha)
        xs, cs, kn, vn = _layer(xs, mod[Bp:Bp + Bs], cache_conv[l], cache_k[l], cache_v[l], wts, lam_init, alpha)
        for lst, val in zip(outs, (kp, vp, cp, kn, vn, cs)):
            lst.append(val)
    Lp, Ls = x_prompt.shape[1], x_sample.shape[1]
    k_prompt = jnp.stack(outs[0]).reshape(depth, Bp, Lp, ATT_HEADS, 2, ATT_HEAD_DIM)
    v_prompt = jnp.stack(outs[1]).reshape(depth, Bp, Lp, ATT_HEADS, ATT_V_DIM)
    conv_prompt = jnp.stack(outs[2])
    k_sample = jnp.stack(outs[3]).reshape(depth, Bs, Ls, ATT_HEADS, 2, ATT_HEAD_DIM)
    v_sample = jnp.stack(outs[4]).reshape(depth, Bs, Ls, ATT_HEADS, ATT_V_DIM)
    conv_sample = jnp.stack(outs[5])
    return (xp, xs, k_prompt, v_prompt, conv_prompt, k_sample, v_sample, conv_sample)
```

```python
import functools
import math

import jax
import jax.numpy as jnp
import numpy as np
from jax import lax
from jax.experimental import pallas as pl
from jax.experimental.pallas import tpu as pltpu

F32 = jnp.float32
BF16 = jnp.bfloat16

CHUNK = 64
ATT_HEADS = 4
ATT_HEAD_DIM = 64
ATT_V_DIM = 2 * ATT_HEAD_DIM
ROPE_THETA = 10000.0
PEER_TOPK = 16
LN_EPS = 1e-5

LANES = 128
SUBLANES = 8
VMEM_LIMIT_DEFAULT = 48 * 1024 * 1024
VMEM_LIMIT_TABLE = 56 * 1024 * 1024
VMEM_LIMIT_FUSED = 60 * 1024 * 1024

_NT = (((1,), (1,)), ((), ()))


def _cparams(n_axes, vmem=VMEM_LIMIT_DEFAULT):
    return pltpu.CompilerParams(dimension_semantics=("arbitrary",) * n_axes, vmem_limit_bytes=vmem)


def _ln(y, g, b):
    mu = jnp.mean(y, axis=-1, keepdims=True)
    d = y - mu
    var = jnp.mean(d * d, axis=-1, keepdims=True)
    return d * lax.rsqrt(var + LN_EPS) * g + b


def _ada_kernel(c_ref, w_ref, b_ref, o_ref):
    c = c_ref[...]
    s = (c * jax.nn.sigmoid(c)).astype(BF16)
    o_ref[...] = jnp.dot(s, w_ref[...], preferred_element_type=F32) + b_ref[...]


def ada_mod(c, w_bf, b):
    rows, d = c.shape
    n = w_bf.shape[1]
    tn = min(n, 1024)
    return pl.pallas_call(
        _ada_kernel,
        grid=(n // tn,),
        in_specs=[pl.BlockSpec((rows, d), lambda j: (0, 0)),
                  pl.BlockSpec((d, tn), lambda j: (0, j)),
                  pl.BlockSpec((1, tn), lambda j: (0, j))],
        out_specs=pl.BlockSpec((rows, tn), lambda j: (0, j)),
        out_shape=jax.ShapeDtypeStruct((rows, n), F32),
        compiler_params=_cparams(1),
        name="ada_mod",
    )(c, w_bf, b.reshape(1, n))


def _inproj_kernel(x_ref, sh_ref, sc_ref, w_ref, b_ref, cos_ref, sin_ref,
                   wvt_ref, bvt_ref, u_ref, k_ref, v_ref, qb_ref, kb_ref, vb_ref, *, conv_dim, qk_dim,
                   transposed_v):
    bb, tl, d = x_ref.shape
    h = x_ref[...] * (1.0 + sc_ref[...]) + sh_ref[...]
    hb = h.reshape(bb * tl, d).astype(BF16)
    z = jnp.dot(hb, w_ref[...], preferred_element_type=F32) + b_ref[...]
    c0, c1, c2, c3 = conv_dim, 2 * conv_dim, 2 * conv_dim + qk_dim, 2 * conv_dim + 2 * qk_dim
    u = z[:, :c0] * jax.nn.sigmoid(z[:, c0:c1])
    u_ref[...] = u.reshape(bb, tl, conv_dim)

    lane = lax.broadcasted_iota(jnp.int32, (bb * tl, qk_dim), 1)
    first_half = (lane % ATT_HEAD_DIM) < (ATT_HEAD_DIM // 2)
    cos = jnp.broadcast_to(cos_ref[...][None], (bb, tl, qk_dim)).reshape(bb * tl, qk_dim)
    sin = jnp.broadcast_to(sin_ref[...][None], (bb, tl, qk_dim)).reshape(bb * tl, qk_dim)

    def rope(t):
        swapped = jnp.where(first_half,
                            pltpu.roll(t, qk_dim - ATT_HEAD_DIM // 2, axis=1),
                            pltpu.roll(t, ATT_HEAD_DIM // 2, axis=1))
        return t * cos + swapped * sin

    q = rope(z[:, c1:c2]) * (ATT_HEAD_DIM ** -0.5)
    k = rope(z[:, c2:c3])
    v = z[:, c3:]
    k_ref[...] = k.reshape(bb, tl, qk_dim)
    v_ref[...] = v.reshape(bb, tl, v.shape[1])
    qb_ref[...] = q.astype(BF16).reshape(bb, tl, qk_dim)
    kb_ref[...] = k.astype(BF16).reshape(bb, tl, qk_dim)
    if transposed_v:
        vt = lax.dot_general(wvt_ref[...], hb, _NT, preferred_element_type=F32) + bvt_ref[...]
        vb_ref[0] = vt.astype(BF16)
    else:
        vb_ref[...] = v.astype(BF16).reshape(bb, tl, v.shape[1])


def in_proj(x, sh1, sc1, w_bf, b_in, cos, sin, bb, tl, conv_dim, qk_dim, att_dim, transposed_v):
    B, L, D = x.shape
    n = w_bf.shape[1]
    assert not transposed_v or bb == 1
    tok = lambda b, l: (b, l, 0)
    mod = lambda b, l: (b, 0, 0)
    const = lambda b, l: (0, 0)
    out_shapes = [jax.ShapeDtypeStruct((B, L, conv_dim), F32),
                  jax.ShapeDtypeStruct((B, L, qk_dim), F32),
                  jax.ShapeDtypeStruct((B, L, att_dim), F32),
                  jax.ShapeDtypeStruct((B, L, qk_dim), BF16),
                  jax.ShapeDtypeStruct((B, L, qk_dim), BF16),
                  jax.ShapeDtypeStruct((B, L, att_dim), BF16)]
    out_specs = [pl.BlockSpec((bb, tl, s.shape[2]), tok) for s in out_shapes]
    if transposed_v:
        out_shapes[5] = jax.ShapeDtypeStruct((B, att_dim, L), BF16)
        out_specs[5] = pl.BlockSpec((1, att_dim, tl), lambda b, l: (b, 0, l))
    wvt = jnp.transpose(w_bf[:, n - att_dim:])
    bvt = b_in[n - att_dim:].reshape(att_dim, 1)
    return pl.pallas_call(
        functools.partial(_inproj_kernel, conv_dim=conv_dim, qk_dim=qk_dim, transposed_v=transposed_v),
        grid=(B // bb, L // tl),
        in_specs=[pl.BlockSpec((bb, tl, D), tok),
                  pl.BlockSpec((bb, 1, D), mod),
                  pl.BlockSpec((bb, 1, D), mod),
                  pl.BlockSpec((D, n), const),
                  pl.BlockSpec((1, n), const),
                  pl.BlockSpec((tl, qk_dim), lambda b, l: (l, 0)),
                  pl.BlockSpec((tl, qk_dim), lambda b, l: (l, 0)),
                  pl.BlockSpec((att_dim, D), const),
                  pl.BlockSpec((att_dim, 1), const)],
        out_specs=out_specs,
        out_shape=out_shapes,
        compiler_params=_cparams(2),
        name="in_proj",
    )(x, sh1, sc1, w_bf, b_in.reshape(1, n), cos, sin, wvt, bvt)


HIST_PAD = 32


def _conv_kernel(u_ref, prev_ref, hist_ref, w_ref, b_ref, g_ref, beta_ref, o_ref, win_ref, sh_ref, *, width):
    i = pl.program_id(1)
    tl = u_ref.shape[1]

    @pl.when(i == 0)
    def _():
        win_ref[0:HIST_PAD, :] = hist_ref[0]

    @pl.when(i != 0)
    def _():
        win_ref[0:HIST_PAD, :] = prev_ref[0]

    win_ref[HIST_PAD:HIST_PAD + tl, :] = u_ref[0]
    span = tl + HIST_PAD - SUBLANES
    for r in range(1, SUBLANES):
        sh_ref[r - 1] = win_ref[r:r + span, :]
    off = HIST_PAD - (width - 1)
    acc = None
    for j in range(width):
        r, q = (off + j) % SUBLANES, (off + j) // SUBLANES
        rows = pl.ds(q * SUBLANES, tl)
        tap = win_ref[rows, :] if r == 0 else sh_ref[r - 1, rows, :]
        term = w_ref[j:j + 1, :] * tap
        acc = term if acc is None else acc + term
    y = _ln(acc + b_ref[...], g_ref[...], beta_ref[...])
    o_ref[0] = (y * jax.nn.sigmoid(y)).astype(BF16)


def conv_ln(u, hist_pad, w_dw, b_dw, ln_g, ln_b, tl):
    B, L, C = u.shape
    width = w_dw.shape[0]
    steps = tl // HIST_PAD
    row = lambda b, i: (0, 0)
    return pl.pallas_call(
        functools.partial(_conv_kernel, width=width),
        grid=(B, L // tl),
        in_specs=[pl.BlockSpec((1, tl, C), lambda b, i: (b, i, 0)),
                  pl.BlockSpec((1, HIST_PAD, C), lambda b, i: (b, jnp.maximum(i * steps - 1, 0), 0)),
                  pl.BlockSpec((1, HIST_PAD, C), lambda b, i: (b, 0, 0)),
                  pl.BlockSpec((width, C), row),
                  pl.BlockSpec((1, C), row), pl.BlockSpec((1, C), row), pl.BlockSpec((1, C), row)],
        out_specs=pl.BlockSpec((1, tl, C), lambda b, i: (b, i, 0)),
        out_shape=jax.ShapeDtypeStruct((B, L, C), BF16),
        scratch_shapes=[pltpu.VMEM((HIST_PAD + tl, C), F32),
                        pltpu.VMEM((SUBLANES - 1, tl + HIST_PAD - SUBLANES, C), F32)],
        compiler_params=_cparams(2),
        name="conv_ln",
    )(u, u, hist_pad, w_dw, b_dw.reshape(1, C), ln_g.reshape(1, C), ln_b.reshape(1, C))


def _lambda_value(lam_ref, lam_init):
    s1 = jnp.sum(lam_ref[0:1, :] * lam_ref[1:2, :], axis=1, keepdims=True)
    s2 = jnp.sum(lam_ref[2:3, :] * lam_ref[3:4, :], axis=1, keepdims=True)
    return jnp.exp(s1) - jnp.exp(s2) + lam_init


def _split_maps(q):
    lane = lax.broadcasted_iota(jnp.int32, q.shape, 1)
    zero = jnp.zeros_like(q)
    return jnp.where(lane < ATT_HEAD_DIM, q, zero), jnp.where(lane >= ATT_HEAD_DIM, q, zero)


def _finish_heads(o0, o1, lam, g, lam_init):
    o = o0 - lam * o1
    o = o * lax.rsqrt(jnp.mean(o * o, axis=-1, keepdims=True) + LN_EPS) * g
    return o * (1.0 - lam_init)


def _attn_prompt_kernel(lam_ref, g_ref, q_ref, k_ref, vt_ref, o_ref, m_sc, l_sc, acc_sc, *, lam_init):
    i = pl.program_id(2)
    tq = q_ref.shape[1]
    q2 = jnp.concatenate(_split_maps(q_ref[0]), axis=0)
    m_sc[...] = jnp.full(m_sc.shape, -jnp.inf, F32)
    l_sc[...] = jnp.zeros(l_sc.shape, F32)
    acc_sc[...] = jnp.zeros(acc_sc.shape, F32)

    kpos = lax.broadcasted_iota(jnp.int32, (tq, 2 * tq), 0)
    qpos = lax.broadcasted_iota(jnp.int32, (tq, 2 * tq), 1) % tq
    allowed = (kpos // CHUNK) <= (qpos // CHUNK)

    def scores(j):
        kb = k_ref[0, pl.ds(pl.multiple_of(j * tq, tq), tq), :]
        return lax.dot_general(kb, q2, _NT, preferred_element_type=F32)

    def update(s, j):
        vt = vt_ref[0, :, pl.ds(pl.multiple_of(j * tq, tq), tq)]
        m_prev = m_sc[...]
        m_new = jnp.maximum(m_prev, jnp.max(s, axis=0, keepdims=True))
        alpha = jnp.exp(m_prev - m_new)
        pe = jnp.exp(s - m_new)
        l_sc[...] = alpha * l_sc[...] + jnp.sum(pe, axis=0, keepdims=True)
        acc_sc[...] = alpha * acc_sc[...] + jnp.dot(vt, pe.astype(BF16), preferred_element_type=F32)
        m_sc[...] = m_new

    def pair(jj, c):
        s_a, s_b = scores(2 * jj), scores(2 * jj + 1)
        update(s_a, 2 * jj)
        update(s_b, 2 * jj + 1)
        return c

    lax.fori_loop(0, i // 2, pair, 0)

    @pl.when(i % 2 == 1)
    def _():
        s_a, s_b = scores(i - 1), jnp.where(allowed, scores(i), -jnp.inf)
        update(s_a, i - 1)
        update(s_b, i)

    @pl.when(i % 2 == 0)
    def _():
        update(jnp.where(allowed, scores(i), -jnp.inf), i)

    lam = _lambda_value(lam_ref, lam_init)
    o = acc_sc[...] / l_sc[...]
    o = o[:, :tq] - lam * o[:, tq:]
    o = o * lax.rsqrt(jnp.mean(o * o, axis=0, keepdims=True) + LN_EPS) * g_ref[...]
    o_ref[0] = jnp.transpose(o * (1.0 - lam_init)).astype(BF16)


def attn_prompt(qb, kb, vt, lam_rows, subln_g, lam_init, tq):
    B, L, _ = qb.shape
    return pl.pallas_call(
        functools.partial(_attn_prompt_kernel, lam_init=lam_init),
        grid=(B, ATT_HEADS, L // tq),
        in_specs=[pl.BlockSpec((4, ATT_HEAD_DIM), lambda b, h, i: (0, 0)),
                  pl.BlockSpec((ATT_V_DIM, 1), lambda b, h, i: (0, 0)),
                  pl.BlockSpec((1, tq, LANES), lambda b, h, i: (b, i, h)),
                  pl.BlockSpec((1, L, LANES), lambda b, h, i: (b, 0, h)),
                  pl.BlockSpec((1, ATT_V_DIM, L), lambda b, h, i: (b, h, 0))],
        out_specs=pl.BlockSpec((1, tq, LANES), lambda b, h, i: (b, i, h)),
        out_shape=jax.ShapeDtypeStruct((B, L, ATT_HEADS * ATT_V_DIM), BF16),
        scratch_shapes=[pltpu.VMEM((1, 2 * tq), F32), pltpu.VMEM((1, 2 * tq), F32),
                        pltpu.VMEM((ATT_V_DIM, 2 * tq), F32)],
        compiler_params=_cparams(3),
        name="attn_prompt",
    )(lam_rows, subln_g.reshape(ATT_V_DIM, 1), qb, kb, vt)


def _attn_sample_kernel(lam_ref, g_ref, q_ref, ck_ref, cv_ref, kn_ref, vn_ref, o_ref, *, lam_init, past):
    ls = q_ref.shape[1]
    qs = _split_maps(q_ref[0])
    kp = ck_ref[0].astype(BF16)
    vp = cv_ref[0].astype(BF16)
    kn = kn_ref[0]
    vn = vn_ref[0]
    qpos = past + lax.broadcasted_iota(jnp.int32, (ls, ls), 0)
    kpos = past + lax.broadcasted_iota(jnp.int32, (ls, ls), 1)
    allowed = (kpos // CHUNK) <= (qpos // CHUNK)
    outs = []
    for p in range(2):
        sp = lax.dot_general(qs[p], kp, _NT, preferred_element_type=F32)
        sn = lax.dot_general(qs[p], kn, _NT, preferred_element_type=F32)
        sn = jnp.where(allowed, sn, -jnp.inf)
        m = jnp.maximum(jnp.max(sp, axis=1, keepdims=True), jnp.max(sn, axis=1, keepdims=True))
        pp = jnp.exp(sp - m)
        pn = jnp.exp(sn - m)
        denom = jnp.sum(pp, axis=1, keepdims=True) + jnp.sum(pn, axis=1, keepdims=True)
        acc = (jnp.dot(pp.astype(BF16), vp, preferred_element_type=F32)
               + jnp.dot(pn.astype(BF16), vn, preferred_element_type=F32))
        outs.append(acc / denom)
    lam = _lambda_value(lam_ref, lam_init)
    o_ref[0] = _finish_heads(outs[0], outs[1], lam, g_ref[...], lam_init).astype(BF16)


def attn_sample(qb, cache_k, cache_v, kb, vb, lam_rows, subln_g, lam_init):
    B, Ls, _ = qb.shape
    P = cache_k.shape[1]
    blk = lambda n: pl.BlockSpec((1, n, LANES), lambda b, h: (b, 0, h))
    return pl.pallas_call(
        functools.partial(_attn_sample_kernel, lam_init=lam_init, past=P),
        grid=(B, ATT_HEADS),
        in_specs=[pl.BlockSpec((4, ATT_HEAD_DIM), lambda b, h: (0, 0)),
                  pl.BlockSpec((1, ATT_V_DIM), lambda b, h: (0, 0)),
                  blk(Ls), blk(P), blk(P), blk(Ls), blk(Ls)],
        out_specs=blk(Ls),
        out_shape=jax.ShapeDtypeStruct((B, Ls, ATT_HEADS * ATT_V_DIM), BF16),
        compiler_params=_cparams(2),
        name="attn_sample",
    )(lam_rows, subln_g.reshape(1, ATT_V_DIM), qb, cache_k, cache_v, kb, vb)


def _outproj_kernel(conv_ref, att_ref, x_ref, g1_ref, sh2_ref, sc2_ref, wc_ref, wa_ref, b_ref,
                    lg_ref, lb_ref, x1_ref, hh_ref, h2b_ref, *, alpha):
    bb, tl, d = x_ref.shape
    conv = conv_ref[...].reshape(bb * tl, conv_ref.shape[2])
    att = att_ref[...].reshape(bb * tl, att_ref.shape[2])
    mm = (jnp.dot(conv, wc_ref[...], preferred_element_type=F32)
          + jnp.dot(att, wa_ref[...], preferred_element_type=F32) + b_ref[...])
    y = alpha * x_ref[...] + g1_ref[...] * mm.reshape(bb, tl, d)
    x1 = _ln(y, lg_ref[...], lb_ref[...])
    x1_ref[...] = x1
    h2 = x1 * (1.0 + sc2_ref[...]) + sh2_ref[...]
    h2b_ref[...] = h2.astype(BF16)
    hi, lo = _split_bf16(h2.reshape(bb * tl, d))
    bits = [lax.bitcast_convert_type(part.astype(F32), jnp.uint32) for part in (hi, lo)]
    rows = [b[:, c * LANES:(c + 1) * LANES] for b in bits for c in range(d // LANES)]
    for i in range(len(rows) // 2):
        hh_ref[:, i, :] = (rows[2 * i] >> 16) | rows[2 * i + 1]


def out_proj(conv_out, att, x, g1, sh2, sc2, wc_bf, wa_bf, b_out, ln_g, ln_b, alpha, bb, tl):
    B, L, D = x.shape
    tok = lambda b, l: (b, l, 0)
    mod = lambda b, l: (b, 0, 0)
    const = lambda b, l: (0, 0)
    return pl.pallas_call(
        functools.partial(_outproj_kernel, alpha=alpha),
        grid=(B // bb, L // tl),
        in_specs=[pl.BlockSpec((bb, tl, conv_out.shape[2]), tok),
                  pl.BlockSpec((bb, tl, att.shape[2]), tok),
                  pl.BlockSpec((bb, tl, D), tok),
                  pl.BlockSpec((bb, 1, D), mod), pl.BlockSpec((bb, 1, D), mod), pl.BlockSpec((bb, 1, D), mod),
                  pl.BlockSpec(wc_bf.shape, const), pl.BlockSpec(wa_bf.shape, const),
                  pl.BlockSpec((1, D), const), pl.BlockSpec((1, D), const), pl.BlockSpec((1, D), const)],
        out_specs=[pl.BlockSpec((bb, tl, D), tok),
                   pl.BlockSpec((bb * tl, D // LANES, LANES), lambda b, l: (b * (L // tl) + l, 0, 0)),
                   pl.BlockSpec((bb, tl, D), tok)],
        out_shape=[jax.ShapeDtypeStruct((B, L, D), F32),
                   jax.ShapeDtypeStruct((B * L, D // LANES, LANES), jnp.uint32),
                   jax.ShapeDtypeStruct((B, L, D), BF16)],
        compiler_params=_cparams(2),
        name="out_proj",
    )(conv_out, att, x, g1, sh2, sc2, wc_bf, wa_bf, b_out.reshape(1, D), ln_g.reshape(1, D), ln_b.reshape(1, D))


TOPK_STREAMS = 2


def _extract_top(vals, order, count, out, payload=None):
    big = jnp.float32(1e9)
    out_v, out_p = [], []
    for _ in range(count):
        m = jnp.max(vals, axis=0, keepdims=True)
        pos = jnp.min(jnp.where(vals == m, order, big), axis=0, keepdims=True)
        sel = order == pos
        out_v.append(m)
        out_p.append(pos if payload is None else jnp.max(jnp.where(sel, payload, -1.0), axis=0, keepdims=True))
        vals = jnp.where(sel, -jnp.inf, vals)
        yield
    out.append((jnp.concatenate(out_v, axis=0), jnp.concatenate(out_p, axis=0)))


def _interleave(streams):
    done = [0] * len(streams)
    total = max(n for _, n in streams)
    for tick in range(1, total + 1):
        for s, (gen, n) in enumerate(streams):
            while done[s] * total < tick * n:
                next(gen, None)
                done[s] += 1
    for gen, _ in streams:
        for _ in gen:
            pass


class _TopkWork:
    def __init__(self, h_ref, wq_ref, keys_ref, q_sc, e_sc, g_sc, max_streams=TOPK_STREAMS, alternate_halves=True):
        self.keys_ref, self.q_sc, self.e_sc, self.g_sc = keys_ref, q_sc, e_sc, g_sc
        self.alternate_halves = alternate_halves
        self.n_hp, self.n_keys, _ = keys_ref.shape
        tl = h_ref.shape[0]
        K = PEER_TOPK
        qp = jnp.dot(h_ref[...], wq_ref[...], preferred_element_type=F32)
        for hp in range(self.n_hp):
            q_sc[hp] = qp[:, hp * LANES:(hp + 1) * LANES].astype(BF16)
        self.key_iota = lax.broadcasted_iota(jnp.int32, (self.n_keys, LANES), 0).astype(F32)
        assert K == 16
        head_rows = K + (SUBLANES - 1) * SUBLANES
        row = lax.broadcasted_iota(jnp.int32, (head_rows + SUBLANES, LANES), 0)
        ci = jnp.where(row < K, 0, jnp.where(row < head_rows, ((row - K) >> 3) + 1, row - head_rows + SUBLANES))
        cj = jnp.where(row < K, row, jnp.where(row < head_rows, (row - K) & 7, 0))
        limit = jnp.where(ci == 0, 16, jnp.where(ci == 1, 8, jnp.where(ci == 2, 5, jnp.where(
            ci == 3, 4, jnp.where(ci == 4, 3, jnp.where(ci <= 7, 2, 1))))))
        self.cand_ok = cj < limit
        self.cand_order = (ci * K + cj).astype(F32)
        n_chunks = tl // LANES
        self.streams = max_streams if n_chunks % max_streams == 0 else 1
        self.groups = n_chunks // self.streams
        self.n_steps = (self.n_hp // 2) * self.groups

    CHUNK_STEPS = 2 * PEER_TOPK + SUBLANES

    def head_chunk(self, h, col):
        K = PEER_TOPK
        halves = ([], [])

        def take_half(p):
            qc = self.q_sc[2 * h + p, pl.ds(col, LANES), :]
            s = lax.dot_general(self.keys_ref[2 * h + p], qc, _NT, preferred_element_type=F32)
            return _extract_top(s, self.key_iota, K, halves[p])

        if self.alternate_halves:
            takes = [take_half(0), take_half(1)]
            for _ in range(K):
                for take in takes:
                    next(take)
                yield
            for take in takes:
                next(take, None)
        else:
            for p in range(2):
                take = take_half(p)
                for r in range(K):
                    next(take)
                    if (p * K + r) % 2:
                        yield
                next(take, None)
        (sv0, si0), (sv1, si1) = halves[0][0], halves[1][0]
        cv = [sv0[0:1] + sv1]
        ce = [si0[0:1] * self.n_keys + si1]
        for r in range(1, SUBLANES):
            cv.append(sv0[r:r + 1] + sv1[0:SUBLANES])
            ce.append(si0[r:r + 1] * self.n_keys + si1[0:SUBLANES])
            yield
        cv.append(sv0[SUBLANES:K] + sv1[0:1])
        ce.append(si0[SUBLANES:K] * self.n_keys + si1[0:1])
        yield
        cand = jnp.where(self.cand_ok, jnp.concatenate(cv, axis=0), -jnp.inf)
        final = []
        yield from _extract_top(cand, self.cand_order, K, final, payload=jnp.concatenate(ce, axis=0))
        fv, fe = final[0]
        ex = jnp.exp(fv - fv[0:1])
        gate = ex / jnp.sum(ex, axis=0, keepdims=True)
        off = pl.multiple_of(h * K, K)
        self.e_sc[pl.ds(off, K), pl.ds(col, LANES)] = fe
        self.g_sc[pl.ds(off, K), pl.ds(col, LANES)] = gate

    def step_streams(self, i):
        h = i // self.groups
        return [(self.head_chunk(h, pl.multiple_of(((i % self.groups) * self.streams + c) * LANES, LANES)),
                 self.CHUNK_STEPS) for c in range(self.streams)]

    def finish(self, e_ref, g_ref):
        e_ref[...] = jnp.transpose(self.e_sc[...]).astype(jnp.int32) * TABLE_ROWS_PER_EXPERT
        g_ref[...] = jnp.transpose(self.g_sc[...])


def _topk_kernel(h_ref, wq_ref, keys_ref, e_ref, g_ref, q_sc, e_sc, g_sc):
    work = _TopkWork(h_ref, wq_ref, keys_ref, q_sc, e_sc, g_sc)

    def step(i, carry):
        _interleave(work.step_streams(i))
        return carry

    lax.fori_loop(0, work.n_steps, step, 0)
    work.finish(e_ref, g_ref)


def _topk_specs(D, wq_bf, keys_bf, tl, blk0):
    n_hp, n_keys, half = keys_bf.shape
    slots = (n_hp // 2) * PEER_TOPK
    assert half == LANES and n_keys == LANES and PEER_TOPK * PEER_TOPK <= 256
    in_specs = [pl.BlockSpec((tl, D), lambda i: (i + blk0, 0)),
                pl.BlockSpec(wq_bf.shape, lambda i: (0, 0), pipeline_mode=pl.Buffered(1)),
                pl.BlockSpec(keys_bf.shape, lambda i: (0, 0, 0), pipeline_mode=pl.Buffered(1))]
    out_specs = [pl.BlockSpec((tl, slots), lambda i: (i, 0))] * 2
    scratch = [pltpu.VMEM((n_hp, tl, LANES), BF16), pltpu.VMEM((slots, tl), F32), pltpu.VMEM((slots, tl), F32)]
    return slots, in_specs, out_specs, scratch


def peer_topk(h2b, wq_bf, keys_bf, tl, blk0=0, nblk=None):
    T, D = h2b.shape
    nblk = T // tl if nblk is None else nblk
    slots, in_specs, out_specs, scratch = _topk_specs(D, wq_bf, keys_bf, tl, blk0)
    return pl.pallas_call(
        _topk_kernel,
        grid=(nblk,),
        in_specs=in_specs,
        out_specs=out_specs,
        out_shape=[jax.ShapeDtypeStruct((nblk * tl, slots), jnp.int32),
                   jax.ShapeDtypeStruct((nblk * tl, slots), F32)],
        scratch_shapes=scratch,
        compiler_params=_cparams(1),
        name="peer_topk",
    )(h2b, wq_bf, keys_bf)


TABLE_ROWS_PER_EXPERT = 4
PEER_TOKEN_BLOCK = 128
PEER_TOKEN_UNROLL = 16
GATHER_ID_WINDOW = 16


PACK_EXPERT_BLOCK = 512


def _pack_kernel(t_ref, o_ref):
    n = t_ref.shape[0]
    bits = lax.bitcast_convert_type(t_ref[...].astype(BF16).astype(F32), jnp.uint32)
    for i in range(TABLE_ROWS_PER_EXPERT):
        lo = bits[:, (2 * i) * LANES:(2 * i + 1) * LANES]
        hi = bits[:, (2 * i + 1) * LANES:(2 * i + 2) * LANES]
        o_ref[pl.ds(i, n, stride=TABLE_ROWS_PER_EXPERT), :] = (lo >> 16) | hi


def pack_table(tab):
    n, d = tab.shape
    assert d == 2 * LANES * TABLE_ROWS_PER_EXPERT
    tn = min(n, PACK_EXPERT_BLOCK)
    return pl.pallas_call(
        _pack_kernel,
        grid=(n // tn,),
        in_specs=[pl.BlockSpec((tn, d), lambda i: (i, 0))],
        out_specs=pl.BlockSpec((tn * TABLE_ROWS_PER_EXPERT, LANES), lambda i: (i, 0)),
        out_shape=jax.ShapeDtypeStruct((n * TABLE_ROWS_PER_EXPERT, LANES), jnp.uint32),
        compiler_params=_cparams(1),
        name="pack_table",
    )(tab)


def _gather_rows(e_ref, tab_ref, base, slots):
    rows = []
    for j0 in range(0, slots, GATHER_ID_WINDOW):
        ids = e_ref.at[pl.ds(pl.multiple_of(base + j0, GATHER_ID_WINDOW), GATHER_ID_WINDOW)]
        for j in range(GATHER_ID_WINDOW):
            start = pl.multiple_of(ids[j], TABLE_ROWS_PER_EXPERT)
            rows.append(pltpu.bitcast(tab_ref[pl.ds(start, TABLE_ROWS_PER_EXPERT), :], BF16))
    return jnp.concatenate(rows, axis=0)


def _split_bf16(x):
    hi = x.astype(BF16)
    return hi, (x - hi.astype(F32)).astype(BF16)


def _chunk_mask(slots):
    lane = lax.broadcasted_iota(jnp.int32, (SUBLANES, slots * SUBLANES), 1)
    sub = lax.broadcasted_iota(jnp.int32, (SUBLANES, slots * SUBLANES), 0)
    return (lane & (SUBLANES - 1)) == sub


def _slot_spread(slots):
    r = lax.broadcasted_iota(jnp.int32, (slots, slots * SUBLANES), 0)
    c = lax.broadcasted_iota(jnp.int32, (slots, slots * SUBLANES), 1)
    return jnp.where((c >> 3) == r, 1.0, 0.0).astype(BF16)


def _peer_a_tokens(count, t0, z0, e_ref, hh_ref, tab_ref, z_sc, mask, slots):
    for u in range(count):
        t = t0 + u
        g = _gather_rows(e_ref, tab_ref, pl.multiple_of(t * slots, slots), slots)
        hh = pltpu.bitcast(hh_ref[t], BF16)
        z = lax.dot_general(hh, g, _NT, preferred_element_type=F32)
        z8 = z[0:SUBLANES] + z[SUBLANES:2 * SUBLANES]
        z_sc[pl.ds(z0 + u, 1), :] = jnp.sum(jnp.where(mask, z8, 0.0), axis=0, keepdims=True)
        yield


def _peer_a_weights(z_sc, gate, slots):
    zhi, zlo = _split_bf16(z_sc[...])
    spread = _slot_spread(slots)
    a = (lax.dot_general(zhi, spread, _NT, preferred_element_type=F32)
         + lax.dot_general(zlo, spread, _NT, preferred_element_type=F32))
    return 0.5 * a * (1.0 + lax.erf(a * (2.0 ** -0.5))) * gate


def _peer_a_kernel(e_ref, hh_ref, g_ref, tab_ref, w_ref, z_sc, *, slots):
    tb = hh_ref.shape[0]
    mask = _chunk_mask(slots)

    def tokens(i, carry):
        t0 = i * PEER_TOKEN_UNROLL
        for _ in _peer_a_tokens(PEER_TOKEN_UNROLL, t0, t0, e_ref, hh_ref, tab_ref, z_sc, mask, slots):
            pass
        return carry

    lax.fori_loop(0, tb // PEER_TOKEN_UNROLL, tokens, 0)
    w_ref[...] = _peer_a_weights(z_sc, g_ref[...], slots)


def _peer_a_specs(tab_u32, slots, tb, blk0):
    in_specs = [pl.BlockSpec((tb * slots,), lambda i: (i,), memory_space=pltpu.SMEM),
                pl.BlockSpec((tb, SUBLANES, LANES), lambda i: (i + blk0, 0, 0)),
                pl.BlockSpec((tb, slots), lambda i: (i, 0)),
                pl.BlockSpec(tab_u32.shape, lambda i: (0, 0), pipeline_mode=pl.Buffered(1))]
    scratch = [pltpu.VMEM((PEER_TOKEN_BLOCK, slots * SUBLANES), F32)]
    return in_specs, pl.BlockSpec((tb, slots), lambda i: (i, 0)), scratch


def peer_a(e_flat, hh, gate, tab_u32, slots, blk0=0):
    T = gate.shape[0]
    tb = PEER_TOKEN_BLOCK
    in_specs, out_spec, scratch = _peer_a_specs(tab_u32, slots, tb, blk0)
    return pl.pallas_call(
        functools.partial(_peer_a_kernel, slots=slots),
        grid=(T // tb,),
        in_specs=in_specs,
        out_specs=out_spec,
        out_shape=jax.ShapeDtypeStruct((T, slots), F32),
        scratch_shapes=scratch,
        compiler_params=_cparams(1, VMEM_LIMIT_TABLE),
        name="peer_a",
    )(e_flat, hh, gate, tab_u32)


def _topk_a_kernel(e_ref, hh_ref, gate_ref, tab_ref, h_ref, wq_ref, keys_ref, w_ref, e_out_ref, g_out_ref,
                   z_sc, q_sc, e_sc, g_sc, *, slots):
    tl = hh_ref.shape[0]
    work = _TopkWork(h_ref, wq_ref, keys_ref, q_sc, e_sc, g_sc, max_streams=1, alternate_halves=False)
    mask = _chunk_mask(slots)
    per_step = tl // work.n_steps
    flush = PEER_TOKEN_BLOCK // per_step
    assert flush * per_step == PEER_TOKEN_BLOCK

    def step(i, carry):
        gathers = _peer_a_tokens(per_step, i * per_step, (i % flush) * per_step, e_ref, hh_ref, tab_ref, z_sc,
                                 mask, slots)
        _interleave(work.step_streams(i) + [(gathers, per_step)])

        @pl.when(i % flush == flush - 1)
        def _():
            rows = pl.ds(pl.multiple_of((i // flush) * PEER_TOKEN_BLOCK, PEER_TOKEN_BLOCK), PEER_TOKEN_BLOCK)
            w_ref[rows, :] = _peer_a_weights(z_sc, gate_ref[rows, :], slots)

        return carry

    lax.fori_loop(0, work.n_steps, step, 0)
    work.finish(e_out_ref, g_out_ref)


def peer_topk_a(e_flat, hh, gate, tab_u32, h2b, wq_bf, keys_bf, tl, a_blk0, k_blk0):
    T, D = gate.shape[0], h2b.shape[1]
    slots, k_in, k_out, k_scratch = _topk_specs(D, wq_bf, keys_bf, tl, k_blk0)
    a_in, a_out, a_scratch = _peer_a_specs(tab_u32, slots, tl, a_blk0)
    return pl.pallas_call(
        functools.partial(_topk_a_kernel, slots=slots),
        grid=(T // tl,),
        in_specs=a_in + k_in,
        out_specs=[a_out] + k_out,
        out_shape=[jax.ShapeDtypeStruct((T, slots), F32), jax.ShapeDtypeStruct((T, slots), jnp.int32),
                   jax.ShapeDtypeStruct((T, slots), F32)],
        scratch_shapes=a_scratch + k_scratch,
        compiler_params=_cparams(1, VMEM_LIMIT_FUSED),
        name="peer_topk_a",
    )(e_flat, hh, gate, tab_u32, h2b, wq_bf, keys_bf)


def _peer_b_kernel(e_ref, w_ref, tab_ref, o_ref, rep_sc, *, slots):
    tb = o_ref.shape[0]
    spread = _slot_spread(slots)
    whi, wlo = _split_bf16(w_ref[...])
    rep_sc[0] = jnp.dot(whi, spread, preferred_element_type=F32)
    rep_sc[1] = jnp.dot(wlo, spread, preferred_element_type=F32)
    mask = _chunk_mask(slots)
    wide = (SUBLANES, slots * SUBLANES)

    def tokens(i, carry):
        for u in range(PEER_TOKEN_UNROLL):
            t = i * PEER_TOKEN_UNROLL + u
            g = _gather_rows(e_ref, tab_ref, pl.multiple_of(t * slots, slots), slots)
            parts = [jnp.where(mask, jnp.broadcast_to(rep_sc[p, pl.ds(t, 1), :], wide), 0.0).astype(BF16)
                     for p in range(2)]
            r = jnp.dot(jnp.concatenate(parts, axis=0), g, preferred_element_type=F32)
            o_ref[t] = r[0:SUBLANES] + r[SUBLANES:2 * SUBLANES]
        return carry

    lax.fori_loop(0, tb // PEER_TOKEN_UNROLL, tokens, 0)


def peer_b(e_flat, w, tab_u32, slots):
    T = w.shape[0]
    tb = PEER_TOKEN_BLOCK
    return pl.pallas_call(
        functools.partial(_peer_b_kernel, slots=slots),
        grid=(T // tb,),
        in_specs=[pl.BlockSpec((tb * slots,), lambda i: (i,), memory_space=pltpu.SMEM),
                  pl.BlockSpec((tb, slots), lambda i: (i, 0)),
                  pl.BlockSpec(tab_u32.shape, lambda i: (0, 0), pipeline_mode=pl.Buffered(1))],
        out_specs=pl.BlockSpec((tb, SUBLANES, LANES), lambda i: (i, 0, 0)),
        out_shape=jax.ShapeDtypeStruct((T, SUBLANES, LANES), F32),
        scratch_shapes=[pltpu.VMEM((2, tb, slots * SUBLANES), F32)],
        compiler_params=_cparams(1, VMEM_LIMIT_TABLE),
        name="peer_b",
    )(e_flat, w, tab_u32)


def _final_kernel(x_ref, ff_ref, g2_ref, lg_ref, lb_ref, o_ref, *, alpha):
    bb, tl, d = x_ref.shape
    ff = jnp.concatenate([ff_ref[:, c, :] for c in range(d // LANES)], axis=1)
    y = alpha * x_ref[...] + g2_ref[...] * ff.reshape(bb, tl, d)
    o_ref[...] = _ln(y, lg_ref[...], lb_ref[...])


def final_ln(x1, ff, g2, ln_g, ln_b, alpha, bb, tl):
    B, L, D = x1.shape
    tok = lambda b, l: (b, l, 0)
    const = lambda b, l: (0, 0)
    return pl.pallas_call(
        functools.partial(_final_kernel, alpha=alpha),
        grid=(B // bb, L // tl),
        in_specs=[pl.BlockSpec((bb, tl, D), tok),
                  pl.BlockSpec((bb * tl, D // LANES, LANES), lambda b, l: (b * (L // tl) + l, 0, 0)),
                  pl.BlockSpec((bb, 1, D), lambda b, l: (b, 0, 0)),
                  pl.BlockSpec((1, D), const), pl.BlockSpec((1, D), const)],
        out_specs=pl.BlockSpec((bb, tl, D), tok),
        out_shape=jax.ShapeDtypeStruct((B, L, D), F32),
        compiler_params=_cparams(2),
        name="final_ln",
    )(x1, ff, g2, ln_g.reshape(1, D), ln_b.reshape(1, D))


def _rope_tables(past, length, n_groups):
    half = ATT_HEAD_DIM // 2
    inv = 1.0 / (ROPE_THETA ** (jnp.arange(half, dtype=F32) / half))
    ang = (past + jnp.arange(length)).astype(F32)[:, None] * inv[None, :]
    cos = jnp.cos(ang)
    sin = jnp.sin(ang)
    cos_t = jnp.tile(jnp.concatenate([cos, cos], axis=-1), (1, n_groups))
    sin_t = jnp.tile(jnp.concatenate([-sin, sin], axis=-1), (1, n_groups))
    return cos_t, sin_t


def _token_tiles(B, L, target):
    if L >= target:
        return 1, target
    bb = max(1, min(B, target // L))
    while B % bb:
        bb -= 1
    return bb, L


TOPK_TOKEN_BLOCK = 512
PEER_MAX_SPLIT = 8


def _peer_select_and_weigh(h2b, hh, wts, slots):
    T = h2b.shape[0]
    tl = min(T, TOPK_TOKEN_BLOCK)
    wq, keys, tab = wts["w_query"], wts["sub_keys"], wts["u_tab"]
    split = max(s for s in range(1, PEER_MAX_SPLIT + 1) if T % (s * tl) == 0)
    if split == 1 or tl % PEER_TOKEN_BLOCK:
        e, gate = peer_topk(h2b, wq, keys, tl)
        return e, peer_a(e.reshape(T * slots), hh, gate, tab, slots)
    nblk = T // (split * tl)
    chunk = nblk * tl
    e_c, gate_c = peer_topk(h2b, wq, keys, tl, 0, nblk)
    es, ws = [e_c], []
    for c in range(1, split):
        w_c, e_n, gate_n = peer_topk_a(e_c.reshape(chunk * slots), hh, gate_c, tab, h2b, wq, keys, tl,
                                       (c - 1) * nblk, c * nblk)
        ws.append(w_c)
        es.append(e_n)
        e_c, gate_c = e_n, gate_n
    ws.append(peer_a(e_c.reshape(chunk * slots), hh, gate_c, tab, slots, (split - 1) * chunk // PEER_TOKEN_BLOCK))
    return jnp.concatenate(es, axis=0), jnp.concatenate(ws, axis=0)


def _layer(x, mod, conv_hist, k_past, v_past, wts, lam_init, alpha):
    B, L, D = x.shape
    conv_dim = wts["w_dw"].shape[1]
    qk_dim = ATT_HEADS * 2 * ATT_HEAD_DIM
    att_dim = ATT_HEADS * ATT_V_DIM
    width = wts["w_dw"].shape[0]
    assert L >= width - 1 and width - 1 <= HIST_PAD
    sh1, sc1, g1, sh2, sc2, g2 = [m[:, None, :] for m in jnp.split(mod, 6, axis=-1)]
    past = 0 if k_past is None else k_past.shape[1]
    cos, sin = _rope_tables(past, L, qk_dim // ATT_HEAD_DIM)
    bb, tl = _token_tiles(B, L, 512)

    u, k, v, qb, kb, vb = in_proj(x, sh1, sc1, wts["w_in"], wts["b_in"], cos, sin, bb, tl,
                                  conv_dim, qk_dim, att_dim, transposed_v=k_past is None)
    hist_pad = jnp.pad(conv_hist, ((0, 0), (HIST_PAD - (width - 1), 0), (0, 0)))
    conv_out = conv_ln(u, hist_pad, wts["w_dw"], wts["b_dw"], wts["conv_ln_g"], wts["conv_ln_b"], min(L, 256))
    conv_state = u[:, L - (width - 1):, :]

    if k_past is None:
        att = attn_prompt(qb, kb, vb, wts["lam_rows"], wts["subln_g"], lam_init, min(L, 256))
    else:
        att = attn_sample(qb, k_past.reshape(B, past, qk_dim), v_past.reshape(B, past, att_dim), kb, vb,
                          wts["lam_rows"], wts["subln_g"], lam_init)

    x1, hh, h2b = out_proj(conv_out, att, x, g1, sh2, sc2, wts["w_out_conv"], wts["w_out_att"], wts["b_out"],
                           wts["ln1_g"], wts["ln1_b"], alpha, bb, tl)

    T = B * L
    slots = (wts["sub_keys"].shape[0] // 2) * PEER_TOPK
    e, w = _peer_select_and_weigh(h2b.reshape(T, D), hh, wts, slots)
    ff = peer_b(e.reshape(T * slots), w, wts["v_tab"], slots)
    out = final_ln(x1, ff, g2, wts["ln2_g"], wts["ln2_b"], alpha, bb, tl)
    return out, conv_state, k, v


def kernel(x_prompt, x_sample, cache_k, cache_v, cache_conv, c_prompt, c_sample, w_ada, b_ada, w_in, b_in, w_dw, b_dw, conv_ln_g, conv_ln_b, lam_q1, lam_k1, lam_q2, lam_k2, subln_g, w_out, b_out, ln1_g, ln1_b, w_query, sub_keys, u_tab, v_tab, ln2_g, ln2_b):
    depth = w_ada.shape[0]
    D = x_prompt.shape[-1]
    assert D == SUBLANES * LANES
    alpha = (2 * depth) ** 0.25
    Bp, Bs = c_prompt.shape[0], c_sample.shape[0]
    xp, xs = x_prompt, x_sample
    outs = [[] for _ in range(6)]
    for l in range(depth):
        lam_init = 0.8 - 0.6 * math.exp(-0.3 * l)
        conv_dim = w_dw.shape[2]
        n_exp = u_tab.shape[1]
        wts = {
            "w_in": w_in[l].astype(BF16), "b_in": b_in[l], "w_dw": w_dw[l], "b_dw": b_dw[l],
            "conv_ln_g": conv_ln_g[l], "conv_ln_b": conv_ln_b[l],
            "lam_rows": jnp.stack([lam_q1[l], lam_k1[l], lam_q2[l], lam_k2[l]]).astype(F32),
            "subln_g": subln_g[l],
            "w_out_conv": w_out[l, :conv_dim].astype(BF16), "w_out_att": w_out[l, conv_dim:].astype(BF16),
            "b_out": b_out[l], "ln1_g": ln1_g[l], "ln1_b": ln1_b[l],
            "w_query": w_query[l].astype(BF16),
            "sub_keys": sub_keys[l].astype(BF16).reshape(-1, sub_keys.shape[3], sub_keys.shape[4]),
            "u_tab": pack_table(u_tab[l]), "v_tab": pack_table(v_tab[l]),
            "ln2_g": ln2_g[l], "ln2_b": ln2_b[l],
        }
        c_all = jnp.concatenate([c_prompt, c_sample], axis=0)
        pad = (-c_all.shape[0]) % 16
        c_all = jnp.pad(c_all, ((0, pad), (0, 0)))
        mod = ada_mod(c_all, w_ada[l].astype(BF16), b_ada[l])
        hist0 = jnp.zeros((Bp, w_dw.shape[1] - 1, conv_dim), xp.dtype)
        xp, cp, kp, vp = _layer(xp, mod[:Bp], hist0, None, None, wts, lam_init, alpha)
        xs, cs, kn, vn = _layer(xs, mod[Bp:Bp + Bs], cache_conv[l], cache_k[l], cache_v[l], wts, lam_init, alpha)
        for lst, val in zip(outs, (kp, vp, cp, kn, vn, cs)):
            lst.append(val)
    Lp, Ls = x_prompt.shape[1], x_sample.shape[1]
    k_prompt = jnp.stack(outs[0]).reshape(depth, Bp, Lp, ATT_HEADS, 2, ATT_HEAD_DIM)
    v_prompt = jnp.stack(outs[1]).reshape(depth, Bp, Lp, ATT_HEADS, ATT_V_DIM)
    conv_prompt = jnp.stack(outs[2])
    k_sample = jnp.stack(outs[3]).reshape(depth, Bs, Ls, ATT_HEADS, 2, ATT_HEAD_DIM)
    v_sample = jnp.stack(outs[4]).reshape(depth, Bs, Ls, ATT_HEADS, ATT_V_DIM)
    conv_sample = jnp.stack(outs[5])
    return (xp, xs, k_prompt, v_prompt, conv_prompt, k_sample, v_sample, conv_sample)
```

```python
import functools
import math

import jax
import jax.numpy as jnp
import numpy as np
from jax import lax
from jax.experimental import pallas as pl
from jax.experimental.pallas import tpu as pltpu

F32 = jnp.float32
BF16 = jnp.bfloat16

CHUNK = 64
ATT_HEADS = 4
ATT_HEAD_DIM = 64
ATT_V_DIM = 2 * ATT_HEAD_DIM
ROPE_THETA = 10000.0
PEER_TOPK = 16
LN_EPS = 1e-5

LANES = 128
SUBLANES = 8
VMEM_LIMIT_DEFAULT = 48 * 1024 * 1024
VMEM_LIMIT_TABLE = 56 * 1024 * 1024
VMEM_LIMIT_FUSED = 60 * 1024 * 1024

_NT = (((1,), (1,)), ((), ()))


def _cparams(n_axes, vmem=VMEM_LIMIT_DEFAULT):
    return pltpu.CompilerParams(dimension_semantics=("arbitrary",) * n_axes, vmem_limit_bytes=vmem)


def _ln(y, g, b):
    mu = jnp.mean(y, axis=-1, keepdims=True)
    d = y - mu
    var = jnp.mean(d * d, axis=-1, keepdims=True)
    return d * lax.rsqrt(var + LN_EPS) * g + b


def _ada_kernel(c_ref, w_ref, b_ref, o_ref):
    c = c_ref[...]
    s = (c * jax.nn.sigmoid(c)).astype(BF16)
    o_ref[...] = jnp.dot(s, w_ref[...], preferred_element_type=F32) + b_ref[...]


def ada_mod(c, w_bf, b):
    rows, d = c.shape
    n = w_bf.shape[1]
    tn = min(n, 1024)
    return pl.pallas_call(
        _ada_kernel,
        grid=(n // tn,),
        in_specs=[pl.BlockSpec((rows, d), lambda j: (0, 0)),
                  pl.BlockSpec((d, tn), lambda j: (0, j)),
                  pl.BlockSpec((1, tn), lambda j: (0, j))],
        out_specs=pl.BlockSpec((rows, tn), lambda j: (0, j)),
        out_shape=jax.ShapeDtypeStruct((rows, n), F32),
        compiler_params=_cparams(1),
        name="ada_mod",
    )(c, w_bf, b.reshape(1, n))


def _inproj_kernel(x_ref, sh_ref, sc_ref, w_ref, b_ref, cos_ref, sin_ref,
                   wvt_ref, bvt_ref, u_ref, k_ref, v_ref, qb_ref, kb_ref, vb_ref, *, conv_dim, qk_dim,
                   transposed_v):
    bb, tl, d = x_ref.shape
    h = x_ref[...] * (1.0 + sc_ref[...]) + sh_ref[...]
    hb = h.reshape(bb * tl, d).astype(BF16)
    z = jnp.dot(hb, w_ref[...], preferred_element_type=F32) + b_ref[...]
    c0, c1, c2, c3 = conv_dim, 2 * conv_dim, 2 * conv_dim + qk_dim, 2 * conv_dim + 2 * qk_dim
    u = z[:, :c0] * jax.nn.sigmoid(z[:, c0:c1])
    u_ref[...] = u.reshape(bb, tl, conv_dim)

    lane = lax.broadcasted_iota(jnp.int32, (bb * tl, qk_dim), 1)
    first_half = (lane % ATT_HEAD_DIM) < (ATT_HEAD_DIM // 2)
    cos = jnp.broadcast_to(cos_ref[...][None], (bb, tl, qk_dim)).reshape(bb * tl, qk_dim)
    sin = jnp.broadcast_to(sin_ref[...][None], (bb, tl, qk_dim)).reshape(bb * tl, qk_dim)

    def rope(t):
        swapped = jnp.where(first_half,
                            pltpu.roll(t, qk_dim - ATT_HEAD_DIM // 2, axis=1),
                            pltpu.roll(t, ATT_HEAD_DIM // 2, axis=1))
        return t * cos + swapped * sin

    q = rope(z[:, c1:c2]) * (ATT_HEAD_DIM ** -0.5)
    k = rope(z[:, c2:c3])
    v = z[:, c3:]
    k_ref[...] = k.reshape(bb, tl, qk_dim)
    v_ref[...] = v.reshape(bb, tl, v.shape[1])
    qb_ref[...] = q.astype(BF16).reshape(bb, tl, qk_dim)
    kb_ref[...] = k.astype(BF16).reshape(bb, tl, qk_dim)
    if transposed_v:
        vt = lax.dot_general(wvt_ref[...], hb, _NT, preferred_element_type=F32) + bvt_ref[...]
        vb_ref[0] = vt.astype(BF16)
    else:
        vb_ref[...] = v.astype(BF16).reshape(bb, tl, v.shape[1])


def in_proj(x, sh1, sc1, w_bf, b_in, cos, sin, bb, tl, conv_dim, qk_dim, att_dim, transposed_v):
    B, L, D = x.shape
    n = w_bf.shape[1]
    assert not transposed_v or bb == 1
    tok = lambda b, l: (b, l, 0)
    mod = lambda b, l: (b, 0, 0)
    const = lambda b, l: (0, 0)
    out_shapes = [jax.ShapeDtypeStruct((B, L, conv_dim), F32),
                  jax.ShapeDtypeStruct((B, L, qk_dim), F32),
                  jax.ShapeDtypeStruct((B, L, att_dim), F32),
                  jax.ShapeDtypeStruct((B, L, qk_dim), BF16),
                  jax.ShapeDtypeStruct((B, L, qk_dim), BF16),
                  jax.ShapeDtypeStruct((B, L, att_dim), BF16)]
    out_specs = [pl.BlockSpec((bb, tl, s.shape[2]), tok) for s in out_shapes]
    if transposed_v:
        out_shapes[5] = jax.ShapeDtypeStruct((B, att_dim, L), BF16)
        out_specs[5] = pl.BlockSpec((1, att_dim, tl), lambda b, l: (b, 0, l))
    wvt = jnp.transpose(w_bf[:, n - att_dim:])
    bvt = b_in[n - att_dim:].reshape(att_dim, 1)
    return pl.pallas_call(
        functools.partial(_inproj_kernel, conv_dim=conv_dim, qk_dim=qk_dim, transposed_v=transposed_v),
        grid=(B // bb, L // tl),
        in_specs=[pl.BlockSpec((bb, tl, D), tok),
                  pl.BlockSpec((bb, 1, D), mod),
                  pl.BlockSpec((bb, 1, D), mod),
                  pl.BlockSpec((D, n), const),
                  pl.BlockSpec((1, n), const),
                  pl.BlockSpec((tl, qk_dim), lambda b, l: (l, 0)),
                  pl.BlockSpec((tl, qk_dim), lambda b, l: (l, 0)),
                  pl.BlockSpec((att_dim, D), const),
                  pl.BlockSpec((att_dim, 1), const)],
        out_specs=out_specs,
        out_shape=out_shapes,
        compiler_params=_cparams(2),
        name="in_proj",
    )(x, sh1, sc1, w_bf, b_in.reshape(1, n), cos, sin, wvt, bvt)


HIST_PAD = 32


def _conv_kernel(u_ref, prev_ref, hist_ref, w_ref, b_ref, g_ref, beta_ref, o_ref, win_ref, sh_ref, *, width):
    i = pl.program_id(1)
    tl = u_ref.shape[1]

    @pl.when(i == 0)
    def _():
        win_ref[0:HIST_PAD, :] = hist_ref[0]

    @pl.when(i != 0)
    def _():
        win_ref[0:HIST_PAD, :] = prev_ref[0]

    win_ref[HIST_PAD:HIST_PAD + tl, :] = u_ref[0]
    span = tl + HIST_PAD - SUBLANES
    for r in range(1, SUBLANES):
        sh_ref[r - 1] = win_ref[r:r + span, :]
    off = HIST_PAD - (width - 1)
    acc = None
    for j in range(width):
        r, q = (off + j) % SUBLANES, (off + j) // SUBLANES
        rows = pl.ds(q * SUBLANES, tl)
        tap = win_ref[rows, :] if r == 0 else sh_ref[r - 1, rows, :]
        term = w_ref[j:j + 1, :] * tap
        acc = term if acc is None else acc + term
    y = _ln(acc + b_ref[...], g_ref[...], beta_ref[...])
    o_ref[0] = (y * jax.nn.sigmoid(y)).astype(BF16)


def conv_ln(u, hist_pad, w_dw, b_dw, ln_g, ln_b, tl):
    B, L, C = u.shape
    width = w_dw.shape[0]
    steps = tl // HIST_PAD
    row = lambda b, i: (0, 0)
    return pl.pallas_call(
        functools.partial(_conv_kernel, width=width),
        grid=(B, L // tl),
        in_specs=[pl.BlockSpec((1, tl, C), lambda b, i: (b, i, 0)),
                  pl.BlockSpec((1, HIST_PAD, C), lambda b, i: (b, jnp.maximum(i * steps - 1, 0), 0)),
                  pl.BlockSpec((1, HIST_PAD, C), lambda b, i: (b, 0, 0)),
                  pl.BlockSpec((width, C), row),
                  pl.BlockSpec((1, C), row), pl.BlockSpec((1, C), row), pl.BlockSpec((1, C), row)],
        out_specs=pl.BlockSpec((1, tl, C), lambda b, i: (b, i, 0)),
        out_shape=jax.ShapeDtypeStruct((B, L, C), BF16),
        scratch_shapes=[pltpu.VMEM((HIST_PAD + tl, C), F32),
                        pltpu.VMEM((SUBLANES - 1, tl + HIST_PAD - SUBLANES, C), F32)],
        compiler_params=_cparams(2),
        name="conv_ln",
    )(u, u, hist_pad, w_dw, b_dw.reshape(1, C), ln_g.reshape(1, C), ln_b.reshape(1, C))


def _lambda_value(lam_ref, lam_init):
    s1 = jnp.sum(lam_ref[0:1, :] * lam_ref[1:2, :], axis=1, keepdims=True)
    s2 = jnp.sum(lam_ref[2:3, :] * lam_ref[3:4, :], axis=1, keepdims=True)
    return jnp.exp(s1) - jnp.exp(s2) + lam_init


def _split_maps(q):
    lane = lax.broadcasted_iota(jnp.int32, q.shape, 1)
    zero = jnp.zeros_like(q)
    return jnp.where(lane < ATT_HEAD_DIM, q, zero), jnp.where(lane >= ATT_HEAD_DIM, q, zero)


def _finish_heads(o0, o1, lam, g, lam_init):
    o = o0 - lam * o1
    o = o * lax.rsqrt(jnp.mean(o * o, axis=-1, keepdims=True) + LN_EPS) * g
    return o * (1.0 - lam_init)


def _attn_prompt_kernel(lam_ref, g_ref, q_ref, k_ref, vt_ref, o_ref, m_sc, l_sc, acc_sc, *, lam_init):
    i = pl.program_id(2)
    tq = q_ref.shape[1]
    q2 = jnp.concatenate(_split_maps(q_ref[0]), axis=0)
    m_sc[...] = jnp.full(m_sc.shape, -jnp.inf, F32)
    l_sc[...] = jnp.zeros(l_sc.shape, F32)
    acc_sc[...] = jnp.zeros(acc_sc.shape, F32)

    kpos = lax.broadcasted_iota(jnp.int32, (tq, 2 * tq), 0)
    qpos = lax.broadcasted_iota(jnp.int32, (tq, 2 * tq), 1) % tq
    allowed = (kpos // CHUNK) <= (qpos // CHUNK)

    def scores(j):
        kb = k_ref[0, pl.ds(pl.multiple_of(j * tq, tq), tq), :]
        return lax.dot_general(kb, q2, _NT, preferred_element_type=F32)

    def update(s, j):
        vt = vt_ref[0, :, pl.ds(pl.multiple_of(j * tq, tq), tq)]
        m_prev = m_sc[...]
        m_new = jnp.maximum(m_prev, jnp.max(s, axis=0, keepdims=True))
        alpha = jnp.exp(m_prev - m_new)
        pe = jnp.exp(s - m_new)
        l_sc[...] = alpha * l_sc[...] + jnp.sum(pe, axis=0, keepdims=True)
        acc_sc[...] = alpha * acc_sc[...] + jnp.dot(vt, pe.astype(BF16), preferred_element_type=F32)
        m_sc[...] = m_new

    def pair(jj, c):
        s_a, s_b = scores(2 * jj), scores(2 * jj + 1)
        update(s_a, 2 * jj)
        update(s_b, 2 * jj + 1)
        return c

    lax.fori_loop(0, i // 2, pair, 0)

    @pl.when(i % 2 == 1)
    def _():
        s_a, s_b = scores(i - 1), jnp.where(allowed, scores(i), -jnp.inf)
        update(s_a, i - 1)
        update(s_b, i)

    @pl.when(i % 2 == 0)
    def _():
        update(jnp.where(allowed, scores(i), -jnp.inf), i)

    lam = _lambda_value(lam_ref, lam_init)
    o = acc_sc[...] / l_sc[...]
    o = o[:, :tq] - lam * o[:, tq:]
    o = o * lax.rsqrt(jnp.mean(o * o, axis=0, keepdims=True) + LN_EPS) * g_ref[...]
    o_ref[0] = jnp.transpose(o * (1.0 - lam_init)).astype(BF16)


def attn_prompt(qb, kb, vt, lam_rows, subln_g, lam_init, tq):
    B, L, _ = qb.shape
    return pl.pallas_call(
        functools.partial(_attn_prompt_kernel, lam_init=lam_init),
        grid=(B, ATT_HEADS, L // tq),
        in_specs=[pl.BlockSpec((4, ATT_HEAD_DIM), lambda b, h, i: (0, 0)),
                  pl.BlockSpec((ATT_V_DIM, 1), lambda b, h, i: (0, 0)),
                  pl.BlockSpec((1, tq, LANES), lambda b, h, i: (b, i, h)),
                  pl.BlockSpec((1, L, LANES), lambda b, h, i: (b, 0, h)),
                  pl.BlockSpec((1, ATT_V_DIM, L), lambda b, h, i: (b, h, 0))],
        out_specs=pl.BlockSpec((1, tq, LANES), lambda b, h, i: (b, i, h)),
        out_shape=jax.ShapeDtypeStruct((B, L, ATT_HEADS * ATT_V_DIM), BF16),
        scratch_shapes=[pltpu.VMEM((1, 2 * tq), F32), pltpu.VMEM((1, 2 * tq), F32),
                        pltpu.VMEM((ATT_V_DIM, 2 * tq), F32)],
        compiler_params=_cparams(3),
        name="attn_prompt",
    )(lam_rows, subln_g.reshape(ATT_V_DIM, 1), qb, kb, vt)


def _attn_sample_kernel(lam_ref, g_ref, q_ref, ck_ref, cv_ref, kn_ref, vn_ref, o_ref, *, lam_init, past):
    ls = q_ref.shape[1]
    qs = _split_maps(q_ref[0])
    kp = ck_ref[0].astype(BF16)
    vp = cv_ref[0].astype(BF16)
    kn = kn_ref[0]
    vn = vn_ref[0]
    qpos = past + lax.broadcasted_iota(jnp.int32, (ls, ls), 0)
    kpos = past + lax.broadcasted_iota(jnp.int32, (ls, ls), 1)
    allowed = (kpos // CHUNK) <= (qpos // CHUNK)
    outs = []
    for p in range(2):
        sp = lax.dot_general(qs[p], kp, _NT, preferred_element_type=F32)
        sn = lax.dot_general(qs[p], kn, _NT, preferred_element_type=F32)
        sn = jnp.where(allowed, sn, -jnp.inf)
        m = jnp.maximum(jnp.max(sp, axis=1, keepdims=True), jnp.max(sn, axis=1, keepdims=True))
        pp = jnp.exp(sp - m)
        pn = jnp.exp(sn - m)
        denom = jnp.sum(pp, axis=1, keepdims=True) + jnp.sum(pn, axis=1, keepdims=True)
        acc = (jnp.dot(pp.astype(BF16), vp, preferred_element_type=F32)
               + jnp.dot(pn.astype(BF16), vn, preferred_element_type=F32))
        outs.append(acc / denom)
    lam = _lambda_value(lam_ref, lam_init)
    o_ref[0] = _finish_heads(outs[0], outs[1], lam, g_ref[...], lam_init).astype(BF16)


def attn_sample(qb, cache_k, cache_v, kb, vb, lam_rows, subln_g, lam_init):
    B, Ls, _ = qb.shape
    P = cache_k.shape[1]
    blk = lambda n: pl.BlockSpec((1, n, LANES), lambda b, h: (b, 0, h))
    return pl.pallas_call(
        functools.partial(_attn_sample_kernel, lam_init=lam_init, past=P),
        grid=(B, ATT_HEADS),
        in_specs=[pl.BlockSpec((4, ATT_HEAD_DIM), lambda b, h: (0, 0)),
                  pl.BlockSpec((1, ATT_V_DIM), lambda b, h: (0, 0)),
                  blk(Ls), blk(P), blk(P), blk(Ls), blk(Ls)],
        out_specs=blk(Ls),
        out_shape=jax.ShapeDtypeStruct((B, Ls, ATT_HEADS * ATT_V_DIM), BF16),
        compiler_params=_cparams(2),
        name="attn_sample",
    )(lam_rows, subln_g.reshape(1, ATT_V_DIM), qb, cache_k, cache_v, kb, vb)


def _outproj_kernel(conv_ref, att_ref, x_ref, g1_ref, sh2_ref, sc2_ref, wc_ref, wa_ref, b_ref,
                    lg_ref, lb_ref, x1_ref, hh_ref, h2b_ref, *, alpha):
    bb, tl, d = x_ref.shape
    conv = conv_ref[...].reshape(bb * tl, conv_ref.shape[2])
    att = att_ref[...].reshape(bb * tl, att_ref.shape[2])
    mm = (jnp.dot(conv, wc_ref[...], preferred_element_type=F32)
          + jnp.dot(att, wa_ref[...], preferred_element_type=F32) + b_ref[...])
    y = alpha * x_ref[...] + g1_ref[...] * mm.reshape(bb, tl, d)
    x1 = _ln(y, lg_ref[...], lb_ref[...])
    x1_ref[...] = x1
    h2 = x1 * (1.0 + sc2_ref[...]) + sh2_ref[...]
    h2b_ref[...] = h2.astype(BF16)
    hi, lo = _split_bf16(h2.reshape(bb * tl, d))
    bits = [lax.bitcast_convert_type(part.astype(F32), jnp.uint32) for part in (hi, lo)]
    rows = [b[:, c * LANES:(c + 1) * LANES] for b in bits for c in range(d // LANES)]
    for i in range(len(rows) // 2):
        hh_ref[:, i, :] = (rows[2 * i] >> 16) | rows[2 * i + 1]


def out_proj(conv_out, att, x, g1, sh2, sc2, wc_bf, wa_bf, b_out, ln_g, ln_b, alpha, bb, tl):
    B, L, D = x.shape
    tok = lambda b, l: (b, l, 0)
    mod = lambda b, l: (b, 0, 0)
    const = lambda b, l: (0, 0)
    return pl.pallas_call(
        functools.partial(_outproj_kernel, alpha=alpha),
        grid=(B // bb, L // tl),
        in_specs=[pl.BlockSpec((bb, tl, conv_out.shape[2]), tok),
                  pl.BlockSpec((bb, tl, att.shape[2]), tok),
                  pl.BlockSpec((bb, tl, D), tok),
                  pl.BlockSpec((bb, 1, D), mod), pl.BlockSpec((bb, 1, D), mod), pl.BlockSpec((bb, 1, D), mod),
                  pl.BlockSpec(wc_bf.shape, const), pl.BlockSpec(wa_bf.shape, const),
                  pl.BlockSpec((1, D), const), pl.BlockSpec((1, D), const), pl.BlockSpec((1, D), const)],
        out_specs=[pl.BlockSpec((bb, tl, D), tok),
                   pl.BlockSpec((bb * tl, D // LANES, LANES), lambda b, l: (b * (L // tl) + l, 0, 0)),
                   pl.BlockSpec((bb, tl, D), tok)],
        out_shape=[jax.ShapeDtypeStruct((B, L, D), F32),
                   jax.ShapeDtypeStruct((B * L, D // LANES, LANES), jnp.uint32),
                   jax.ShapeDtypeStruct((B, L, D), BF16)],
        compiler_params=_cparams(2),
        name="out_proj",
    )(conv_out, att, x, g1, sh2, sc2, wc_bf, wa_bf, b_out.reshape(1, D), ln_g.reshape(1, D), ln_b.reshape(1, D))


TOPK_STREAMS = 2


def _extract_top(vals, order, count, out, payload=None):
    big = jnp.float32(1e9)
    out_v, out_p = [], []
    for _ in range(count):
        m = jnp.max(vals, axis=0, keepdims=True)
        pos = jnp.min(jnp.where(vals == m, order, big), axis=0, keepdims=True)
        sel = order == pos
        out_v.append(m)
        out_p.append(pos if payload is None else jnp.max(jnp.where(sel, payload, -1.0), axis=0, keepdims=True))
        vals = jnp.where(sel, -jnp.inf, vals)
        yield
    out.append((jnp.concatenate(out_v, axis=0), jnp.concatenate(out_p, axis=0)))


def _interleave(streams):
    done = [0] * len(streams)
    total = max(n for _, n in streams)
    for tick in range(1, total + 1):
        for s, (gen, n) in enumerate(streams):
            while done[s] * total < tick * n:
                next(gen, None)
                done[s] += 1
    for gen, _ in streams:
        for _ in gen:
            pass


class _TopkWork:
    def __init__(self, h_ref, wq_ref, keys_ref, q_sc, e_sc, g_sc, max_streams=TOPK_STREAMS, alternate_halves=True):
        self.keys_ref, self.q_sc, self.e_sc, self.g_sc = keys_ref, q_sc, e_sc, g_sc
        self.alternate_halves = alternate_halves
        self.n_hp, self.n_keys, _ = keys_ref.shape
        tl = h_ref.shape[0]
        K = PEER_TOPK
        qp = jnp.dot(h_ref[...], wq_ref[...], preferred_element_type=F32)
        for hp in range(self.n_hp):
            q_sc[hp] = qp[:, hp * LANES:(hp + 1) * LANES].astype(BF16)
        self.key_iota = lax.broadcasted_iota(jnp.int32, (self.n_keys, LANES), 0).astype(F32)
        assert K == 16
        head_rows = K + (SUBLANES - 1) * SUBLANES
        row = lax.broadcasted_iota(jnp.int32, (head_rows + SUBLANES, LANES), 0)
        ci = jnp.where(row < K, 0, jnp.where(row < head_rows, ((row - K) >> 3) + 1, row - head_rows + SUBLANES))
        cj = jnp.where(row < K, row, jnp.where(row < head_rows, (row - K) & 7, 0))
        limit = jnp.where(ci == 0, 16, jnp.where(ci == 1, 8, jnp.where(ci == 2, 5, jnp.where(
            ci == 3, 4, jnp.where(ci == 4, 3, jnp.where(ci <= 7, 2, 1))))))
        self.cand_ok = cj < limit
        self.cand_order = (ci * K + cj).astype(F32)
        n_chunks = tl // LANES
        self.streams = max_streams if n_chunks % max_streams == 0 else 1
        self.groups = n_chunks // self.streams
        self.n_steps = (self.n_hp // 2) * self.groups

    CHUNK_STEPS = 2 * PEER_TOPK + SUBLANES

    def head_chunk(self, h, col):
        K = PEER_TOPK
        halves = ([], [])

        def take_half(p):
            qc = self.q_sc[2 * h + p, pl.ds(col, LANES), :]
            s = lax.dot_general(self.keys_ref[2 * h + p], qc, _NT, preferred_element_type=F32)
            return _extract_top(s, self.key_iota, K, halves[p])

        if self.alternate_halves:
            takes = [take_half(0), take_half(1)]
            for _ in range(K):
                for take in takes:
                    next(take)
                yield
            for take in takes:
                next(take, None)
        else:
            for p in range(2):
                take = take_half(p)
                for r in range(K):
                    next(take)
                    if (p * K + r) % 2:
                        yield
                next(take, None)
        (sv0, si0), (sv1, si1) = halves[0][0], halves[1][0]
        cv = [sv0[0:1] + sv1]
        ce = [si0[0:1] * self.n_keys + si1]
        for r in range(1, SUBLANES):
            cv.append(sv0[r:r + 1] + sv1[0:SUBLANES])
            ce.append(si0[r:r + 1] * self.n_keys + si1[0:SUBLANES])
            yield
        cv.append(sv0[SUBLANES:K] + sv1[0:1])
        ce.append(si0[SUBLANES:K] * self.n_keys + si1[0:1])
        yield
        cand = jnp.where(self.cand_ok, jnp.concatenate(cv, axis=0), -jnp.inf)
        final = []
        yield from _extract_top(cand, self.cand_order, K, final, payload=jnp.concatenate(ce, axis=0))
        fv, fe = final[0]
        ex = jnp.exp(fv - fv[0:1])
        gate = ex / jnp.sum(ex, axis=0, keepdims=True)
        off = pl.multiple_of(h * K, K)
        self.e_sc[pl.ds(off, K), pl.ds(col, LANES)] = fe
        self.g_sc[pl.ds(off, K), pl.ds(col, LANES)] = gate

    def step_streams(self, i):
        h = i // self.groups
        return [(self.head_chunk(h, pl.multiple_of(((i % self.groups) * self.streams + c) * LANES, LANES)),
                 self.CHUNK_STEPS) for c in range(self.streams)]

    def finish(self, e_ref, g_ref):
        e_ref[...] = jnp.transpose(self.e_sc[...]).astype(jnp.int32) * TABLE_ROWS_PER_EXPERT
        g_ref[...] = jnp.transpose(self.g_sc[...])


def _topk_kernel(h_ref, wq_ref, keys_ref, e_ref, g_ref, q_sc, e_sc, g_sc):
    work = _TopkWork(h_ref, wq_ref, keys_ref, q_sc, e_sc, g_sc)

    def step(i, carry):
        _interleave(work.step_streams(i))
        return carry

    lax.fori_loop(0, work.n_steps, step, 0)
    work.finish(e_ref, g_ref)


def _topk_specs(D, wq_bf, keys_bf, tl, blk0):
    n_hp, n_keys, half = keys_bf.shape
    slots = (n_hp // 2) * PEER_TOPK
    assert half == LANES and n_keys == LANES and PEER_TOPK * PEER_TOPK <= 256
    in_specs = [pl.BlockSpec((tl, D), lambda i: (i + blk0, 0)),
                pl.BlockSpec(wq_bf.shape, lambda i: (0, 0), pipeline_mode=pl.Buffered(1)),
                pl.BlockSpec(keys_bf.shape, lambda i: (0, 0, 0), pipeline_mode=pl.Buffered(1))]
    out_specs = [pl.BlockSpec((tl, slots), lambda i: (i, 0))] * 2
    scratch = [pltpu.VMEM((n_hp, tl, LANES), BF16), pltpu.VMEM((slots, tl), F32), pltpu.VMEM((slots, tl), F32)]
    return slots, in_specs, out_specs, scratch


def peer_topk(h2b, wq_bf, keys_bf, tl, blk0=0, nblk=None):
    T, D = h2b.shape
    nblk = T // tl if nblk is None else nblk
    slots, in_specs, out_specs, scratch = _topk_specs(D, wq_bf, keys_bf, tl, blk0)
    return pl.pallas_call(
        _topk_kernel,
        grid=(nblk,),
        in_specs=in_specs,
        out_specs=out_specs,
        out_shape=[jax.ShapeDtypeStruct((nblk * tl, slots), jnp.int32),
                   jax.ShapeDtypeStruct((nblk * tl, slots), F32)],
        scratch_shapes=scratch,
        compiler_params=_cparams(1),
        name="peer_topk",
    )(h2b, wq_bf, keys_bf)


TABLE_ROWS_PER_EXPERT = 4
PEER_TOKEN_BLOCK = 128
PEER_TOKEN_UNROLL = 32
GATHER_ID_WINDOW = 16


PACK_EXPERT_BLOCK = 512


def _pack_kernel(t_ref, o_ref):
    n = t_ref.shape[0]
    bits = lax.bitcast_convert_type(t_ref[...].astype(BF16).astype(F32), jnp.uint32)
    for i in range(TABLE_ROWS_PER_EXPERT):
        lo = bits[:, (2 * i) * LANES:(2 * i + 1) * LANES]
        hi = bits[:, (2 * i + 1) * LANES:(2 * i + 2) * LANES]
        o_ref[pl.ds(i, n, stride=TABLE_ROWS_PER_EXPERT), :] = (lo >> 16) | hi


def pack_table(tab):
    n, d = tab.shape
    assert d == 2 * LANES * TABLE_ROWS_PER_EXPERT
    tn = min(n, PACK_EXPERT_BLOCK)
    return pl.pallas_call(
        _pack_kernel,
        grid=(n // tn,),
        in_specs=[pl.BlockSpec((tn, d), lambda i: (i, 0))],
        out_specs=pl.BlockSpec((tn * TABLE_ROWS_PER_EXPERT, LANES), lambda i: (i, 0)),
        out_shape=jax.ShapeDtypeStruct((n * TABLE_ROWS_PER_EXPERT, LANES), jnp.uint32),
        compiler_params=_cparams(1),
        name="pack_table",
    )(tab)


def _gather_rows(e_ref, tab_ref, base, slots):
    rows = []
    for j0 in range(0, slots, GATHER_ID_WINDOW):
        ids = e_ref.at[pl.ds(pl.multiple_of(base + j0, GATHER_ID_WINDOW), GATHER_ID_WINDOW)]
        for j in range(GATHER_ID_WINDOW):
            start = pl.multiple_of(ids[j], TABLE_ROWS_PER_EXPERT)
            rows.append(pltpu.bitcast(tab_ref[pl.ds(start, TABLE_ROWS_PER_EXPERT), :], BF16))
    return jnp.concatenate(rows, axis=0)


def _split_bf16(x):
    hi = x.astype(BF16)
    return hi, (x - hi.astype(F32)).astype(BF16)


def _chunk_mask(slots):
    lane = lax.broadcasted_iota(jnp.int32, (SUBLANES, slots * SUBLANES), 1)
    sub = lax.broadcasted_iota(jnp.int32, (SUBLANES, slots * SUBLANES), 0)
    return (lane & (SUBLANES - 1)) == sub


def _slot_spread(slots):
    r = lax.broadcasted_iota(jnp.int32, (slots, slots * SUBLANES), 0)
    c = lax.broadcasted_iota(jnp.int32, (slots, slots * SUBLANES), 1)
    return jnp.where((c >> 3) == r, 1.0, 0.0).astype(BF16)


def _peer_a_tokens(count, t0, z0, e_ref, hh_ref, tab_ref, z_sc, mask, slots):
    for u in range(count):
        t = t0 + u
        g = _gather_rows(e_ref, tab_ref, pl.multiple_of(t * slots, slots), slots)
        hh = pltpu.bitcast(hh_ref[t], BF16)
        z = lax.dot_general(hh, g, _NT, preferred_element_type=F32)
        z8 = z[0:SUBLANES] + z[SUBLANES:2 * SUBLANES]
        z_sc[pl.ds(z0 + u, 1), :] = jnp.sum(jnp.where(mask, z8, 0.0), axis=0, keepdims=True)
        yield


def _peer_a_weights(z_sc, gate, slots):
    zhi, zlo = _split_bf16(z_sc[...])
    spread = _slot_spread(slots)
    a = (lax.dot_general(zhi, spread, _NT, preferred_element_type=F32)
         + lax.dot_general(zlo, spread, _NT, preferred_element_type=F32))
    return 0.5 * a * (1.0 + lax.erf(a * (2.0 ** -0.5))) * gate


def _peer_a_kernel(e_ref, hh_ref, g_ref, tab_ref, w_ref, z_sc, *, slots):
    tb = hh_ref.shape[0]
    mask = _chunk_mask(slots)

    def tokens(i, carry):
        t0 = i * PEER_TOKEN_UNROLL
        for _ in _peer_a_tokens(PEER_TOKEN_UNROLL, t0, t0, e_ref, hh_ref, tab_ref, z_sc, mask, slots):
            pass
        return carry

    lax.fori_loop(0, tb // PEER_TOKEN_UNROLL, tokens, 0)
    w_ref[...] = _peer_a_weights(z_sc, g_ref[...], slots)


def _peer_a_specs(tab_u32, slots, tb, blk0):
    in_specs = [pl.BlockSpec((tb * slots,), lambda i: (i,), memory_space=pltpu.SMEM),
                pl.BlockSpec((tb, SUBLANES, LANES), lambda i: (i + blk0, 0, 0)),
                pl.BlockSpec((tb, slots), lambda i: (i, 0)),
                pl.BlockSpec(tab_u32.shape, lambda i: (0, 0), pipeline_mode=pl.Buffered(1))]
    scratch = [pltpu.VMEM((PEER_TOKEN_BLOCK, slots * SUBLANES), F32)]
    return in_specs, pl.BlockSpec((tb, slots), lambda i: (i, 0)), scratch


def peer_a(e_flat, hh, gate, tab_u32, slots, blk0=0):
    T = gate.shape[0]
    tb = PEER_TOKEN_BLOCK
    in_specs, out_spec, scratch = _peer_a_specs(tab_u32, slots, tb, blk0)
    return pl.pallas_call(
        functools.partial(_peer_a_kernel, slots=slots),
        grid=(T // tb,),
        in_specs=in_specs,
        out_specs=out_spec,
        out_shape=jax.ShapeDtypeStruct((T, slots), F32),
        scratch_shapes=scratch,
        compiler_params=_cparams(1, VMEM_LIMIT_TABLE),
        name="peer_a",
    )(e_flat, hh, gate, tab_u32)


def _topk_a_kernel(e_ref, hh_ref, gate_ref, tab_ref, h_ref, wq_ref, keys_ref, w_ref, e_out_ref, g_out_ref,
                   z_sc, q_sc, e_sc, g_sc, *, slots):
    tl = hh_ref.shape[0]
    work = _TopkWork(h_ref, wq_ref, keys_ref, q_sc, e_sc, g_sc, max_streams=1, alternate_halves=False)
    mask = _chunk_mask(slots)
    per_step = tl // work.n_steps
    flush = PEER_TOKEN_BLOCK // per_step
    assert flush * per_step == PEER_TOKEN_BLOCK

    def step(i, carry):
        gathers = _peer_a_tokens(per_step, i * per_step, (i % flush) * per_step, e_ref, hh_ref, tab_ref, z_sc,
                                 mask, slots)
        _interleave(work.step_streams(i) + [(gathers, per_step)])

        @pl.when(i % flush == flush - 1)
        def _():
            rows = pl.ds(pl.multiple_of((i // flush) * PEER_TOKEN_BLOCK, PEER_TOKEN_BLOCK), PEER_TOKEN_BLOCK)
            w_ref[rows, :] = _peer_a_weights(z_sc, gate_ref[rows, :], slots)

        return carry

    lax.fori_loop(0, work.n_steps, step, 0)
    work.finish(e_out_ref, g_out_ref)


def peer_topk_a(e_flat, hh, gate, tab_u32, h2b, wq_bf, keys_bf, tl, a_blk0, k_blk0):
    T, D = gate.shape[0], h2b.shape[1]
    slots, k_in, k_out, k_scratch = _topk_specs(D, wq_bf, keys_bf, tl, k_blk0)
    a_in, a_out, a_scratch = _peer_a_specs(tab_u32, slots, tl, a_blk0)
    return pl.pallas_call(
        functools.partial(_topk_a_kernel, slots=slots),
        grid=(T // tl,),
        in_specs=a_in + k_in,
        out_specs=[a_out] + k_out,
        out_shape=[jax.ShapeDtypeStruct((T, slots), F32), jax.ShapeDtypeStruct((T, slots), jnp.int32),
                   jax.ShapeDtypeStruct((T, slots), F32)],
        scratch_shapes=a_scratch + k_scratch,
        compiler_params=_cparams(1, VMEM_LIMIT_FUSED),
        name="peer_topk_a",
    )(e_flat, hh, gate, tab_u32, h2b, wq_bf, keys_bf)


def _peer_b_kernel(e_ref, w_ref, tab_ref, o_ref, rep_sc, *, slots):
    tb = o_ref.shape[0]
    spread = _slot_spread(slots)
    whi, wlo = _split_bf16(w_ref[...])
    rep_sc[0] = jnp.dot(whi, spread, preferred_element_type=F32)
    rep_sc[1] = jnp.dot(wlo, spread, preferred_element_type=F32)
    mask = _chunk_mask(slots)
    wide = (SUBLANES, slots * SUBLANES)

    def tokens(i, carry):
        for u in range(PEER_TOKEN_UNROLL):
            t = i * PEER_TOKEN_UNROLL + u
            g = _gather_rows(e_ref, tab_ref, pl.multiple_of(t * slots, slots), slots)
            parts = [jnp.where(mask, jnp.broadcast_to(rep_sc[p, pl.ds(t, 1), :], wide), 0.0).astype(BF16)
                     for p in range(2)]
            r = jnp.dot(jnp.concatenate(parts, axis=0), g, preferred_element_type=F32)
            o_ref[t] = r[0:SUBLANES] + r[SUBLANES:2 * SUBLANES]
        return carry

    lax.fori_loop(0, tb // PEER_TOKEN_UNROLL, tokens, 0)


def peer_b(e_flat, w, tab_u32, slots):
    T = w.shape[0]
    tb = PEER_TOKEN_BLOCK
    return pl.pallas_call(
        functools.partial(_peer_b_kernel, slots=slots),
        grid=(T // tb,),
        in_specs=[pl.BlockSpec((tb * slots,), lambda i: (i,), memory_space=pltpu.SMEM),
                  pl.BlockSpec((tb, slots), lambda i: (i, 0)),
                  pl.BlockSpec(tab_u32.shape, lambda i: (0, 0), pipeline_mode=pl.Buffered(1))],
        out_specs=pl.BlockSpec((tb, SUBLANES, LANES), lambda i: (i, 0, 0)),
        out_shape=jax.ShapeDtypeStruct((T, SUBLANES, LANES), F32),
        scratch_shapes=[pltpu.VMEM((2, tb, slots * SUBLANES), F32)],
        compiler_params=_cparams(1, VMEM_LIMIT_TABLE),
        name="peer_b",
    )(e_flat, w, tab_u32)


def _final_kernel(x_ref, ff_ref, g2_ref, lg_ref, lb_ref, o_ref, *, alpha):
    bb, tl, d = x_ref.shape
    ff = jnp.concatenate([ff_ref[:, c, :] for c in range(d // LANES)], axis=1)
    y = alpha * x_ref[...] + g2_ref[...] * ff.reshape(bb, tl, d)
    o_ref[...] = _ln(y, lg_ref[...], lb_ref[...])


def final_ln(x1, ff, g2, ln_g, ln_b, alpha, bb, tl):
    B, L, D = x1.shape
    tok = lambda b, l: (b, l, 0)
    const = lambda b, l: (0, 0)
    return pl.pallas_call(
        functools.partial(_final_kernel, alpha=alpha),
        grid=(B // bb, L // tl),
        in_specs=[pl.BlockSpec((bb, tl, D), tok),
                  pl.BlockSpec((bb * tl, D // LANES, LANES), lambda b, l: (b * (L // tl) + l, 0, 0)),
                  pl.BlockSpec((bb, 1, D), lambda b, l: (b, 0, 0)),
                  pl.BlockSpec((1, D), const), pl.BlockSpec((1, D), const)],
        out_specs=pl.BlockSpec((bb, tl, D), tok),
        out_shape=jax.ShapeDtypeStruct((B, L, D), F32),
        compiler_params=_cparams(2),
        name="final_ln",
    )(x1, ff, g2, ln_g.reshape(1, D), ln_b.reshape(1, D))


def _rope_tables(past, length, n_groups):
    half = ATT_HEAD_DIM // 2
    inv = 1.0 / (ROPE_THETA ** (jnp.arange(half, dtype=F32) / half))
    ang = (past + jnp.arange(length)).astype(F32)[:, None] * inv[None, :]
    cos = jnp.cos(ang)
    sin = jnp.sin(ang)
    cos_t = jnp.tile(jnp.concatenate([cos, cos], axis=-1), (1, n_groups))
    sin_t = jnp.tile(jnp.concatenate([-sin, sin], axis=-1), (1, n_groups))
    return cos_t, sin_t


def _token_tiles(B, L, target):
    if L >= target:
        return 1, target
    bb = max(1, min(B, target // L))
    while B % bb:
        bb -= 1
    return bb, L


TOPK_TOKEN_BLOCK = 512
PEER_MAX_SPLIT = 8


def _peer_select_and_weigh(h2b, hh, wts, slots):
    T = h2b.shape[0]
    tl = min(T, TOPK_TOKEN_BLOCK)
    wq, keys, tab = wts["w_query"], wts["sub_keys"], wts["u_tab"]
    split = max(s for s in range(1, PEER_MAX_SPLIT + 1) if T % (s * tl) == 0)
    if split == 1 or tl % PEER_TOKEN_BLOCK:
        e, gate = peer_topk(h2b, wq, keys, tl)
        return e, peer_a(e.reshape(T * slots), hh, gate, tab, slots)
    nblk = T // (split * tl)
    chunk = nblk * tl
    e_c, gate_c = peer_topk(h2b, wq, keys, tl, 0, nblk)
    es, ws = [e_c], []
    for c in range(1, split):
        w_c, e_n, gate_n = peer_topk_a(e_c.reshape(chunk * slots), hh, gate_c, tab, h2b, wq, keys, tl,
                                       (c - 1) * nblk, c * nblk)
        ws.append(w_c)
        es.append(e_n)
        e_c, gate_c = e_n, gate_n
    ws.append(peer_a(e_c.reshape(chunk * slots), hh, gate_c, tab, slots, (split - 1) * chunk // PEER_TOKEN_BLOCK))
    return jnp.concatenate(es, axis=0), jnp.concatenate(ws, axis=0)


def _layer(x, mod, conv_hist, k_past, v_past, wts, lam_init, alpha):
    B, L, D = x.shape
    conv_dim = wts["w_dw"].shape[1]
    qk_dim = ATT_HEADS * 2 * ATT_HEAD_DIM
    att_dim = ATT_HEADS * ATT_V_DIM
    width = wts["w_dw"].shape[0]
    assert L >= width - 1 and width - 1 <= HIST_PAD
    sh1, sc1, g1, sh2, sc2, g2 = [m[:, None, :] for m in jnp.split(mod, 6, axis=-1)]
    past = 0 if k_past is None else k_past.shape[1]
    cos, sin = _rope_tables(past, L, qk_dim // ATT_HEAD_DIM)
    bb, tl = _token_tiles(B, L, 512)

    u, k, v, qb, kb, vb = in_proj(x, sh1, sc1, wts["w_in"], wts["b_in"], cos, sin, bb, tl,
                                  conv_dim, qk_dim, att_dim, transposed_v=k_past is None)
    hist_pad = jnp.pad(conv_hist, ((0, 0), (HIST_PAD - (width - 1), 0), (0, 0)))
    conv_out = conv_ln(u, hist_pad, wts["w_dw"], wts["b_dw"], wts["conv_ln_g"], wts["conv_ln_b"], min(L, 256))
    conv_state = u[:, L - (width - 1):, :]

    if k_past is None:
        att = attn_prompt(qb, kb, vb, wts["lam_rows"], wts["subln_g"], lam_init, min(L, 256))
    else:
        att = attn_sample(qb, k_past.reshape(B, past, qk_dim), v_past.reshape(B, past, att_dim), kb, vb,
                          wts["lam_rows"], wts["subln_g"], lam_init)

    x1, hh, h2b = out_proj(conv_out, att, x, g1, sh2, sc2, wts["w_out_conv"], wts["w_out_att"], wts["b_out"],
                           wts["ln1_g"], wts["ln1_b"], alpha, bb, tl)

    T = B * L
    slots = (wts["sub_keys"].shape[0] // 2) * PEER_TOPK
    e, w = _peer_select_and_weigh(h2b.reshape(T, D), hh, wts, slots)
    ff = peer_b(e.reshape(T * slots), w, wts["v_tab"], slots)
    out = final_ln(x1, ff, g2, wts["ln2_g"], wts["ln2_b"], alpha, bb, tl)
    return out, conv_state, k, v


def kernel(x_prompt, x_sample, cache_k, cache_v, cache_conv, c_prompt, c_sample, w_ada, b_ada, w_in, b_in, w_dw, b_dw, conv_ln_g, conv_ln_b, lam_q1, lam_k1, lam_q2, lam_k2, subln_g, w_out, b_out, ln1_g, ln1_b, w_query, sub_keys, u_tab, v_tab, ln2_g, ln2_b):
    depth = w_ada.shape[0]
    D = x_prompt.shape[-1]
    assert D == SUBLANES * LANES
    alpha = (2 * depth) ** 0.25
    Bp, Bs = c_prompt.shape[0], c_sample.shape[0]
    xp, xs = x_prompt, x_sample
    outs = [[] for _ in range(6)]
    for l in range(depth):
        lam_init = 0.8 - 0.6 * math.exp(-0.3 * l)
        conv_dim = w_dw.shape[2]
        n_exp = u_tab.shape[1]
        wts = {
            "w_in": w_in[l].astype(BF16), "b_in": b_in[l], "w_dw": w_dw[l], "b_dw": b_dw[l],
            "conv_ln_g": conv_ln_g[l], "conv_ln_b": conv_ln_b[l],
            "lam_rows": jnp.stack([lam_q1[l], lam_k1[l], lam_q2[l], lam_k2[l]]).astype(F32),
            "subln_g": subln_g[l],
            "w_out_conv": w_out[l, :conv_dim].astype(BF16), "w_out_att": w_out[l, conv_dim:].astype(BF16),
            "b_out": b_out[l], "ln1_g": ln1_g[l], "ln1_b": ln1_b[l],
            "w_query": w_query[l].astype(BF16),
            "sub_keys": sub_keys[l].astype(BF16).reshape(-1, sub_keys.shape[3], sub_keys.shape[4]),
            "u_tab": pack_table(u_tab[l]), "v_tab": pack_table(v_tab[l]),
            "ln2_g": ln2_g[l], "ln2_b": ln2_b[l],
        }
        c_all = jnp.concatenate([c_prompt, c_sample], axis=0)
        pad = (-c_all.shape[0]) % 16
        c_all = jnp.pad(c_all, ((0, pad), (0, 0)))
        mod = ada_mod(c_all, w_ada[l].astype(BF16), b_ada[l])
        hist0 = jnp.zeros((Bp, w_dw.shape[1] - 1, conv_dim), xp.dtype)
        xp, cp, kp, vp = _layer(xp, mod[:Bp], hist0, None, None, wts, lam_init, alpha)
        xs, cs, kn, vn = _layer(xs, mod[Bp:Bp + Bs], cache_conv[l], cache_k[l], cache_v[l], wts, lam_init, alpha)
        for lst, val in zip(outs, (kp, vp, cp, kn, vn, cs)):
            lst.append(val)
    Lp, Ls = x_prompt.shape[1], x_sample.shape[1]
    k_prompt = jnp.stack(outs[0]).reshape(depth, Bp, Lp, ATT_HEADS, 2, ATT_HEAD_DIM)
    v_prompt = jnp.stack(outs[1]).reshape(depth, Bp, Lp, ATT_HEADS, ATT_V_DIM)
    conv_prompt = jnp.stack(outs[2])
    k_sample = jnp.stack(outs[3]).reshape(depth, Bs, Ls, ATT_HEADS, 2, ATT_HEAD_DIM)
    v_sample = jnp.stack(outs[4]).reshape(depth, Bs, Ls, ATT_HEADS, ATT_V_DIM)
    conv_sample = jnp.stack(outs[5])
    return (xp, xs, k_prompt, v_prompt, conv_prompt, k_sample, v_sample, conv_sample)
```
